```python
import jax, jax.numpy as jnp
from jax import lax
import numpy as np

D_MODEL = 1024
BATCH = 8
SEQ = 4096
DEPTH = 1

GRID_W = 64
CTX_LEN = 256

F_GROUP_W = 128
F_WIDTH = D_MODEL // 2
F_GROUPS = F_WIDTH // F_GROUP_W

D_INNER = (3 * D_MODEL) // 2
SSD_HEAD_DIM = 64
SSD_HEADS = D_INNER // SSD_HEAD_DIM
SSD_GROUPS = 4
D_STATE = 128
CONV_W = 3
CHUNK = 128
CONV_CH = D_INNER + 2 * SSD_GROUPS * D_STATE

N_BRANCH = 2
P_IN = F_WIDTH + D_INNER + CONV_CH + 2 * SSD_HEADS + N_BRANCH * D_MODEL

N_EXPERTS = 16
EC_FACTOR = 2
EXPERT_FF = D_MODEL

N_MOD = 6
RMS_EPS = 1e-6

kernel_name = "hybrid_fnet_ssd_ec_moe_dit_block"


def rms_norm(x, w):
    xf = x.astype(jnp.float32)
    y = xf * lax.rsqrt(jnp.mean(xf * xf, axis=-1, keepdims=True) + RMS_EPS)
    return (y * w.astype(jnp.float32)).astype(x.dtype)


def modulate(x, gain, shift, scale):
    return rms_norm(x, gain) * (1 + scale) + shift


def centred_dwconv(u, w, b, n_rows):
    bn, length, ch = u.shape
    row_len = length // n_rows
    r = u.reshape(bn, n_rows, row_len, ch)
    pad = CONV_W // 2
    rp = jnp.pad(r, ((0, 0), (0, 0), (pad, pad), (0, 0)))
    out = sum(rp[:, :, k:k + row_len] * w[k] for k in range(CONV_W)) + b
    return out.reshape(bn, length, ch)


def ssd_chunked(xh, dt, a, bm, cm, h0):
    bn, length, nh, hp = xh.shape
    g, n = bm.shape[2], bm.shape[3]
    hg = nh // g
    nc = length // CHUNK
    x = xh.reshape(bn, nc, CHUNK, g, hg, hp)
    dtc = dt.reshape(bn, nc, CHUNK, g, hg)
    bc = bm.reshape(bn, nc, CHUNK, g, n)
    cc = cm.reshape(bn, nc, CHUNK, g, n)
    acum = jnp.cumsum(dtc * a.reshape(g, hg), axis=2)
    xdt = x * dtc[..., None]
    seg = acum[:, :, :, None] - acum[:, :, None, :]
    lower = jnp.tril(jnp.ones((CHUNK, CHUNK), dtype=bool))[:, :, None, None]
    decay = jnp.where(lower, jnp.exp(jnp.where(lower, seg, 0.0)), 0.0)
    cb = jnp.einsum('bctgn,bcsgn->bctsg', cc, bc)
    y_diag = jnp.einsum('bctsg,bctsgh,bcsghp->bctghp', cb, decay, xdt)
    decay_end = jnp.exp(acum[:, :, -1:] - acum)
    states = jnp.einsum('bcsgn,bcsgh,bcsghp->bcghpn', bc, decay_end, xdt)
    chunk_decay = jnp.exp(acum[:, :, -1])

    def step(h, inp):
        dec, st = inp
        return dec[..., None, None] * h + st, h

    h_final, h_prev = lax.scan(step, h0, (jnp.moveaxis(chunk_decay, 1, 0), jnp.moveaxis(states, 1, 0)))
    h_prev = jnp.moveaxis(h_prev, 0, 1)
    y_off = jnp.einsum('bctgn,bctgh,bcghpn->bctghp', cc, jnp.exp(acum), h_prev)
    y = (y_diag + y_off).reshape(bn, length, nh, hp)
    return y, h_final


def bidir_ssd(xh, dt_raw, bm, cm, dt_bias, a_log, h0_f, h0_b):
    dt_f = jax.nn.softplus(dt_raw[:, :, 0] + dt_bias[0])
    dt_b = jax.nn.softplus(dt_raw[:, :, 1] + dt_bias[1])
    a_f = -jnp.exp(a_log[0].astype(jnp.float32))
    a_b = -jnp.exp(a_log[1].astype(jnp.float32))
    y_f, h_f = ssd_chunked(xh, dt_f, a_f, bm, cm, h0_f)
    flip = lambda t: jnp.flip(t, axis=1)
    y_b, h_b = ssd_chunked(flip(xh), flip(dt_b), a_b, flip(bm), flip(cm), h0_b)
    return y_f + flip(y_b), h_f, h_b


def token_mixer(h, n_rows, h0_f, h0_b, w_in, conv_w, conv_b, dt_bias, a_log, d_skip, ssd_norm,
                w_fourier, w_ssd_out, b_gate, w_out):
    bn, length, _ = h.shape
    proj = h @ w_in
    i1 = F_WIDTH
    i2 = i1 + D_INNER
    i3 = i2 + CONV_CH
    i4 = i3 + 2 * SSD_HEADS
    u_f, z, xbc, dt_raw, gate_logits = jnp.split(proj, [i1, i2, i3, i4], axis=-1)
    uf = u_f.reshape(bn, length, F_GROUPS, F_GROUP_W).astype(jnp.float32)
    f = jnp.fft.fft2(uf, axes=(1, 3), norm='ortho').real.astype(h.dtype).reshape(bn, length, F_WIDTH)
    f_branch = f @ w_fourier
    xbc = jax.nn.silu(centred_dwconv(xbc, conv_w, conv_b, n_rows))
    xs, bm, cm = jnp.split(xbc, [D_INNER, D_INNER + SSD_GROUPS * D_STATE], axis=-1)
    xh = xs.reshape(bn, length, SSD_HEADS, SSD_HEAD_DIM).astype(jnp.float32)
    bm = bm.reshape(bn, length, SSD_GROUPS, D_STATE).astype(jnp.float32)
    cm = cm.reshape(bn, length, SSD_GROUPS, D_STATE).astype(jnp.float32)
    dt_raw = dt_raw.reshape(bn, length, 2, SSD_HEADS).astype(jnp.float32)
    y, h_f, h_b = bidir_ssd(xh, dt_raw, bm, cm, dt_bias, a_log, h0_f, h0_b)
    y = y + d_skip[:, None].astype(jnp.float32) * xh
    y = y.reshape(bn, length, D_INNER).astype(h.dtype)
    y = rms_norm(y * jax.nn.silu(z), ssd_norm)
    s_branch = y @ w_ssd_out
    g = jax.nn.sigmoid(gate_logits.reshape(bn, length, N_BRANCH, D_MODEL) + b_gate)
    merged = g[:, :, 0] * f_branch + g[:, :, 1] * s_branch
    return merged @ w_out, h_f, h_b


def expert_choice_ffn(h, w_router, w_e_gate, w_e_up, w_e_down):
    bn, length, d = h.shape
    cap = EC_FACTOR * length // N_EXPERTS
    aff = jax.nn.softmax((h @ w_router).astype(jnp.float32), axis=-1)
    top_aff, top_idx = lax.top_k(jnp.swapaxes(aff, 1, 2), cap)
    xe = jax.vmap(lambda hb, ib: hb[ib])(h, top_idx)
    hid = jax.nn.silu(jnp.einsum('becd,edf->becf', xe, w_e_gate)) * jnp.einsum('becd,edf->becf', xe, w_e_up)
    ye = jnp.einsum('becf,efd->becd', hid, w_e_down) * top_aff[..., None].astype(h.dtype)
    return jax.vmap(lambda ib, yb: jnp.zeros((length, d), yb.dtype).at[ib.reshape(-1)].add(yb.reshape(-1, d)))(top_idx, ye)


def setup_inputs(seed: int = 0) -> dict:
    key = jax.random.key(seed)
    ks = jax.random.split(key, 32)
    nrm = lambda k, shape, s: jax.random.normal(k, shape, jnp.float32) * s
    L = DEPTH
    dt0 = jnp.exp(jax.random.uniform(ks[12], (L, 2, SSD_HEADS), jnp.float32, np.log(1e-3), np.log(1e-1)))
    return {
        'x': nrm(ks[0], (BATCH, SEQ, D_MODEL), 1.0),
        'c': nrm(ks[1], (BATCH, D_MODEL), 1.0),
        'ctx': nrm(ks[2], (BATCH, CTX_LEN, D_MODEL), 1.0),
        'c_ctx': nrm(ks[3], (D_MODEL,), 1.0),
        'w_mod': nrm(ks[4], (L, D_MODEL, N_MOD * D_MODEL), 0.5 * D_MODEL ** -0.5),
        'b_mod': nrm(ks[5], (L, N_MOD * D_MODEL), 0.02),
        'norm_mix_pre': 1.0 + nrm(ks[6], (L, D_MODEL), 0.05),
        'norm_mix_post': 1.0 + nrm(ks[7], (L, D_MODEL), 0.05),
        'norm_ffn_pre': 1.0 + nrm(ks[8], (L, D_MODEL), 0.05),
        'norm_ffn_post': 1.0 + nrm(ks[9], (L, D_MODEL), 0.05),
        'w_in': nrm(ks[10], (L, D_MODEL, P_IN), D_MODEL ** -0.5),
        'conv_w': nrm(ks[11], (L, CONV_W, CONV_CH), CONV_W ** -0.5),
        'conv_b': nrm(ks[13], (L, CONV_CH), 0.02),
        'dt_bias': dt0 + jnp.log(-jnp.expm1(-dt0)),
        'a_log': jnp.log(jax.random.uniform(ks[14], (L, 2, SSD_HEADS), jnp.float32, 1.0, 16.0)),
        'd_skip': 1.0 + nrm(ks[15], (L, SSD_HEADS), 0.1),
        'ssd_norm': 1.0 + nrm(ks[16], (L, D_INNER), 0.05),
        'w_fourier': nrm(ks[17], (L, F_WIDTH, D_MODEL), F_WIDTH ** -0.5),
        'w_ssd_out': nrm(ks[18], (L, D_INNER, D_MODEL), D_INNER ** -0.5),
        'b_gate': nrm(ks[19], (L, N_BRANCH, D_MODEL), 0.1),
        'w_out': nrm(ks[20], (L, D_MODEL, D_MODEL), D_MODEL ** -0.5),
        'w_router': nrm(ks[21], (L, D_MODEL, N_EXPERTS), D_MODEL ** -0.5),
        'w_e_gate': nrm(ks[22], (L, N_EXPERTS, D_MODEL, EXPERT_FF), D_MODEL ** -0.5),
        'w_e_up': nrm(ks[23], (L, N_EXPERTS, D_MODEL, EXPERT_FF), D_MODEL ** -0.5),
        'w_e_down': nrm(ks[24], (L, N_EXPERTS, EXPERT_FF, D_MODEL), EXPERT_FF ** -0.5),
    }


def reference(x, c, ctx, c_ctx, w_mod, b_mod, norm_mix_pre, norm_mix_post, norm_ffn_pre, norm_ffn_post,
              w_in, conv_w, conv_b, dt_bias, a_log, d_skip, ssd_norm, w_fourier, w_ssd_out, b_gate, w_out,
              w_router, w_e_gate, w_e_up, w_e_down):
    bn, length, d = x.shape
    rows = length // GRID_W
    hx, hc = x, ctx
    for l in range(DEPTH):
        mod_x = (jax.nn.silu(c) @ w_mod[l] + b_mod[l]).reshape(bn, N_MOD, 1, d)
        mod_c = (jax.nn.silu(c_ctx) @ w_mod[l] + b_mod[l]).reshape(N_MOD, d)
        sh1, sc1, g1, sh2, sc2, g2 = [mod_x[:, i] for i in range(N_MOD)]
        csh1, csc1, cg1, csh2, csc2, cg2 = [mod_c[i] for i in range(N_MOD)]
        mix_w = (w_in[l], conv_w[l], conv_b[l], dt_bias[l], a_log[l], d_skip[l], ssd_norm[l],
                 w_fourier[l], w_ssd_out[l], b_gate[l], w_out[l])
        h0 = jnp.zeros((bn, SSD_GROUPS, SSD_HEADS // SSD_GROUPS, SSD_HEAD_DIM, D_STATE), jnp.float32)
        mix_c, hf_c, hb_c = token_mixer(modulate(hc, norm_mix_pre[l], csh1, csc1), 1, h0, h0, *mix_w)
        mix_x, _, _ = token_mixer(modulate(hx, norm_mix_pre[l], sh1, sc1), rows, hf_c, hb_c, *mix_w)
        hx = hx + g1 * rms_norm(mix_x, norm_mix_post[l])
        ffn_x = expert_choice_ffn(modulate(hx, norm_ffn_pre[l], sh2, sc2), w_router[l], w_e_gate[l], w_e_up[l], w_e_down[l])
        hx = hx + g2 * rms_norm(ffn_x, norm_ffn_post[l])
        if l + 1 < DEPTH:
            hc = hc + cg1 * rms_norm(mix_c, norm_mix_post[l])
            ffn_c = expert_choice_ffn(modulate(hc, norm_ffn_pre[l], csh2, csc2), w_router[l], w_e_gate[l], w_e_up[l], w_e_down[l])
            hc = hc + cg2 * rms_norm(ffn_c, norm_ffn_post[l])
    return hx
```

```python
import functools

import numpy as np
import jax
import jax.numpy as jnp
from jax import lax
from jax.experimental import pallas as pl
from jax.experimental.pallas import tpu as pltpu

F32 = jnp.float32
BF16 = jnp.bfloat16
HIGHEST = lax.Precision.HIGHEST

D_MODEL = 1024
GRID_W = 64
F_GROUP_W = 128
F_WIDTH = 512
D_INNER = 1536
HEAD_DIM = 64
N_HEADS = 24
N_GROUPS = 4
HEADS_PER_GROUP = 6
D_STATE = 128
CHUNK = 128
BC_WIDTH = N_GROUPS * D_STATE
N_EXPERTS = 16
EC_FACTOR = 2
N_MOD = 6
RMS_EPS = 1e-6
GROUP_COLS = HEADS_PER_GROUP * HEAD_DIM

FFT_N1 = 128
FFT_N2 = 32
FFT_KB = 8

VMEM_LIMIT = 56 * 1024 * 1024


def _cparams(n_axes, vmem=None):
    return pltpu.CompilerParams(dimension_semantics=("arbitrary",) * n_axes,
                                vmem_limit_bytes=vmem)


def _const_spec(shape):
    nd = len(shape)
    return pl.BlockSpec(shape, lambda *_: (0,) * nd, pipeline_mode=pl.Buffered(1))


def _rms(x, w):
    ms = jnp.mean(x * x, axis=-1, keepdims=True)
    return x * lax.rsqrt(ms + RMS_EPS) * w


def _sigmoid(x):
    return 1.0 / (1.0 + jnp.exp(-x))


def _silu(x):
    return x * _sigmoid(x)


def _mod_kernel(c_ref, w_ref, b_ref, o_ref):
    c = c_ref[...]
    o_ref[...] = jnp.dot(_silu(c), w_ref[...], precision=HIGHEST,
                         preferred_element_type=F32) + b_ref[...]


def _modulation(cs, w_mod, b_mod):
    rows = cs.shape[0]
    n = w_mod.shape[1]
    blk = D_MODEL
    return pl.pallas_call(
        _mod_kernel,
        out_shape=jax.ShapeDtypeStruct((rows, n), F32),
        grid=(n // blk,),
        in_specs=[pl.BlockSpec((rows, D_MODEL), lambda j: (0, 0)),
                  pl.BlockSpec((D_MODEL, blk), lambda j: (0, j)),
                  pl.BlockSpec((1, blk), lambda j: (0, j))],
        out_specs=pl.BlockSpec((rows, blk), lambda j: (0, j)),
        compiler_params=_cparams(1),
        name="modulation",
    )(cs, w_mod, b_mod.reshape(1, n))


def _conv_silu(xbc, cw_ref, cb_ref, row_len):
    tm = xbc.shape[0]
    pos = lax.broadcasted_iota(jnp.int32, (tm, 1), 0) % row_len
    prev = jnp.where(pos == 0, 0.0, pltpu.roll(xbc, 1, axis=0))
    nxt = jnp.where(pos == row_len - 1, 0.0, pltpu.roll(xbc, tm - 1, axis=0))
    out = prev * cw_ref[0:1, :] + xbc * cw_ref[1:2, :] + nxt * cw_ref[2:3, :] + cb_ref[...]
    return _silu(out)


def _softplus(x):
    return jnp.maximum(x, 0.0) + jnp.log(1.0 + jnp.exp(-jnp.abs(x)))


def _inproj_kernel(x_ref, sh_ref, sc_ref, gain_ref, wuf_ref, wz_ref, wxs_ref, wb_ref, wc_ref,
                   wdt_ref, wg_ref, dft_ref, cwx_ref, cbx_ref, cwb_ref, cbb_ref, cwc_ref, cbc_ref,
                   dtb_ref, bg_ref,
                   zr_ref, zi_ref, z_ref, xs_ref, bm_ref, cm_ref, dt_ref, g_ref, *, row_len):
    x = x_ref[0]
    h = _rms(x, gain_ref[...]) * (1.0 + sc_ref[0]) + sh_ref[0]
    hb = h.astype(BF16)
    dot = functools.partial(jnp.dot, preferred_element_type=F32)

    xs_ref[0] = _conv_silu(dot(hb, wxs_ref[...]), cwx_ref, cbx_ref, row_len).astype(BF16)
    bm_ref[0] = _conv_silu(dot(hb, wb_ref[...]), cwb_ref, cbb_ref, row_len).astype(BF16)
    cm_ref[0] = _conv_silu(dot(hb, wc_ref[...]), cwc_ref, cbc_ref, row_len).astype(BF16)
    dt_ref[0] = _softplus(dot(hb, wdt_ref[...]) + dtb_ref[...])
    if zr_ref is None:
        return
    uf = dot(hb, wuf_ref[...]).astype(BF16)
    for g in range(F_WIDTH // F_GROUP_W):
        sl = slice(g * F_GROUP_W, (g + 1) * F_GROUP_W)
        zz = dot(uf[:, sl], dft_ref[...].astype(BF16))
        zr_ref[0, :, sl] = zz[:, :F_GROUP_W].astype(BF16)
        zi_ref[0, :, sl] = zz[:, F_GROUP_W:].astype(BF16)
    z_ref[0] = dot(hb, wz_ref[...]).astype(BF16)
    g_ref[0] = _sigmoid(dot(hb, wg_ref[...]) + bg_ref[...]).astype(BF16)


def _inproj_ctx_kernel(x_ref, sh_ref, sc_ref, gain_ref, wxs_ref, wb_ref, wc_ref, wdt_ref,
                       cwx_ref, cbx_ref, cwb_ref, cbb_ref, cwc_ref, cbc_ref, dtb_ref,
                       xs_ref, bm_ref, cm_ref, dt_ref, *, row_len):
    _inproj_kernel(x_ref, sh_ref, sc_ref, gain_ref, None, None, wxs_ref, wb_ref, wc_ref,
                   wdt_ref, None, None, cwx_ref, cbx_ref, cwb_ref, cbb_ref, cwc_ref, cbc_ref,
                   dtb_ref, None, None, None, None, xs_ref, bm_ref, cm_ref, dt_ref, None,
                   row_len=row_len)


def _in_projection(x, shift, scale, gain, w, row_len, tm, full):
    bn, length, d = x.shape
    grid = (bn, length // tm)
    tok = lambda width: pl.BlockSpec((1, tm, width), lambda b, i: (b, i, 0))
    vec = pl.BlockSpec((1, 1, d), lambda b, i: (b, 0, 0))
    out = lambda width, dt: jax.ShapeDtypeStruct((bn, length, width), dt)
    conv_ops = [w["cwx"], w["cbx"], w["cwb"], w["cbb"], w["cwc"], w["cbc"]]
    ssd_shapes = [out(D_INNER, BF16), out(BC_WIDTH, BF16), out(BC_WIDTH, BF16), out(256, F32)]
    ssd_specs = [tok(D_INNER), tok(BC_WIDTH), tok(BC_WIDTH), tok(256)]
    if full:
        ops = [x, shift, scale, gain, w["wuf"], w["wz"], w["wxs"], w["wb"], w["wc"], w["wdt"], w["wg"],
               w["dft"]] + conv_ops + [w["dtb"], w["bg"]]
        kern = functools.partial(_inproj_kernel, row_len=row_len)
        out_shape = [out(F_WIDTH, BF16), out(F_WIDTH, BF16), out(D_INNER, BF16)] + ssd_shapes + \
                    [out(2 * D_MODEL, BF16)]
        out_specs = [tok(F_WIDTH), tok(F_WIDTH), tok(D_INNER)] + ssd_specs + [tok(2 * D_MODEL)]
        name = "in_projection"
    else:
        ops = [x, shift, scale, gain, w["wxs"], w["wb"], w["wc"], w["wdt"]] + conv_ops + [w["dtb"]]
        kern = functools.partial(_inproj_ctx_kernel, row_len=row_len)
        out_shape = ssd_shapes
        out_specs = ssd_specs
        name = "in_projection_ctx"
    in_specs = [tok(d), vec, vec] + [_const_spec(o.shape) for o in ops[3:]]
    return pl.pallas_call(kern, out_shape=out_shape, grid=grid, in_specs=in_specs,
                          out_specs=out_specs, compiler_params=_cparams(2, VMEM_LIMIT),
                          name=name)(*ops)


def _dft_constants():
    j = np.arange(F_GROUP_W)
    ang = 2.0 * np.pi * np.outer(j, j) / F_GROUP_W
    dft_c = np.concatenate([np.cos(ang), -np.sin(ang)], axis=1)
    k1 = np.arange(FFT_N1)
    a1 = 2.0 * np.pi * np.outer(k1, k1) / FFT_N1
    c1, s1 = np.cos(a1), np.sin(a1)
    w1 = np.block([[c1, s1], [-s1, c1]])
    n2 = np.arange(FFT_N2)
    at = 2.0 * np.pi * np.outer(k1, n2) / (FFT_N1 * FFT_N2)
    tw_c = np.repeat(np.cos(at), 128, axis=1)
    tw_s = np.repeat(np.sin(at), 128, axis=1)
    a2 = 2.0 * np.pi * np.outer(n2, n2) / FFT_N2
    w2 = np.zeros((FFT_N2, FFT_KB, 2, FFT_KB, FFT_N2))
    for jj in range(FFT_KB):
        w2[:, jj, 0, jj, :] = np.cos(a2)
        w2[:, jj, 1, jj, :] = np.sin(a2)
    w2 = w2.reshape(FFT_N2 * FFT_KB, 2 * FFT_KB * FFT_N2)
    return (jnp.asarray(dft_c, F32), jnp.asarray(w1, F32), jnp.asarray(tw_c, F32),
            jnp.asarray(tw_s, F32), jnp.asarray(w2, F32))


FFT_COLS = 4 * F_WIDTH


def _fft1_kernel(zr_ref, zi_ref, w1_ref, twc_ref, tws_ref, tr_ref, ti_ref):
    rhs = jnp.concatenate([zr_ref[0], zi_ref[0]], axis=0)
    t = jnp.dot(w1_ref[...].astype(BF16), rhs, preferred_element_type=F32)
    tr, ti = t[:FFT_N1], t[FFT_N1:]
    reps = F_WIDTH // 128
    for q in range(FFT_COLS // F_WIDTH):
        cw = jnp.concatenate([twc_ref[:, q * 128:(q + 1) * 128]] * reps, axis=1)
        sw = jnp.concatenate([tws_ref[:, q * 128:(q + 1) * 128]] * reps, axis=1)
        sl = slice(q * F_WIDTH, (q + 1) * F_WIDTH)
        a, b = tr[:, sl], ti[:, sl]
        tr_ref[0, :, sl] = (a * cw + b * sw).astype(BF16)
        ti_ref[0, :, sl] = (b * cw - a * sw).astype(BF16)


def _fft2_kernel(tr_ref, ti_ref, w2_ref, o_ref, *, scale):
    rhs = jnp.concatenate([tr_ref[0], ti_ref[0]], axis=0)
    y = jnp.dot(w2_ref[...].astype(BF16), rhs, preferred_element_type=F32) * scale
    for k2 in range(FFT_N2):
        o_ref[0, k2] = y[k2 * FFT_KB:(k2 + 1) * FFT_KB]


def _sequence_dft_real(zr, zi, w1, tw_c, tw_s, w2):
    bn, length, width = zr.shape
    cols = FFT_N2 * width
    zr2 = zr.reshape(bn, FFT_N1, cols)
    zi2 = zi.reshape(bn, FFT_N1, cols)
    blk = pl.BlockSpec((1, FFT_N1, FFT_COLS), lambda b, i: (b, 0, i))
    twb = pl.BlockSpec((FFT_N1, 4 * 128), lambda b, i: (0, i))
    tr, ti = pl.pallas_call(
        _fft1_kernel,
        out_shape=[jax.ShapeDtypeStruct((bn, FFT_N1, cols), BF16)] * 2,
        grid=(bn, cols // FFT_COLS),
        in_specs=[blk, blk, _const_spec(w1.shape), twb, twb],
        out_specs=[blk, blk],
        compiler_params=_cparams(2),
        name="seq_dft_stage1",
    )(zr2, zi2, w1, tw_c, tw_s)
    tr = tr.reshape(bn, length, width)
    ti = ti.reshape(bn, length, width)
    rows = FFT_KB * FFT_N2
    tb = pl.BlockSpec((1, rows, width), lambda b, k: (b, k, 0))
    scale = 1.0 / float(np.sqrt(length * F_GROUP_W))
    f = pl.pallas_call(
        functools.partial(_fft2_kernel, scale=scale),
        out_shape=jax.ShapeDtypeStruct((bn, FFT_N2, FFT_N1, width), F32),
        grid=(bn, FFT_N1 // FFT_KB),
        in_specs=[tb, tb, _const_spec(w2.shape)],
        out_specs=pl.BlockSpec((1, FFT_N2, FFT_KB, width), lambda b, k: (b, 0, k, 0)),
        compiler_params=_cparams(2),
        name="seq_dft_stage2",
    )(tr, ti, w2)
    return f.reshape(bn, length, width)


def _ssd_kernel(xs_ref, bm_ref, cm_ref, dt_ref, alog_ref, tri_ref, h0_ref, y_ref, hfin_ref, st_ref,
                *, nchunks):
    c = pl.program_id(2)

    @pl.when(c == 0)
    def _():
        st_ref[...] = h0_ref[0, 0]

    q = CHUNK
    tri = tri_ref[0]
    mask = tri > 0.5
    a = -jnp.exp(alog_ref[0])
    dt = dt_ref[0]
    da = dt * a
    acum = jnp.dot(tri, da, precision=HIGHEST, preferred_element_type=F32)
    acum_t = acum.T
    alast = jnp.sum(da, axis=0, keepdims=True)
    lo = lax.broadcasted_iota(jnp.int32, (q, 128), 1) < HEAD_DIM
    lo_row = lo[0:1]
    col = lambda m, i: jnp.broadcast_to(m[:, i:i + 1], (q, 128))

    for g in range(N_GROUPS):
        gs = slice(g * D_STATE, (g + 1) * D_STATE)
        bg = bm_ref[0, :, gs]
        cg = cm_ref[0, :, gs]
        cb = lax.dot_general(cg, bg, (((1,), (1,)), ((), ())), preferred_element_type=F32)
        h_t = st_ref[g]
        yoff = jnp.dot(cg, h_t.astype(BF16), preferred_element_type=F32)
        xw_parts = []
        cd_parts = []
        for pr in range(HEADS_PER_GROUP // 2):
            h0 = g * HEADS_PER_GROUP + 2 * pr
            h1 = h0 + 1
            ps = slice((g * 3 + pr) * 128, (g * 3 + pr + 1) * 128)
            ca0, ca1 = col(acum, h0), col(acum, h1)
            ra0 = jnp.broadcast_to(acum_t[h0:h0 + 1, :], (q, q))
            ra1 = jnp.broadcast_to(acum_t[h1:h1 + 1, :], (q, q))
            m0 = (jnp.exp(jnp.where(mask, ca0 - ra0, -1e30)) * cb).astype(BF16)
            m1 = (jnp.exp(jnp.where(mask, ca1 - ra1, -1e30)) * cb).astype(BF16)
            xdt = xs_ref[0, :, ps].astype(F32) * jnp.where(lo, col(dt, h0), col(dt, h1))
            xdt0 = jnp.where(lo, xdt, 0.0).astype(BF16)
            xdt1 = jnp.where(lo, 0.0, xdt).astype(BF16)
            ydiag = (jnp.dot(m0, xdt0, preferred_element_type=F32)
                     + jnp.dot(m1, xdt1, preferred_element_type=F32))
            cap = jnp.where(lo, ca0, ca1)
            y = ydiag + yoff[:, pr * 128:(pr + 1) * 128] * jnp.exp(cap)
            y_ref[0, 0, :, ps] = y.astype(BF16)
            alp = jnp.where(lo_row, jnp.broadcast_to(alast[:, h0:h0 + 1], (1, 128)),
                            jnp.broadcast_to(alast[:, h1:h1 + 1], (1, 128)))
            xw_parts.append((xdt * jnp.exp(alp - cap)).astype(BF16))
            cd_parts.append(jnp.exp(alp))
        xw = jnp.concatenate(xw_parts, axis=1)
        cd = jnp.concatenate(cd_parts, axis=1)
        bg_t = bg.astype(F32).T.astype(BF16)
        st_ref[g] = h_t * cd + jnp.dot(bg_t, xw, preferred_element_type=F32)

    @pl.when(c == nchunks - 1)
    def _():
        hfin_ref[0, 0] = st_ref[...]


def _ssd_scan(xs, bm, cm, dt, alog_pad, tri, h0):
    bn, length, _ = xs.shape
    nchunks = length // CHUNK
    cc = lambda d, c: c + d * (nchunks - 1 - 2 * c)
    tok = lambda width: pl.BlockSpec((1, CHUNK, width), lambda d, b, c: (b, cc(d, c), 0))
    st_shape = (N_GROUPS, D_STATE, GROUP_COLS)
    st_spec = pl.BlockSpec((1, 1) + st_shape, lambda d, b, c: (d, b, 0, 0, 0))
    return pl.pallas_call(
        functools.partial(_ssd_kernel, nchunks=nchunks),
        out_shape=[jax.ShapeDtypeStruct((2, bn, length, D_INNER), BF16),
                   jax.ShapeDtypeStruct((2, bn) + st_shape, F32)],
        grid=(2, bn, nchunks),
        in_specs=[tok(D_INNER), tok(BC_WIDTH), tok(BC_WIDTH),
                  pl.BlockSpec((1, CHUNK, 128), lambda d, b, c: (b, cc(d, c), d)),
                  pl.BlockSpec((1, 1, 128), lambda d, b, c: (d, 0, 0)),
                  pl.BlockSpec((1, CHUNK, CHUNK), lambda d, b, c: (d, 0, 0)),
                  st_spec],
        out_specs=[pl.BlockSpec((1, 1, CHUNK, D_INNER), lambda d, b, c: (d, b, cc(d, c), 0)),
                   st_spec],
        scratch_shapes=[pltpu.VMEM(st_shape, F32)],
        compiler_params=_cparams(3),
        name="ssd_scan",
    )(xs, bm, cm, dt, alog_pad, tri, h0)


def _mix_kernel(yf_ref, yb_ref, xs_ref, z_ref, f_ref, g_ref, x_ref, g1_ref, sh2_ref, sc2_ref,
                dsk_ref, nssd_ref, npost_ref, npre_ref, wso_ref, wf_ref, wo_ref, wr_ref,
                hx_ref, hm_ref, aff_ref):
    dot = functools.partial(jnp.dot, preferred_element_type=F32)
    y = yf_ref[0, 0].astype(F32) + yb_ref[0, 0].astype(F32) + dsk_ref[...] * xs_ref[0].astype(F32)
    y = _rms(y * _silu(z_ref[0].astype(F32)), nssd_ref[...])
    s_branch = dot(y.astype(BF16), wso_ref[...])
    f_branch = dot(f_ref[0].astype(BF16), wf_ref[...])
    gates = g_ref[0].astype(F32)
    merged = gates[:, :D_MODEL] * f_branch + gates[:, D_MODEL:] * s_branch
    mix = dot(merged.astype(BF16), wo_ref[...])
    hx = x_ref[0] + g1_ref[0] * _rms(mix, npost_ref[...])
    hx_ref[0] = hx
    hm = _rms(hx, npre_ref[...]) * (1.0 + sc2_ref[0]) + sh2_ref[0]
    hm_ref[0] = hm
    logits = jnp.dot(hm, wr_ref[...], precision=HIGHEST, preferred_element_type=F32)
    e = jnp.exp(logits - jnp.max(logits, axis=-1, keepdims=True))
    aff_ref[0] = e / jnp.sum(e, axis=-1, keepdims=True)


def _mix(y2, xs, z, f, gates, x, g1, sh2, sc2, consts, tm):
    bn, length, d = x.shape
    tok = lambda width: pl.BlockSpec((1, tm, width), lambda b, i: (b, i, 0))
    ydir = lambda k: pl.BlockSpec((1, 1, tm, D_INNER), lambda b, i: (k, b, i, 0))
    vec = pl.BlockSpec((1, 1, d), lambda b, i: (b, 0, 0))
    return pl.pallas_call(
        _mix_kernel,
        out_shape=[jax.ShapeDtypeStruct((bn, length, d), F32),
                   jax.ShapeDtypeStruct((bn, length, d), F32),
                   jax.ShapeDtypeStruct((bn, length, N_EXPERTS), F32)],
        grid=(bn, length // tm),
        in_specs=[ydir(0), ydir(1), tok(D_INNER), tok(D_INNER), tok(F_WIDTH), tok(2 * D_MODEL), tok(d),
                  vec, vec, vec] + [_const_spec(c.shape) for c in consts],
        out_specs=[tok(d), tok(d), tok(N_EXPERTS)],
        compiler_params=_cparams(2, VMEM_LIMIT),
        name="merge_out_router",
    )(y2, y2, xs, z, f, gates, x, g1, sh2, sc2, *consts)


def _expert_kernel(idx_ref, val_ref, hm_ref, wg_ref, wu_ref, wd_ref, ye_ref, xbuf, sem, *, cap):
    b = pl.program_id(1)

    def row_copy(j, r):
        return pltpu.make_async_copy(hm_ref.at[b, pl.ds(r, 1), :], xbuf.at[pl.ds(j, 1), :], sem)

    def issue(j, carry):
        row_copy(j, idx_ref[0, 0, j]).start()
        return carry

    def wait(j, carry):
        row_copy(j, 0).wait()
        return carry

    lax.fori_loop(0, cap, issue, 0)
    lax.fori_loop(0, cap, wait, 0)
    dot = functools.partial(jnp.dot, preferred_element_type=F32)
    x = xbuf[...].astype(BF16)
    hid = (_silu(dot(x, wg_ref[0])) * dot(x, wu_ref[0])).astype(BF16)
    ye_ref[0] = dot(hid, wd_ref[0]) * val_ref[0]


def _experts(idx, vals, hm, wg, wu, wd):
    bn, length, d = hm.shape
    ne, _, ff = wg.shape
    cap = idx.shape[-1]
    sel = lambda e, b: (b * ne + e, 0, 0)
    wspec = pl.BlockSpec((1, d, ff), lambda e, b: (e, 0, 0))
    return pl.pallas_call(
        functools.partial(_expert_kernel, cap=cap),
        out_shape=jax.ShapeDtypeStruct((bn * ne, cap, d), F32),
        grid=(ne, bn),
        in_specs=[pl.BlockSpec((1, 1, cap), sel, memory_space=pltpu.SMEM),
                  pl.BlockSpec((1, cap, 1), sel),
                  pl.BlockSpec(memory_space=pl.ANY),
                  wspec, wspec, pl.BlockSpec((1, ff, d), lambda e, b: (e, 0, 0))],
        out_specs=pl.BlockSpec((1, cap, d), sel),
        scratch_shapes=[pltpu.VMEM((cap, d), F32), pltpu.SemaphoreType.DMA(())],
        compiler_params=_cparams(2, VMEM_LIMIT),
        name="expert_ffn",
    )(idx, vals, hm, wg, wu, wd)


def _combine_kernel(idx_ref, ye_ref, o_ref, *, cap):
    @pl.when(pl.program_id(1) == 0)
    def _():
        o_ref[...] = jnp.zeros_like(o_ref)

    def body(j, carry):
        r = idx_ref[0, 0, j]
        o_ref[0, pl.ds(r, 1), :] += ye_ref[0, pl.ds(j, 1), :]
        return carry

    lax.fori_loop(0, cap, body, 0, unroll=8)


def _combine(idx, ye, bn, length, d):
    ne = N_EXPERTS
    cap = idx.shape[-1]
    sel = lambda b, e: (b * ne + e, 0, 0)
    return pl.pallas_call(
        functools.partial(_combine_kernel, cap=cap),
        out_shape=jax.ShapeDtypeStruct((bn, length, d), F32),
        grid=(bn, ne),
        in_specs=[pl.BlockSpec((1, 1, cap), sel, memory_space=pltpu.SMEM),
                  pl.BlockSpec((1, cap, d), sel)],
        out_specs=pl.BlockSpec((1, length, d), lambda b, e: (b, 0, 0)),
        compiler_params=_cparams(2, VMEM_LIMIT),
        name="expert_combine",
    )(idx, ye)


def _final_kernel(hx_ref, ffn_ref, g2_ref, w_ref, o_ref):
    o_ref[0] = hx_ref[0] + g2_ref[0] * _rms(ffn_ref[0], w_ref[...])


def _final(hx, ffn, g2, w, tm):
    bn, length, d = hx.shape
    tok = pl.BlockSpec((1, tm, d), lambda b, i: (b, i, 0))
    return pl.pallas_call(
        _final_kernel,
        out_shape=jax.ShapeDtypeStruct((bn, length, d), F32),
        grid=(bn, length // tm),
        in_specs=[tok, tok, pl.BlockSpec((1, 1, d), lambda b, i: (b, 0, 0)), _const_spec(w.shape)],
        out_specs=tok,
        compiler_params=_cparams(2),
        name="ffn_residual",
    )(hx, ffn, g2, w)


def kernel(x, c, ctx, c_ctx, w_mod, b_mod, norm_mix_pre, norm_mix_post, norm_ffn_pre, norm_ffn_post,
           w_in, conv_w, conv_b, dt_bias, a_log, d_skip, ssd_norm, w_fourier, w_ssd_out, b_gate, w_out,
           w_router, w_e_gate, w_e_up, w_e_down):
    bn, length, d = x.shape
    ctx_len = ctx.shape[1]
    l = 0
    row = lambda v: v.reshape(1, -1)

    cs = jnp.concatenate([c, c_ctx[None], jnp.zeros((7, d), F32)], axis=0)
    mods = _modulation(cs, w_mod[l], b_mod[l])
    mod_x = mods[:bn].reshape(bn, N_MOD, 1, d)
    mod_c = jnp.broadcast_to(mods[bn].reshape(1, N_MOD, 1, d), (bn, N_MOD, 1, d))
    sh1, sc1, g1, sh2, sc2, g2 = [mod_x[:, i] for i in range(N_MOD)]

    i1 = F_WIDTH
    i2 = i1 + D_INNER
    i3 = i2 + D_INNER
    i4 = i3 + BC_WIDTH
    i5 = i4 + BC_WIDTH
    i6 = i5 + 2 * N_HEADS
    wi = w_in[l]
    wdt = jnp.zeros((d, 256), F32).at[:, :N_HEADS].set(wi[:, i5:i5 + N_HEADS])
    wdt = wdt.at[:, 128:128 + N_HEADS].set(wi[:, i5 + N_HEADS:i6])
    dtb = jnp.zeros((1, 256), F32).at[0, :N_HEADS].set(dt_bias[l, 0]).at[0, 128:128 + N_HEADS].set(dt_bias[l, 1])
    dft_c, w1, tw_c, tw_s, w2 = _dft_constants()
    cw, cb = conv_w[l], conv_b[l]
    j1, j2 = D_INNER, D_INNER + BC_WIDTH
    w = dict(
        wuf=wi[:, :i1].astype(BF16), wz=wi[:, i1:i2].astype(BF16), wxs=wi[:, i2:i3].astype(BF16),
        wb=wi[:, i3:i4].astype(BF16), wc=wi[:, i4:i5].astype(BF16), wdt=wdt.astype(BF16),
        wg=wi[:, i6:].astype(BF16), dft=dft_c,
        cwx=cw[:, :j1], cbx=row(cb[:j1]), cwb=cw[:, j1:j2], cbb=row(cb[j1:j2]),
        cwc=cw[:, j2:], cbc=row(cb[j2:]), dtb=dtb, bg=row(b_gate[l]))
    gain_pre = row(norm_mix_pre[l])

    alog_pad = jnp.zeros((2, 1, 128), F32).at[:, 0, :N_HEADS].set(a_log[l])
    r = np.arange(CHUNK)
    tri = jnp.asarray(np.stack([r[:, None] >= r[None, :], r[:, None] <= r[None, :]]), F32)

    cxs, cbm, ccm, cdt = _in_projection(ctx, mod_c[:, 0], mod_c[:, 1], gain_pre, w, ctx_len, ctx_len, False)
    h_zero = jnp.zeros((2, bn, N_GROUPS, D_STATE, GROUP_COLS), F32)
    _, h_ctx = _ssd_scan(cxs, cbm, ccm, cdt, alog_pad, tri, h_zero)

    zr, zi, z, xs, bm, cm, dt, gates = _in_projection(x, sh1, sc1, gain_pre, w, GRID_W, 256, True)
    f = _sequence_dft_real(zr, zi, w1, tw_c, tw_s, w2)
    y2, _ = _ssd_scan(xs, bm, cm, dt, alog_pad, tri, h_ctx)

    consts = [row(jnp.repeat(d_skip[l], HEAD_DIM)), row(ssd_norm[l]), row(norm_mix_post[l]),
              row(norm_ffn_pre[l]), w_ssd_out[l].astype(BF16), w_fourier[l].astype(BF16),
              w_out[l].astype(BF16), w_router[l]]
    hx, hm, aff = _mix(y2, xs, z, f, gates, x, g1, sh2, sc2, consts, 256)

    cap = EC_FACTOR * length // N_EXPERTS
    top_aff, top_idx = lax.top_k(jnp.swapaxes(aff, 1, 2), cap)
    idx = top_idx.reshape(bn * N_EXPERTS, 1, cap).astype(jnp.int32)
    vals = top_aff.reshape(bn * N_EXPERTS, cap, 1)
    ye = _experts(idx, vals, hm, w_e_gate[l].astype(BF16), w_e_up[l].astype(BF16),
                  w_e_down[l].astype(BF16))
    ffn = _combine(idx, ye, bn, length, d)
    return _final(hx, ffn, g2, row(norm_ffn_post[l]), 512)
```

```python
import functools

import numpy as np
import jax
import jax.numpy as jnp
from jax import lax
from jax.experimental import pallas as pl
from jax.experimental.pallas import tpu as pltpu

F32 = jnp.float32
BF16 = jnp.bfloat16
HIGHEST = lax.Precision.HIGHEST

D_MODEL = 1024
GRID_W = 64
F_GROUP_W = 128
F_WIDTH = 512
D_INNER = 1536
HEAD_DIM = 64
N_HEADS = 24
N_GROUPS = 4
HEADS_PER_GROUP = 6
D_STATE = 128
CHUNK = 128
BC_WIDTH = N_GROUPS * D_STATE
N_EXPERTS = 16
EC_FACTOR = 2
N_MOD = 6
RMS_EPS = 1e-6
GROUP_COLS = HEADS_PER_GROUP * HEAD_DIM

FFT_N1 = 128
FFT_N2 = 32
FFT_KB = 8

VMEM_LIMIT = 56 * 1024 * 1024


def _cparams(n_axes, vmem=None):
    return pltpu.CompilerParams(dimension_semantics=("arbitrary",) * n_axes,
                                vmem_limit_bytes=vmem)


def _const_spec(shape):
    nd = len(shape)
    return pl.BlockSpec(shape, lambda *_: (0,) * nd, pipeline_mode=pl.Buffered(1))


def _rms(x, w):
    ms = jnp.mean(x * x, axis=-1, keepdims=True)
    return x * lax.rsqrt(ms + RMS_EPS) * w


def _sigmoid(x):
    return 1.0 / (1.0 + jnp.exp(-x))


def _silu(x):
    return x * _sigmoid(x)


def _mod_kernel(c_ref, w_ref, b_ref, o_ref):
    c = c_ref[...]
    o_ref[...] = jnp.dot(_silu(c), w_ref[...], precision=HIGHEST,
                         preferred_element_type=F32) + b_ref[...]


def _modulation(cs, w_mod, b_mod):
    rows = cs.shape[0]
    n = w_mod.shape[1]
    blk = D_MODEL
    return pl.pallas_call(
        _mod_kernel,
        out_shape=jax.ShapeDtypeStruct((rows, n), F32),
        grid=(n // blk,),
        in_specs=[pl.BlockSpec((rows, D_MODEL), lambda j: (0, 0)),
                  pl.BlockSpec((D_MODEL, blk), lambda j: (0, j)),
                  pl.BlockSpec((1, blk), lambda j: (0, j))],
        out_specs=pl.BlockSpec((rows, blk), lambda j: (0, j)),
        compiler_params=_cparams(1),
        name="modulation",
    )(cs, w_mod, b_mod.reshape(1, n))


def _conv_silu(xbc, cw_ref, cb_ref, row_len):
    tm = xbc.shape[0]
    pos = lax.broadcasted_iota(jnp.int32, (tm, 1), 0) % row_len
    prev = jnp.where(pos == 0, 0.0, pltpu.roll(xbc, 1, axis=0))
    nxt = jnp.where(pos == row_len - 1, 0.0, pltpu.roll(xbc, tm - 1, axis=0))
    out = prev * cw_ref[0:1, :] + xbc * cw_ref[1:2, :] + nxt * cw_ref[2:3, :] + cb_ref[...]
    return _silu(out)


def _softplus(x):
    return jnp.maximum(x, 0.0) + jnp.log(1.0 + jnp.exp(-jnp.abs(x)))


def _inproj_kernel(x_ref, sh_ref, sc_ref, gain_ref, wuf_ref, wz_ref, wxs_ref, wb_ref, wc_ref,
                   wdt_ref, wg_ref, dft_ref, cwx_ref, cbx_ref, cwb_ref, cbb_ref, cwc_ref, cbc_ref,
                   dtb_ref, bg_ref,
                   zr_ref, zi_ref, z_ref, xs_ref, bm_ref, cm_ref, dt_ref, g_ref, *, row_len):
    x = x_ref[0]
    h = _rms(x, gain_ref[...]) * (1.0 + sc_ref[0]) + sh_ref[0]
    hb = h.astype(BF16)
    dot = functools.partial(jnp.dot, preferred_element_type=F32)

    xs_ref[0] = _conv_silu(dot(hb, wxs_ref[...]), cwx_ref, cbx_ref, row_len).astype(BF16)
    bm_ref[0] = _conv_silu(dot(hb, wb_ref[...]), cwb_ref, cbb_ref, row_len).astype(BF16)
    cm_ref[0] = _conv_silu(dot(hb, wc_ref[...]), cwc_ref, cbc_ref, row_len).astype(BF16)
    dt_ref[0] = _softplus(dot(hb, wdt_ref[...]) + dtb_ref[...])
    if zr_ref is None:
        return
    uf = dot(hb, wuf_ref[...]).astype(BF16)
    for g in range(F_WIDTH // F_GROUP_W):
        sl = slice(g * F_GROUP_W, (g + 1) * F_GROUP_W)
        zz = dot(uf[:, sl], dft_ref[...].astype(BF16))
        zr_ref[0, :, sl] = zz[:, :F_GROUP_W].astype(BF16)
        zi_ref[0, :, sl] = zz[:, F_GROUP_W:].astype(BF16)
    z_ref[0] = dot(hb, wz_ref[...]).astype(BF16)
    g_ref[0] = _sigmoid(dot(hb, wg_ref[...]) + bg_ref[...]).astype(BF16)


def _inproj_ctx_kernel(x_ref, sh_ref, sc_ref, gain_ref, wxs_ref, wb_ref, wc_ref, wdt_ref,
                       cwx_ref, cbx_ref, cwb_ref, cbb_ref, cwc_ref, cbc_ref, dtb_ref,
                       xs_ref, bm_ref, cm_ref, dt_ref, *, row_len):
    _inproj_kernel(x_ref, sh_ref, sc_ref, gain_ref, None, None, wxs_ref, wb_ref, wc_ref,
                   wdt_ref, None, None, cwx_ref, cbx_ref, cwb_ref, cbb_ref, cwc_ref, cbc_ref,
                   dtb_ref, None, None, None, None, xs_ref, bm_ref, cm_ref, dt_ref, None,
                   row_len=row_len)


def _in_projection(x, shift, scale, gain, w, row_len, tm, full):
    bn, length, d = x.shape
    grid = (bn, length // tm)
    tok = lambda width: pl.BlockSpec((1, tm, width), lambda b, i: (b, i, 0))
    vec = pl.BlockSpec((1, 1, d), lambda b, i: (b, 0, 0))
    out = lambda width, dt: jax.ShapeDtypeStruct((bn, length, width), dt)
    conv_ops = [w["cwx"], w["cbx"], w["cwb"], w["cbb"], w["cwc"], w["cbc"]]
    ssd_shapes = [out(D_INNER, BF16), out(BC_WIDTH, BF16), out(BC_WIDTH, BF16), out(256, F32)]
    ssd_specs = [tok(D_INNER), tok(BC_WIDTH), tok(BC_WIDTH), tok(256)]
    if full:
        ops = [x, shift, scale, gain, w["wuf"], w["wz"], w["wxs"], w["wb"], w["wc"], w["wdt"], w["wg"],
               w["dft"]] + conv_ops + [w["dtb"], w["bg"]]
        kern = functools.partial(_inproj_kernel, row_len=row_len)
        out_shape = [out(F_WIDTH, BF16), out(F_WIDTH, BF16), out(D_INNER, BF16)] + ssd_shapes + \
                    [out(2 * D_MODEL, BF16)]
        out_specs = [tok(F_WIDTH), tok(F_WIDTH), tok(D_INNER)] + ssd_specs + [tok(2 * D_MODEL)]
        name = "in_projection"
    else:
        ops = [x, shift, scale, gain, w["wxs"], w["wb"], w["wc"], w["wdt"]] + conv_ops + [w["dtb"]]
        kern = functools.partial(_inproj_ctx_kernel, row_len=row_len)
        out_shape = ssd_shapes
        out_specs = ssd_specs
        name = "in_projection_ctx"
    in_specs = [tok(d), vec, vec] + [_const_spec(o.shape) for o in ops[3:]]
    return pl.pallas_call(kern, out_shape=out_shape, grid=grid, in_specs=in_specs,
                          out_specs=out_specs, compiler_params=_cparams(2, VMEM_LIMIT),
                          name=name)(*ops)


def _dft_constants():
    j = np.arange(F_GROUP_W)
    ang = 2.0 * np.pi * np.outer(j, j) / F_GROUP_W
    dft_c = np.concatenate([np.cos(ang), -np.sin(ang)], axis=1)
    k1 = np.arange(FFT_N1)
    a1 = 2.0 * np.pi * np.outer(k1, k1) / FFT_N1
    c1, s1 = np.cos(a1), np.sin(a1)
    w1 = np.block([[c1, s1], [-s1, c1]])
    n2 = np.arange(FFT_N2)
    at = 2.0 * np.pi * np.outer(k1, n2) / (FFT_N1 * FFT_N2)
    tw_c = np.repeat(np.cos(at), 128, axis=1)
    tw_s = np.repeat(np.sin(at), 128, axis=1)
    a2 = 2.0 * np.pi * np.outer(n2, n2) / FFT_N2
    w2 = np.zeros((FFT_N2, FFT_KB, 2, FFT_KB, FFT_N2))
    for jj in range(FFT_KB):
        w2[:, jj, 0, jj, :] = np.cos(a2)
        w2[:, jj, 1, jj, :] = np.sin(a2)
    w2 = w2.reshape(FFT_N2 * FFT_KB, 2 * FFT_KB * FFT_N2)
    return (jnp.asarray(dft_c, F32), jnp.asarray(w1, F32), jnp.asarray(tw_c, F32),
            jnp.asarray(tw_s, F32), jnp.asarray(w2, F32))


FFT_QB = 16


def _seq_dft_kernel(zr_ref, zi_ref, w1_ref, twc_ref, tws_ref, w2_ref, o_ref, qr_s, qi_s, tr_s, ti_s, *, scale):
    w1 = w1_ref[...].astype(BF16)
    reps = F_WIDTH // 128
    for qb in range(FFT_N2 // FFT_QB):
        qs = slice(qb * FFT_QB, (qb + 1) * FFT_QB)
        zr_t = pltpu.einshape("abc->bac", zr_ref[0, :, qs, :])
        zi_t = pltpu.einshape("abc->bac", zi_ref[0, :, qs, :])
        for q in range(FFT_QB):
            n2 = qb * FFT_QB + q
            rhs = jnp.concatenate([zr_t[q], zi_t[q]], axis=0)
            t = jnp.dot(w1, rhs, preferred_element_type=F32)
            a, b = t[:FFT_N1], t[FFT_N1:]
            cw = jnp.concatenate([twc_ref[:, n2 * 128:(n2 + 1) * 128]] * reps, axis=1)
            sw = jnp.concatenate([tws_ref[:, n2 * 128:(n2 + 1) * 128]] * reps, axis=1)
            qr_s[q] = (a * cw + b * sw).astype(BF16)
            qi_s[q] = (b * cw - a * sw).astype(BF16)
        tr_s[:, qs, :] = pltpu.einshape("abc->bac", qr_s[...])
        ti_s[:, qs, :] = pltpu.einshape("abc->bac", qi_s[...])
    w2 = w2_ref[...].astype(BF16)
    rows = FFT_KB * FFT_N2
    for kb in range(FFT_N1 // FFT_KB):
        ks = slice(kb * FFT_KB, (kb + 1) * FFT_KB)
        rhs = jnp.concatenate([tr_s[ks].reshape(rows, F_WIDTH), ti_s[ks].reshape(rows, F_WIDTH)], axis=0)
        y = jnp.dot(w2, rhs, preferred_element_type=F32) * scale
        o_ref[0, :, ks, :] = y.reshape(FFT_N2, FFT_KB, F_WIDTH)


def _sequence_dft_real(zr, zi, w1, tw_c, tw_s, w2):
    bn, length, width = zr.shape
    blk = pl.BlockSpec((1, FFT_N1, FFT_N2, width), lambda b: (b, 0, 0, 0))
    scale = 1.0 / float(np.sqrt(length * F_GROUP_W))
    f = pl.pallas_call(
        functools.partial(_seq_dft_kernel, scale=scale),
        out_shape=jax.ShapeDtypeStruct((bn, FFT_N2, FFT_N1, width), F32),
        grid=(bn,),
        in_specs=[blk, blk, _const_spec(w1.shape), _const_spec(tw_c.shape), _const_spec(tw_s.shape),
                  _const_spec(w2.shape)],
        out_specs=pl.BlockSpec((1, FFT_N2, FFT_N1, width), lambda b: (b, 0, 0, 0)),
        scratch_shapes=[pltpu.VMEM((FFT_QB, FFT_N1, width), BF16)] * 2
                       + [pltpu.VMEM((FFT_N1, FFT_N2, width), BF16)] * 2,
        compiler_params=_cparams(1, VMEM_LIMIT),
        name="seq_dft",
    )(zr.reshape(bn, FFT_N1, FFT_N2, width), zi.reshape(bn, FFT_N1, FFT_N2, width), w1, tw_c, tw_s, w2)
    return f.reshape(bn, length, width)


def _ssd_kernel(xs_ref, bm_ref, cm_ref, dt_ref, alog_ref, tri_ref, h0_ref, y_ref, hfin_ref, st_ref,
                *, nchunks):
    c = pl.program_id(2)

    @pl.when(c == 0)
    def _():
        st_ref[...] = h0_ref[0, 0]

    q = CHUNK
    tri = tri_ref[0]
    mask = tri > 0.5
    a = -jnp.exp(alog_ref[0])
    dt = dt_ref[0]
    da = dt * a
    acum = jnp.dot(tri, da, precision=HIGHEST, preferred_element_type=F32)
    acum_t = acum.T
    alast = jnp.sum(da, axis=0, keepdims=True)
    lo = lax.broadcasted_iota(jnp.int32, (q, 128), 1) < HEAD_DIM
    lo_row = lo[0:1]
    col = lambda m, i: jnp.broadcast_to(m[:, i:i + 1], (q, 128))

    for g in range(N_GROUPS):
        gs = slice(g * D_STATE, (g + 1) * D_STATE)
        bg = bm_ref[0, :, gs]
        cg = cm_ref[0, :, gs]
        cb = lax.dot_general(cg, bg, (((1,), (1,)), ((), ())), preferred_element_type=F32)
        h_t = st_ref[g]
        yoff = jnp.dot(cg, h_t.astype(BF16), preferred_element_type=F32)
        xw_parts = []
        cd_parts = []
        for pr in range(HEADS_PER_GROUP // 2):
            h0 = g * HEADS_PER_GROUP + 2 * pr
            h1 = h0 + 1
            ps = slice((g * 3 + pr) * 128, (g * 3 + pr + 1) * 128)
            ca0, ca1 = col(acum, h0), col(acum, h1)
            ra0 = jnp.broadcast_to(acum_t[h0:h0 + 1, :], (q, q))
            ra1 = jnp.broadcast_to(acum_t[h1:h1 + 1, :], (q, q))
            m0 = (jnp.exp(jnp.where(mask, ca0 - ra0, -1e30)) * cb).astype(BF16)
            m1 = (jnp.exp(jnp.where(mask, ca1 - ra1, -1e30)) * cb).astype(BF16)
            xdt = xs_ref[0, :, ps].astype(F32) * jnp.where(lo, col(dt, h0), col(dt, h1))
            xdt0 = jnp.where(lo, xdt, 0.0).astype(BF16)
            xdt1 = jnp.where(lo, 0.0, xdt).astype(BF16)
            ydiag = (jnp.dot(m0, xdt0, preferred_element_type=F32)
                     + jnp.dot(m1, xdt1, preferred_element_type=F32))
            cap = jnp.where(lo, ca0, ca1)
            y = ydiag + yoff[:, pr * 128:(pr + 1) * 128] * jnp.exp(cap)
            y_ref[0, 0, :, ps] = y.astype(BF16)
            alp = jnp.where(lo_row, jnp.broadcast_to(alast[:, h0:h0 + 1], (1, 128)),
                            jnp.broadcast_to(alast[:, h1:h1 + 1], (1, 128)))
            xw_parts.append((xdt * jnp.exp(alp - cap)).astype(BF16))
            cd_parts.append(jnp.exp(alp))
        xw = jnp.concatenate(xw_parts, axis=1)
        cd = jnp.concatenate(cd_parts, axis=1)
        bg_t = bg.astype(F32).T.astype(BF16)
        st_ref[g] = h_t * cd + jnp.dot(bg_t, xw, preferred_element_type=F32)

    @pl.when(c == nchunks - 1)
    def _():
        hfin_ref[0, 0] = st_ref[...]


def _ssd_scan(xs, bm, cm, dt, alog_pad, tri, h0):
    bn, length, _ = xs.shape
    nchunks = length // CHUNK
    cc = lambda d, c: c + d * (nchunks - 1 - 2 * c)
    tok = lambda width: pl.BlockSpec((1, CHUNK, width), lambda d, b, c: (b, cc(d, c), 0))
    st_shape = (N_GROUPS, D_STATE, GROUP_COLS)
    st_spec = pl.BlockSpec((1, 1) + st_shape, lambda d, b, c: (d, b, 0, 0, 0))
    return pl.pallas_call(
        functools.partial(_ssd_kernel, nchunks=nchunks),
        out_shape=[jax.ShapeDtypeStruct((2, bn, length, D_INNER), BF16),
                   jax.ShapeDtypeStruct((2, bn) + st_shape, F32)],
        grid=(2, bn, nchunks),
        in_specs=[tok(D_INNER), tok(BC_WIDTH), tok(BC_WIDTH),
                  pl.BlockSpec((1, CHUNK, 128), lambda d, b, c: (b, cc(d, c), d)),
                  pl.BlockSpec((1, 1, 128), lambda d, b, c: (d, 0, 0)),
                  pl.BlockSpec((1, CHUNK, CHUNK), lambda d, b, c: (d, 0, 0)),
                  st_spec],
        out_specs=[pl.BlockSpec((1, 1, CHUNK, D_INNER), lambda d, b, c: (d, b, cc(d, c), 0)),
                   st_spec],
        scratch_shapes=[pltpu.VMEM(st_shape, F32)],
        compiler_params=_cparams(3),
        name="ssd_scan",
    )(xs, bm, cm, dt, alog_pad, tri, h0)


MIX_SUB = 256


def _split_bf16(v):
    hi = v.astype(BF16)
    return hi, (v - hi.astype(F32)).astype(BF16)


def _mix_kernel(yf_ref, yb_ref, xs_ref, z_ref, f_ref, g_ref, x_ref, g1_ref, sh2_ref, sc2_ref,
                dsk_ref, nssd_ref, npost_ref, npre_ref, wso_ref, wf_ref, wo_ref, wr_ref,
                hx_ref, hm_ref, aff_ref):
    dot = functools.partial(jnp.dot, preferred_element_type=F32)
    wr_hi, wr_lo = _split_bf16(wr_ref[...])
    for r0 in range(0, x_ref.shape[1], MIX_SUB):
        rs = slice(r0, r0 + MIX_SUB)
        y = (yf_ref[0, 0, rs].astype(F32) + yb_ref[0, 0, rs].astype(F32)
             + dsk_ref[...] * xs_ref[0, rs].astype(F32))
        y = _rms(y * _silu(z_ref[0, rs].astype(F32)), nssd_ref[...])
        s_branch = dot(y.astype(BF16), wso_ref[...])
        f_branch = dot(f_ref[0, rs].astype(BF16), wf_ref[...])
        gates = g_ref[0, rs].astype(F32)
        merged = gates[:, :D_MODEL] * f_branch + gates[:, D_MODEL:] * s_branch
        mix = dot(merged.astype(BF16), wo_ref[...])
        hx = x_ref[0, rs] + g1_ref[0] * _rms(mix, npost_ref[...])
        hx_ref[0, rs] = hx
        hm = _rms(hx, npre_ref[...]) * (1.0 + sc2_ref[0]) + sh2_ref[0]
        hm_ref[0, rs] = hm
        hm_hi, hm_lo = _split_bf16(hm)
        logits = dot(hm_hi, wr_hi) + (dot(hm_hi, wr_lo) + dot(hm_lo, wr_hi))
        e = jnp.exp(logits - jnp.max(logits, axis=-1, keepdims=True))
        aff_ref[0, rs] = e / jnp.sum(e, axis=-1, keepdims=True)


def _mix(y2, xs, z, f, gates, x, g1, sh2, sc2, consts, tm):
    bn, length, d = x.shape
    tok = lambda width: pl.BlockSpec((1, tm, width), lambda b, i: (b, i, 0))
    ydir = lambda k: pl.BlockSpec((1, 1, tm, D_INNER), lambda b, i: (k, b, i, 0))
    vec = pl.BlockSpec((1, 1, d), lambda b, i: (b, 0, 0))
    return pl.pallas_call(
        _mix_kernel,
        out_shape=[jax.ShapeDtypeStruct((bn, length, d), F32),
                   jax.ShapeDtypeStruct((bn, length, d), F32),
                   jax.ShapeDtypeStruct((bn, length, N_EXPERTS), F32)],
        grid=(bn, length // tm),
        in_specs=[ydir(0), ydir(1), tok(D_INNER), tok(D_INNER), tok(F_WIDTH), tok(2 * D_MODEL), tok(d),
                  vec, vec, vec] + [_const_spec(c.shape) for c in consts],
        out_specs=[tok(d), tok(d), tok(N_EXPERTS)],
        compiler_params=_cparams(2, VMEM_LIMIT),
        name="merge_out_router",
    )(y2, y2, xs, z, f, gates, x, g1, sh2, sc2, *consts)


def _expert_kernel(idx_ref, idxn_ref, val_ref, hm_ref, wg_ref, wu_ref, wd_ref, ye_ref, buf0, buf1, sem,
                   *, cap, nb, nsteps):
    s = pl.program_id(0)
    bufs = (buf0, buf1)
    dot = functools.partial(jnp.dot, preferred_element_type=F32)

    def row_copy(k, j, b, r):
        return pltpu.make_async_copy(hm_ref.at[b, pl.ds(r, 1), :], bufs[k].at[pl.ds(j, 1), :], sem.at[k])

    def wait_rows(k):
        pltpu.make_async_copy(hm_ref.at[0, pl.ds(0, cap), :], bufs[k], sem.at[k]).wait()

    b0 = (2 * s) % nb

    @pl.when(s == 0)
    def _():
        def issue(j, carry):
            row_copy(0, j, b0, idx_ref[0, 0, j]).start()
            return carry
        lax.fori_loop(0, cap, issue, 0)

    for k in range(2):
        wait_rows(k)
        x = bufs[k][...].astype(BF16)
        if k == 0:
            nxt_idx, nxt_b = (lambda j: idx_ref[1, 0, j]), b0 + 1
        else:
            nxt_idx, nxt_b = (lambda j: idxn_ref[0, 0, j]), (b0 + 2) % nb
        for j in range(cap):
            row_copy(1 - k, j, nxt_b, nxt_idx(j)).start()
        hid = (_silu(dot(x, wg_ref[0])) * dot(x, wu_ref[0])).astype(BF16)
        ye_ref[k] = dot(hid, wd_ref[0]) * val_ref[k]

    @pl.when(s == nsteps - 1)
    def _():
        wait_rows(0)


def _experts(idx, vals, hm, wg, wu, wd):
    bn, length, d = hm.shape
    ne, _, ff = wg.shape
    cap = idx.shape[-1]
    per = bn // 2
    nsteps = ne * per
    cur = lambda s: (s, 0, 0)
    nxt = lambda s: ((s + 1) % nsteps, 0, 0)
    wspec = pl.BlockSpec((1, d, ff), lambda s: (s // per, 0, 0))
    return pl.pallas_call(
        functools.partial(_expert_kernel, cap=cap, nb=bn, nsteps=nsteps),
        out_shape=jax.ShapeDtypeStruct((ne * bn, cap, d), F32),
        grid=(nsteps,),
        in_specs=[pl.BlockSpec((2, 1, cap), cur, memory_space=pltpu.SMEM),
                  pl.BlockSpec((2, 1, cap), nxt, memory_space=pltpu.SMEM),
                  pl.BlockSpec((2, cap, 1), cur),
                  pl.BlockSpec(memory_space=pl.ANY),
                  wspec, wspec, pl.BlockSpec((1, ff, d), lambda s: (s // per, 0, 0))],
        out_specs=pl.BlockSpec((2, cap, d), cur),
        scratch_shapes=[pltpu.VMEM((cap, d), F32), pltpu.VMEM((cap, d), F32),
                        pltpu.SemaphoreType.DMA((2,))],
        compiler_params=_cparams(1, VMEM_LIMIT),
        name="expert_ffn",
    )(idx, idx, vals, hm, wg, wu, wd)


def _combine_kernel(idx_ref, ye_ref, o_ref, *, cap):
    @pl.when(pl.program_id(1) == 0)
    def _():
        o_ref[...] = jnp.zeros_like(o_ref)

    def body(j, carry):
        r = idx_ref[0, 0, j]
        o_ref[0, pl.ds(r, 1), :] += ye_ref[0, pl.ds(j, 1), :]
        return carry

    lax.fori_loop(0, cap, body, 0, unroll=8)


def _combine(idx, ye, bn, length, d):
    ne = N_EXPERTS
    cap = idx.shape[-1]
    sel = lambda b, e: (e * bn + b, 0, 0)
    return pl.pallas_call(
        functools.partial(_combine_kernel, cap=cap),
        out_shape=jax.ShapeDtypeStruct((bn, length, d), F32),
        grid=(bn, ne),
        in_specs=[pl.BlockSpec((1, 1, cap), sel, memory_space=pltpu.SMEM),
                  pl.BlockSpec((1, cap, d), sel)],
        out_specs=pl.BlockSpec((1, length, d), lambda b, e: (b, 0, 0)),
        compiler_params=_cparams(2, VMEM_LIMIT),
        name="expert_combine",
    )(idx, ye)


def _final_kernel(hx_ref, ffn_ref, g2_ref, w_ref, o_ref):
    o_ref[0] = hx_ref[0] + g2_ref[0] * _rms(ffn_ref[0], w_ref[...])


def _final(hx, ffn, g2, w, tm):
    bn, length, d = hx.shape
    tok = pl.BlockSpec((1, tm, d), lambda b, i: (b, i, 0))
    return pl.pallas_call(
        _final_kernel,
        out_shape=jax.ShapeDtypeStruct((bn, length, d), F32),
        grid=(bn, length // tm),
        in_specs=[tok, tok, pl.BlockSpec((1, 1, d), lambda b, i: (b, 0, 0)), _const_spec(w.shape)],
        out_specs=tok,
        compiler_params=_cparams(2),
        name="ffn_residual",
    )(hx, ffn, g2, w)


def kernel(x, c, ctx, c_ctx, w_mod, b_mod, norm_mix_pre, norm_mix_post, norm_ffn_pre, norm_ffn_post,
           w_in, conv_w, conv_b, dt_bias, a_log, d_skip, ssd_norm, w_fourier, w_ssd_out, b_gate, w_out,
           w_router, w_e_gate, w_e_up, w_e_down):
    bn, length, d = x.shape
    ctx_len = ctx.shape[1]
    l = 0
    row = lambda v: v.reshape(1, -1)

    cs = jnp.concatenate([c, c_ctx[None], jnp.zeros((7, d), F32)], axis=0)
    mods = _modulation(cs, w_mod[l], b_mod[l])
    mod_x = mods[:bn].reshape(bn, N_MOD, 1, d)
    mod_c = jnp.broadcast_to(mods[bn].reshape(1, N_MOD, 1, d), (bn, N_MOD, 1, d))
    sh1, sc1, g1, sh2, sc2, g2 = [mod_x[:, i] for i in range(N_MOD)]

    i1 = F_WIDTH
    i2 = i1 + D_INNER
    i3 = i2 + D_INNER
    i4 = i3 + BC_WIDTH
    i5 = i4 + BC_WIDTH
    i6 = i5 + 2 * N_HEADS
    wi = w_in[l]
    wdt = jnp.zeros((d, 256), F32).at[:, :N_HEADS].set(wi[:, i5:i5 + N_HEADS])
    wdt = wdt.at[:, 128:128 + N_HEADS].set(wi[:, i5 + N_HEADS:i6])
    dtb = jnp.zeros((1, 256), F32).at[0, :N_HEADS].set(dt_bias[l, 0]).at[0, 128:128 + N_HEADS].set(dt_bias[l, 1])
    dft_c, w1, tw_c, tw_s, w2 = _dft_constants()
    cw, cb = conv_w[l], conv_b[l]
    j1, j2 = D_INNER, D_INNER + BC_WIDTH
    w = dict(
        wuf=wi[:, :i1].astype(BF16), wz=wi[:, i1:i2].astype(BF16), wxs=wi[:, i2:i3].astype(BF16),
        wb=wi[:, i3:i4].astype(BF16), wc=wi[:, i4:i5].astype(BF16), wdt=wdt.astype(BF16),
        wg=wi[:, i6:].astype(BF16), dft=dft_c,
        cwx=cw[:, :j1], cbx=row(cb[:j1]), cwb=cw[:, j1:j2], cbb=row(cb[j1:j2]),
        cwc=cw[:, j2:], cbc=row(cb[j2:]), dtb=dtb, bg=row(b_gate[l]))
    gain_pre = row(norm_mix_pre[l])

    alog_pad = jnp.zeros((2, 1, 128), F32).at[:, 0, :N_HEADS].set(a_log[l])
    r = np.arange(CHUNK)
    tri = jnp.asarray(np.stack([r[:, None] >= r[None, :], r[:, None] <= r[None, :]]), F32)

    cxs, cbm, ccm, cdt = _in_projection(ctx, mod_c[:, 0], mod_c[:, 1], gain_pre, w, ctx_len, ctx_len, False)
    h_zero = jnp.zeros((2, bn, N_GROUPS, D_STATE, GROUP_COLS), F32)
    _, h_ctx = _ssd_scan(cxs, cbm, ccm, cdt, alog_pad, tri, h_zero)

    zr, zi, z, xs, bm, cm, dt, gates = _in_projection(x, sh1, sc1, gain_pre, w, GRID_W, 256, True)
    f = _sequence_dft_real(zr, zi, w1, tw_c, tw_s, w2)
    y2, _ = _ssd_scan(xs, bm, cm, dt, alog_pad, tri, h_ctx)

    consts = [row(jnp.repeat(d_skip[l], HEAD_DIM)), row(ssd_norm[l]), row(norm_mix_post[l]),
              row(norm_ffn_pre[l]), w_ssd_out[l].astype(BF16), w_fourier[l].astype(BF16),
              w_out[l].astype(BF16), w_router[l]]
    hx, hm, aff = _mix(y2, xs, z, f, gates, x, g1, sh2, sc2, consts, 2 * MIX_SUB)

    cap = EC_FACTOR * length // N_EXPERTS
    top_aff, top_idx = lax.top_k(jnp.transpose(aff, (2, 0, 1)), cap)
    idx = top_idx.reshape(N_EXPERTS * bn, 1, cap).astype(jnp.int32)
    vals = top_aff.reshape(N_EXPERTS * bn, cap, 1)
    ye = _experts(idx, vals, hm, w_e_gate[l].astype(BF16), w_e_up[l].astype(BF16),
                  w_e_down[l].astype(BF16))
    ffn = _combine(idx, ye, bn, length, d)
    return _final(hx, ffn, g2, row(norm_ffn_post[l]), 512)
```

```python
import functools

import numpy as np
import jax
import jax.numpy as jnp
from jax import lax
from jax.experimental import pallas as pl
from jax.experimental.pallas import tpu as pltpu

F32 = jnp.float32
BF16 = jnp.bfloat16
HIGHEST = lax.Precision.HIGHEST

D_MODEL = 1024
GRID_W = 64
F_GROUP_W = 128
F_WIDTH = 512
D_INNER = 1536
HEAD_DIM = 64
N_HEADS = 24
N_GROUPS = 4
HEADS_PER_GROUP = 6
D_STATE = 128
CHUNK = 128
BC_WIDTH = N_GROUPS * D_STATE
N_EXPERTS = 16
EC_FACTOR = 2
N_MOD = 6
RMS_EPS = 1e-6
GROUP_COLS = HEADS_PER_GROUP * HEAD_DIM

FFT_N1 = 128
FFT_N2 = 32
FFT_KB = 8

VMEM_LIMIT = 56 * 1024 * 1024


def _cparams(n_axes, vmem=None):
    return pltpu.CompilerParams(dimension_semantics=("arbitrary",) * n_axes,
                                vmem_limit_bytes=vmem)


def _const_spec(shape):
    nd = len(shape)
    return pl.BlockSpec(shape, lambda *_: (0,) * nd, pipeline_mode=pl.Buffered(1))


def _rms(x, w):
    ms = jnp.mean(x * x, axis=-1, keepdims=True)
    return x * lax.rsqrt(ms + RMS_EPS) * w


def _sigmoid(x):
    return 1.0 / (1.0 + jnp.exp(-x))


def _silu(x):
    return x * _sigmoid(x)


ROW_TILE = (8, 128)


def _to_row_tiles(v):
    parts = [v[:, c * 128:(c + 1) * 128] for c in range(ROW_TILE[0])]
    return jnp.swapaxes(jnp.stack(parts, axis=0), 0, 1)


def _from_row_tiles(v3):
    t = jnp.swapaxes(v3, 0, 1)
    return jnp.concatenate([t[c] for c in range(ROW_TILE[0])], axis=1)


def _mod_kernel(c_ref, w_ref, b_ref, o_ref):
    c = c_ref[...]
    o_ref[...] = jnp.dot(_silu(c), w_ref[...], precision=HIGHEST,
                         preferred_element_type=F32) + b_ref[...]


def _modulation(cs, w_mod, b_mod):
    rows = cs.shape[0]
    n = w_mod.shape[1]
    blk = D_MODEL
    return pl.pallas_call(
        _mod_kernel,
        out_shape=jax.ShapeDtypeStruct((rows, n), F32),
        grid=(n // blk,),
        in_specs=[pl.BlockSpec((rows, D_MODEL), lambda j: (0, 0)),
                  pl.BlockSpec((D_MODEL, blk), lambda j: (0, j)),
                  pl.BlockSpec((1, blk), lambda j: (0, j))],
        out_specs=pl.BlockSpec((rows, blk), lambda j: (0, j)),
        compiler_params=_cparams(1),
        name="modulation",
    )(cs, w_mod, b_mod.reshape(1, n))


def _conv_silu(xbc, cw_ref, cb_ref, row_len):
    tm = xbc.shape[0]
    pos = lax.broadcasted_iota(jnp.int32, (tm, 1), 0) % row_len
    prev = jnp.where(pos == 0, 0.0, pltpu.roll(xbc, 1, axis=0))
    nxt = jnp.where(pos == row_len - 1, 0.0, pltpu.roll(xbc, tm - 1, axis=0))
    out = prev * cw_ref[0:1, :] + xbc * cw_ref[1:2, :] + nxt * cw_ref[2:3, :] + cb_ref[...]
    return _silu(out)


INPROJ_SUB = 256


def _softplus(x):
    return jnp.maximum(x, 0.0) + jnp.log(1.0 + jnp.exp(-jnp.abs(x)))


def _inproj_kernel(x_ref, sh_ref, sc_ref, gain_ref, wuf_ref, wz_ref, wxs_ref, wb_ref, wc_ref,
                   wdt_ref, wg_ref, dft_ref, cwx_ref, cbx_ref, cwb_ref, cbb_ref, cwc_ref, cbc_ref,
                   dtb_ref, bg_ref,
                   zr_ref, zi_ref, z_ref, xs_ref, bm_ref, cm_ref, dt_ref, g_ref, *, row_len):
    dot = functools.partial(jnp.dot, preferred_element_type=F32)
    tm = x_ref.shape[1]
    sub = min(tm, INPROJ_SUB)
    for r0 in range(0, tm, sub):
        rs = slice(r0, r0 + sub)
        h = _rms(x_ref[0, rs], gain_ref[...]) * (1.0 + sc_ref[0]) + sh_ref[0]
        hb = h.astype(BF16)
        xs_ref[0, rs] = _conv_silu(dot(hb, wxs_ref[...]), cwx_ref, cbx_ref, row_len).astype(BF16)
        bm_ref[0, rs] = _conv_silu(dot(hb, wb_ref[...]), cwb_ref, cbb_ref, row_len).astype(BF16)
        cm_ref[0, rs] = _conv_silu(dot(hb, wc_ref[...]), cwc_ref, cbc_ref, row_len).astype(BF16)
        dt_ref[0, rs] = _softplus(dot(hb, wdt_ref[...]) + dtb_ref[...])
        if zr_ref is None:
            continue
        uf = dot(hb, wuf_ref[...]).astype(BF16)
        for g in range(F_WIDTH // F_GROUP_W):
            sl = slice(g * F_GROUP_W, (g + 1) * F_GROUP_W)
            zz = dot(uf[:, sl], dft_ref[...].astype(BF16))
            zr_ref[0, rs, sl] = zz[:, :F_GROUP_W].astype(BF16)
            zi_ref[0, rs, sl] = zz[:, F_GROUP_W:].astype(BF16)
        z_ref[0, rs] = dot(hb, wz_ref[...]).astype(BF16)
        g_ref[0, rs] = _sigmoid(dot(hb, wg_ref[...]) + bg_ref[...]).astype(BF16)


def _inproj_ctx_kernel(x_ref, sh_ref, sc_ref, gain_ref, wxs_ref, wb_ref, wc_ref, wdt_ref,
                       cwx_ref, cbx_ref, cwb_ref, cbb_ref, cwc_ref, cbc_ref, dtb_ref,
                       xs_ref, bm_ref, cm_ref, dt_ref, *, row_len):
    _inproj_kernel(x_ref, sh_ref, sc_ref, gain_ref, None, None, wxs_ref, wb_ref, wc_ref,
                   wdt_ref, None, None, cwx_ref, cbx_ref, cwb_ref, cbb_ref, cwc_ref, cbc_ref,
                   dtb_ref, None, None, None, None, xs_ref, bm_ref, cm_ref, dt_ref, None,
                   row_len=row_len)


def _in_projection(x, shift, scale, gain, w, row_len, tm, full):
    bn, length, d = x.shape
    grid = (bn, length // tm)
    tok = lambda width: pl.BlockSpec((1, tm, width), lambda b, i: (b, i, 0))
    vec = pl.BlockSpec((1, 1, d), lambda b, i: (b, 0, 0))
    out = lambda width, dt: jax.ShapeDtypeStruct((bn, length, width), dt)
    conv_ops = [w["cwx"], w["cbx"], w["cwb"], w["cbb"], w["cwc"], w["cbc"]]
    ssd_shapes = [out(D_INNER, BF16), out(BC_WIDTH, BF16), out(BC_WIDTH, BF16), out(256, F32)]
    ssd_specs = [tok(D_INNER), tok(BC_WIDTH), tok(BC_WIDTH), tok(256)]
    if full:
        ops = [x, shift, scale, gain, w["wuf"], w["wz"], w["wxs"], w["wb"], w["wc"], w["wdt"], w["wg"],
               w["dft"]] + conv_ops + [w["dtb"], w["bg"]]
        kern = functools.partial(_inproj_kernel, row_len=row_len)
        out_shape = [out(F_WIDTH, BF16), out(F_WIDTH, BF16), out(D_INNER, BF16)] + ssd_shapes + \
                    [out(2 * D_MODEL, BF16)]
        out_specs = [tok(F_WIDTH), tok(F_WIDTH), tok(D_INNER)] + ssd_specs + [tok(2 * D_MODEL)]
        name = "in_projection"
    else:
        ops = [x, shift, scale, gain, w["wxs"], w["wb"], w["wc"], w["wdt"]] + conv_ops + [w["dtb"]]
        kern = functools.partial(_inproj_ctx_kernel, row_len=row_len)
        out_shape = ssd_shapes
        out_specs = ssd_specs
        name = "in_projection_ctx"
    in_specs = [tok(d), vec, vec] + [_const_spec(o.shape) for o in ops[3:]]
    return pl.pallas_call(kern, out_shape=out_shape, grid=grid, in_specs=in_specs,
                          out_specs=out_specs, compiler_params=_cparams(2, VMEM_LIMIT),
                          name=name)(*ops)


def _dft_constants():
    j = np.arange(F_GROUP_W)
    ang = 2.0 * np.pi * np.outer(j, j) / F_GROUP_W
    dft_c = np.concatenate([np.cos(ang), -np.sin(ang)], axis=1)
    k1 = np.arange(FFT_N1)
    a1 = 2.0 * np.pi * np.outer(k1, k1) / FFT_N1
    c1, s1 = np.cos(a1), np.sin(a1)
    w1 = np.block([[c1, s1], [-s1, c1]])
    n2 = np.arange(FFT_N2)
    at = 2.0 * np.pi * np.outer(k1, n2) / (FFT_N1 * FFT_N2)
    tw_c = np.repeat(np.cos(at), 128, axis=1)
    tw_s = np.repeat(np.sin(at), 128, axis=1)
    a2 = 2.0 * np.pi * np.outer(n2, n2) / FFT_N2
    w2 = np.zeros((FFT_N2, FFT_KB, 2, FFT_KB, FFT_N2))
    for jj in range(FFT_KB):
        w2[:, jj, 0, jj, :] = np.cos(a2)
        w2[:, jj, 1, jj, :] = np.sin(a2)
    w2 = w2.reshape(FFT_N2 * FFT_KB, 2 * FFT_KB * FFT_N2)
    return (jnp.asarray(dft_c, F32), jnp.asarray(w1, F32), jnp.asarray(tw_c, F32),
            jnp.asarray(tw_s, F32), jnp.asarray(w2, F32))


FFT_QB = 16


def _seq_dft_kernel(zr_ref, zi_ref, w1_ref, twc_ref, tws_ref, w2_ref, o_ref, qr_s, qi_s, tr_s, ti_s, *, scale):
    w1 = w1_ref[...].astype(BF16)
    reps = F_WIDTH // 128
    for qb in range(FFT_N2 // FFT_QB):
        qs = slice(qb * FFT_QB, (qb + 1) * FFT_QB)
        zr_t = jnp.swapaxes(zr_ref[0, :, qs, :], 0, 1)
        zi_t = jnp.swapaxes(zi_ref[0, :, qs, :], 0, 1)
        for q in range(FFT_QB):
            n2 = qb * FFT_QB + q
            rhs = jnp.concatenate([zr_t[q], zi_t[q]], axis=0)
            t = jnp.dot(w1, rhs, preferred_element_type=F32)
            a, b = t[:FFT_N1], t[FFT_N1:]
            cw = jnp.concatenate([twc_ref[:, n2 * 128:(n2 + 1) * 128]] * reps, axis=1)
            sw = jnp.concatenate([tws_ref[:, n2 * 128:(n2 + 1) * 128]] * reps, axis=1)
            qr_s[q] = (a * cw + b * sw).astype(BF16)
            qi_s[q] = (b * cw - a * sw).astype(BF16)
        tr_s[:, qs, :] = jnp.swapaxes(qr_s[...], 0, 1)
        ti_s[:, qs, :] = jnp.swapaxes(qi_s[...], 0, 1)
    w2 = w2_ref[...].astype(BF16)
    rows = FFT_KB * FFT_N2
    for kb in range(FFT_N1 // FFT_KB):
        ks = slice(kb * FFT_KB, (kb + 1) * FFT_KB)
        rhs = jnp.concatenate([tr_s[ks].reshape(rows, F_WIDTH), ti_s[ks].reshape(rows, F_WIDTH)], axis=0)
        y = jnp.dot(w2, rhs, preferred_element_type=F32) * scale
        o_ref[0, :, ks, :] = y.reshape(FFT_N2, FFT_KB, F_WIDTH)


def _sequence_dft_real(zr, zi, w1, tw_c, tw_s, w2):
    bn, length, width = zr.shape
    blk = pl.BlockSpec((1, FFT_N1, FFT_N2, width), lambda b: (b, 0, 0, 0))
    scale = 1.0 / float(np.sqrt(length * F_GROUP_W))
    f = pl.pallas_call(
        functools.partial(_seq_dft_kernel, scale=scale),
        out_shape=jax.ShapeDtypeStruct((bn, FFT_N2, FFT_N1, width), F32),
        grid=(bn,),
        in_specs=[blk, blk, _const_spec(w1.shape), _const_spec(tw_c.shape), _const_spec(tw_s.shape),
                  _const_spec(w2.shape)],
        out_specs=pl.BlockSpec((1, FFT_N2, FFT_N1, width), lambda b: (b, 0, 0, 0)),
        scratch_shapes=[pltpu.VMEM((FFT_QB, FFT_N1, width), BF16)] * 2
                       + [pltpu.VMEM((FFT_N1, FFT_N2, width), BF16)] * 2,
        compiler_params=_cparams(1, VMEM_LIMIT),
        name="seq_dft",
    )(zr.reshape(bn, FFT_N1, FFT_N2, width), zi.reshape(bn, FFT_N1, FFT_N2, width), w1, tw_c, tw_s, w2)
    return f.reshape(bn, length, width)


def _ssd_kernel(xs_ref, bm_ref, cm_ref, dt_ref, alog_ref, tri_ref, h0_ref, y_ref, hfin_ref, st_ref,
                *, nchunks):
    c = pl.program_id(2)

    @pl.when(c == 0)
    def _():
        st_ref[...] = h0_ref[0, 0]

    q = CHUNK
    tri = tri_ref[0]
    mask = tri > 0.5
    a = -jnp.exp(alog_ref[0])
    dt = dt_ref[0]
    da = dt * a
    acum = jnp.dot(tri, da, precision=HIGHEST, preferred_element_type=F32)
    acum_t = acum.T
    alast = jnp.sum(da, axis=0, keepdims=True)
    lo = lax.broadcasted_iota(jnp.int32, (q, 128), 1) < HEAD_DIM
    lo_row = lo[0:1]
    col = lambda m, i: jnp.broadcast_to(m[:, i:i + 1], (q, 128))

    for g in range(N_GROUPS):
        gs = slice(g * D_STATE, (g + 1) * D_STATE)
        bg = bm_ref[0, :, gs]
        cg = cm_ref[0, :, gs]
        cb = lax.dot_general(cg, bg, (((1,), (1,)), ((), ())), preferred_element_type=F32)
        h_t = st_ref[g]
        yoff = jnp.dot(cg, h_t.astype(BF16), preferred_element_type=F32)
        xw_parts = []
        cd_parts = []
        for pr in range(HEADS_PER_GROUP // 2):
            h0 = g * HEADS_PER_GROUP + 2 * pr
            h1 = h0 + 1
            ps = slice((g * 3 + pr) * 128, (g * 3 + pr + 1) * 128)
            ca0, ca1 = col(acum, h0), col(acum, h1)
            ra0 = jnp.broadcast_to(acum_t[h0:h0 + 1, :], (q, q))
            ra1 = jnp.broadcast_to(acum_t[h1:h1 + 1, :], (q, q))
            m0 = (jnp.exp(jnp.where(mask, ca0 - ra0, -1e30)) * cb).astype(BF16)
            m1 = (jnp.exp(jnp.where(mask, ca1 - ra1, -1e30)) * cb).astype(BF16)
            xdt = xs_ref[0, :, ps].astype(F32) * jnp.where(lo, col(dt, h0), col(dt, h1))
            xdt0 = jnp.where(lo, xdt, 0.0).astype(BF16)
            xdt1 = jnp.where(lo, 0.0, xdt).astype(BF16)
            ydiag = (jnp.dot(m0, xdt0, preferred_element_type=F32)
                     + jnp.dot(m1, xdt1, preferred_element_type=F32))
            cap = jnp.where(lo, ca0, ca1)
            y = ydiag + yoff[:, pr * 128:(pr + 1) * 128] * jnp.exp(cap)
            y_ref[0, 0, :, ps] = y.astype(BF16)
            alp = jnp.where(lo_row, jnp.broadcast_to(alast[:, h0:h0 + 1], (1, 128)),
                            jnp.broadcast_to(alast[:, h1:h1 + 1], (1, 128)))
            xw_parts.append((xdt * jnp.exp(alp - cap)).astype(BF16))
            cd_parts.append(jnp.exp(alp))
        xw = jnp.concatenate(xw_parts, axis=1)
        cd = jnp.concatenate(cd_parts, axis=1)
        bg_t = bg.astype(F32).T.astype(BF16)
        st_ref[g] = h_t * cd + jnp.dot(bg_t, xw, preferred_element_type=F32)

    @pl.when(c == nchunks - 1)
    def _():
        hfin_ref[0, 0] = st_ref[...]


def _ssd_scan(xs, bm, cm, dt, alog_pad, tri, h0):
    bn, length, _ = xs.shape
    nchunks = length // CHUNK
    cc = lambda d, c: c + d * (nchunks - 1 - 2 * c)
    tok = lambda width: pl.BlockSpec((1, CHUNK, width), lambda d, b, c: (b, cc(d, c), 0))
    st_shape = (N_GROUPS, D_STATE, GROUP_COLS)
    st_spec = pl.BlockSpec((1, 1) + st_shape, lambda d, b, c: (d, b, 0, 0, 0))
    return pl.pallas_call(
        functools.partial(_ssd_kernel, nchunks=nchunks),
        out_shape=[jax.ShapeDtypeStruct((2, bn, length, D_INNER), BF16),
                   jax.ShapeDtypeStruct((2, bn) + st_shape, F32)],
        grid=(2, bn, nchunks),
        in_specs=[tok(D_INNER), tok(BC_WIDTH), tok(BC_WIDTH),
                  pl.BlockSpec((1, CHUNK, 128), lambda d, b, c: (b, cc(d, c), d)),
                  pl.BlockSpec((1, 1, 128), lambda d, b, c: (d, 0, 0)),
                  pl.BlockSpec((1, CHUNK, CHUNK), lambda d, b, c: (d, 0, 0)),
                  st_spec],
        out_specs=[pl.BlockSpec((1, 1, CHUNK, D_INNER), lambda d, b, c: (d, b, cc(d, c), 0)),
                   st_spec],
        scratch_shapes=[pltpu.VMEM(st_shape, F32)],
        compiler_params=_cparams(3),
        name="ssd_scan",
    )(xs, bm, cm, dt, alog_pad, tri, h0)


MIX_SUB = 256


def _split_bf16(v):
    hi = v.astype(BF16)
    return hi, (v - hi.astype(F32)).astype(BF16)


def _mix_kernel(yf_ref, yb_ref, xs_ref, z_ref, f_ref, g_ref, x_ref, g1_ref, sh2_ref, sc2_ref,
                dsk_ref, nssd_ref, npost_ref, npre_ref, wso_ref, wf_ref, wo_ref, wr_ref,
                hx_ref, hm_ref, aff_ref):
    dot = functools.partial(jnp.dot, preferred_element_type=F32)
    wr_hi, wr_lo = _split_bf16(wr_ref[...])
    for r0 in range(0, x_ref.shape[1], MIX_SUB):
        rs = slice(r0, r0 + MIX_SUB)
        y = (yf_ref[0, 0, rs].astype(F32) + yb_ref[0, 0, rs].astype(F32)
             + dsk_ref[...] * xs_ref[0, rs].astype(F32))
        y = _rms(y * _silu(z_ref[0, rs].astype(F32)), nssd_ref[...])
        s_branch = dot(y.astype(BF16), wso_ref[...])
        f_branch = dot(f_ref[0, rs].astype(BF16), wf_ref[...])
        gates = g_ref[0, rs].astype(F32)
        merged = gates[:, :D_MODEL] * f_branch + gates[:, D_MODEL:] * s_branch
        mix = dot(merged.astype(BF16), wo_ref[...])
        hx = x_ref[0, rs] + g1_ref[0] * _rms(mix, npost_ref[...])
        hx_ref[0, rs] = hx
        hm = _rms(hx, npre_ref[...]) * (1.0 + sc2_ref[0]) + sh2_ref[0]
        hm_ref[0, rs] = _to_row_tiles(hm)
        hm_hi, hm_lo = _split_bf16(hm)
        logits = dot(hm_hi, wr_hi) + (dot(hm_hi, wr_lo) + dot(hm_lo, wr_hi))
        e = jnp.exp(logits - jnp.max(logits, axis=-1, keepdims=True))
        aff_ref[0, rs] = e / jnp.sum(e, axis=-1, keepdims=True)


def _mix(y2, xs, z, f, gates, x, g1, sh2, sc2, consts, tm):
    bn, length, d = x.shape
    tok = lambda width: pl.BlockSpec((1, tm, width), lambda b, i: (b, i, 0))
    ydir = lambda k: pl.BlockSpec((1, 1, tm, D_INNER), lambda b, i: (k, b, i, 0))
    vec = pl.BlockSpec((1, 1, d), lambda b, i: (b, 0, 0))
    return pl.pallas_call(
        _mix_kernel,
        out_shape=[jax.ShapeDtypeStruct((bn, length, d), F32),
                   jax.ShapeDtypeStruct((bn, length) + ROW_TILE, F32),
                   jax.ShapeDtypeStruct((bn, length, N_EXPERTS), F32)],
        grid=(bn, length // tm),
        in_specs=[ydir(0), ydir(1), tok(D_INNER), tok(D_INNER), tok(F_WIDTH), tok(2 * D_MODEL), tok(d),
                  vec, vec, vec] + [_const_spec(c.shape) for c in consts],
        out_specs=[tok(d), pl.BlockSpec((1, tm) + ROW_TILE, lambda b, i: (b, i, 0, 0)), tok(N_EXPERTS)],
        compiler_params=_cparams(2, VMEM_LIMIT),
        name="merge_out_router",
    )(y2, y2, xs, z, f, gates, x, g1, sh2, sc2, *consts)


def _expert_kernel(idx_ref, idxn_ref, val_ref, hm_ref, wg_ref, wu_ref, wd_ref, ye_ref, buf0, buf1, sem,
                   *, cap, nb, nsteps):
    s = pl.program_id(0)
    bufs = (buf0, buf1)
    dot = functools.partial(jnp.dot, preferred_element_type=F32)

    def row_copy(k, j, b, r):
        return pltpu.make_async_copy(hm_ref.at[b, pl.ds(r, 1)], bufs[k].at[pl.ds(j, 1)], sem.at[k])

    def wait_rows(k):
        pltpu.make_async_copy(hm_ref.at[0, pl.ds(0, cap)], bufs[k], sem.at[k]).wait()

    b0 = (2 * s) % nb

    @pl.when(s == 0)
    def _():
        def issue(j, carry):
            row_copy(0, j, b0, idx_ref[0, 0, j]).start()
            return carry
        lax.fori_loop(0, cap, issue, 0)

    for k in range(2):
        wait_rows(k)
        x = _from_row_tiles(bufs[k][...]).astype(BF16)
        if k == 0:
            nxt_idx, nxt_b = (lambda j: idx_ref[1, 0, j]), b0 + 1
        else:
            nxt_idx, nxt_b = (lambda j: idxn_ref[0, 0, j]), (b0 + 2) % nb
        for j in range(cap):
            row_copy(1 - k, j, nxt_b, nxt_idx(j)).start()
        hid = (_silu(dot(x, wg_ref[0])) * dot(x, wu_ref[0])).astype(BF16)
        ye_ref[k] = _to_row_tiles(dot(hid, wd_ref[0]) * val_ref[k])

    @pl.when(s == nsteps - 1)
    def _():
        wait_rows(0)


def _experts(idx, vals, hm, wg, wu, wd):
    bn = hm.shape[0]
    ne, d, ff = wg.shape
    cap = idx.shape[-1]
    per = bn // 2
    nsteps = ne * per
    cur = lambda s: (s, 0, 0)
    nxt = lambda s: ((s + 1) % nsteps, 0, 0)
    wspec = pl.BlockSpec((1, d, ff), lambda s: (s // per, 0, 0))
    return pl.pallas_call(
        functools.partial(_expert_kernel, cap=cap, nb=bn, nsteps=nsteps),
        out_shape=jax.ShapeDtypeStruct((ne * bn, cap) + ROW_TILE, F32),
        grid=(nsteps,),
        in_specs=[pl.BlockSpec((2, 1, cap), cur, memory_space=pltpu.SMEM),
                  pl.BlockSpec((2, 1, cap), nxt, memory_space=pltpu.SMEM),
                  pl.BlockSpec((2, cap, 1), cur),
                  pl.BlockSpec(memory_space=pl.ANY),
                  wspec, wspec, pl.BlockSpec((1, ff, d), lambda s: (s // per, 0, 0))],
        out_specs=pl.BlockSpec((2, cap) + ROW_TILE, lambda s: (s, 0, 0, 0)),
        scratch_shapes=[pltpu.VMEM((cap,) + ROW_TILE, F32), pltpu.VMEM((cap,) + ROW_TILE, F32),
                        pltpu.SemaphoreType.DMA((2,))],
        compiler_params=_cparams(1, VMEM_LIMIT),
        name="expert_ffn",
    )(idx, idx, vals, hm, wg, wu, wd)


COMBINE_ROWS = 16


def _combine_kernel(idx_ref, ye_ref, o_ref, *, cap):
    @pl.when(pl.program_id(1) == 0)
    def _():
        o_ref[...] = jnp.zeros_like(o_ref)

    def body(i, carry):
        j0 = i * COMBINE_ROWS
        rows = [idx_ref[0, 0, j0 + u] for u in range(COMBINE_ROWS)]
        new = [o_ref[0, rows[u]] + ye_ref[0, j0 + u] for u in range(COMBINE_ROWS)]
        for u in range(COMBINE_ROWS):
            o_ref[0, rows[u]] = new[u]
        return carry

    lax.fori_loop(0, cap // COMBINE_ROWS, body, 0)


def _combine(idx, ye, bn, length):
    ne = N_EXPERTS
    cap = idx.shape[-1]
    return pl.pallas_call(
        functools.partial(_combine_kernel, cap=cap),
        out_shape=jax.ShapeDtypeStruct((bn, length) + ROW_TILE, F32),
        grid=(bn, ne),
        in_specs=[pl.BlockSpec((1, 1, cap), lambda b, e: (e * bn + b, 0, 0), memory_space=pltpu.SMEM),
                  pl.BlockSpec((1, cap) + ROW_TILE, lambda b, e: (e * bn + b, 0, 0, 0))],
        out_specs=pl.BlockSpec((1, length) + ROW_TILE, lambda b, e: (b, 0, 0, 0)),
        compiler_params=_cparams(2, VMEM_LIMIT),
        name="expert_combine",
    )(idx, ye)


def _final_kernel(hx_ref, ffn_ref, g2_ref, w_ref, o_ref):
    f3 = ffn_ref[0]
    ms = jnp.mean(f3 * f3, axis=(1, 2), keepdims=True)
    y = _from_row_tiles(f3 * lax.rsqrt(ms + RMS_EPS) * w_ref[...])
    o_ref[0] = hx_ref[0] + g2_ref[0] * y


def _final(hx, ffn, g2, w, tm):
    bn, length, d = hx.shape
    tok = pl.BlockSpec((1, tm, d), lambda b, i: (b, i, 0))
    w3 = w.reshape((1,) + ROW_TILE)
    return pl.pallas_call(
        _final_kernel,
        out_shape=jax.ShapeDtypeStruct((bn, length, d), F32),
        grid=(bn, length // tm),
        in_specs=[tok, pl.BlockSpec((1, tm) + ROW_TILE, lambda b, i: (b, i, 0, 0)),
                  pl.BlockSpec((1, 1, d), lambda b, i: (b, 0, 0)), _const_spec(w3.shape)],
        out_specs=tok,
        compiler_params=_cparams(2),
        name="ffn_residual",
    )(hx, ffn, g2, w3)


def kernel(x, c, ctx, c_ctx, w_mod, b_mod, norm_mix_pre, norm_mix_post, norm_ffn_pre, norm_ffn_post,
           w_in, conv_w, conv_b, dt_bias, a_log, d_skip, ssd_norm, w_fourier, w_ssd_out, b_gate, w_out,
           w_router, w_e_gate, w_e_up, w_e_down):
    bn, length, d = x.shape
    ctx_len = ctx.shape[1]
    l = 0
    row = lambda v: v.reshape(1, -1)

    cs = jnp.concatenate([c, c_ctx[None], jnp.zeros((7, d), F32)], axis=0)
    mods = _modulation(cs, w_mod[l], b_mod[l])
    mod_x = mods[:bn].reshape(bn, N_MOD, 1, d)
    mod_c = jnp.broadcast_to(mods[bn].reshape(1, N_MOD, 1, d), (bn, N_MOD, 1, d))
    sh1, sc1, g1, sh2, sc2, g2 = [mod_x[:, i] for i in range(N_MOD)]

    i1 = F_WIDTH
    i2 = i1 + D_INNER
    i3 = i2 + D_INNER
    i4 = i3 + BC_WIDTH
    i5 = i4 + BC_WIDTH
    i6 = i5 + 2 * N_HEADS
    wi = w_in[l]
    wdt = jnp.zeros((d, 256), F32).at[:, :N_HEADS].set(wi[:, i5:i5 + N_HEADS])
    wdt = wdt.at[:, 128:128 + N_HEADS].set(wi[:, i5 + N_HEADS:i6])
    dtb = jnp.zeros((1, 256), F32).at[0, :N_HEADS].set(dt_bias[l, 0]).at[0, 128:128 + N_HEADS].set(dt_bias[l, 1])
    dft_c, w1, tw_c, tw_s, w2 = _dft_constants()
    cw, cb = conv_w[l], conv_b[l]
    j1, j2 = D_INNER, D_INNER + BC_WIDTH
    w = dict(
        wuf=wi[:, :i1].astype(BF16), wz=wi[:, i1:i2].astype(BF16), wxs=wi[:, i2:i3].astype(BF16),
        wb=wi[:, i3:i4].astype(BF16), wc=wi[:, i4:i5].astype(BF16), wdt=wdt.astype(BF16),
        wg=wi[:, i6:].astype(BF16), dft=dft_c,
        cwx=cw[:, :j1], cbx=row(cb[:j1]), cwb=cw[:, j1:j2], cbb=row(cb[j1:j2]),
        cwc=cw[:, j2:], cbc=row(cb[j2:]), dtb=dtb, bg=row(b_gate[l]))
    gain_pre = row(norm_mix_pre[l])

    alog_pad = jnp.zeros((2, 1, 128), F32).at[:, 0, :N_HEADS].set(a_log[l])
    r = np.arange(CHUNK)
    tri = jnp.asarray(np.stack([r[:, None] >= r[None, :], r[:, None] <= r[None, :]]), F32)

    cxs, cbm, ccm, cdt = _in_projection(ctx, mod_c[:, 0], mod_c[:, 1], gain_pre, w, ctx_len, ctx_len, False)
    h_zero = jnp.zeros((2, bn, N_GROUPS, D_STATE, GROUP_COLS), F32)
    _, h_ctx = _ssd_scan(cxs, cbm, ccm, cdt, alog_pad, tri, h_zero)

    zr, zi, z, xs, bm, cm, dt, gates = _in_projection(x, sh1, sc1, gain_pre, w, GRID_W, 2 * INPROJ_SUB, True)
    f = _sequence_dft_real(zr, zi, w1, tw_c, tw_s, w2)
    y2, _ = _ssd_scan(xs, bm, cm, dt, alog_pad, tri, h_ctx)

    consts = [row(jnp.repeat(d_skip[l], HEAD_DIM)), row(ssd_norm[l]), row(norm_mix_post[l]),
              row(norm_ffn_pre[l]), w_ssd_out[l].astype(BF16), w_fourier[l].astype(BF16),
              w_out[l].astype(BF16), w_router[l]]
    hx, hm, aff = _mix(y2, xs, z, f, gates, x, g1, sh2, sc2, consts, 2 * MIX_SUB)

    cap = EC_FACTOR * length // N_EXPERTS
    top_aff, top_idx = lax.top_k(jnp.transpose(aff, (2, 0, 1)), cap)
    idx = top_idx.reshape(N_EXPERTS * bn, 1, cap).astype(jnp.int32)
    vals = top_aff.reshape(N_EXPERTS * bn, cap, 1)
    ye = _experts(idx, vals, hm, w_e_gate[l].astype(BF16), w_e_up[l].astype(BF16),
                  w_e_down[l].astype(BF16))
    ffn = _combine(idx, ye, bn, length)
    return _final(hx, ffn, g2, row(norm_ffn_post[l]), 512)
```

```python
import functools

import numpy as np
import jax
import jax.numpy as jnp
from jax import lax
from jax.experimental import pallas as pl
from jax.experimental.pallas import tpu as pltpu

F32 = jnp.float32
BF16 = jnp.bfloat16
HIGHEST = lax.Precision.HIGHEST

D_MODEL = 1024
GRID_W = 64
F_GROUP_W = 128
F_WIDTH = 512
D_INNER = 1536
HEAD_DIM = 64
N_HEADS = 24
N_GROUPS = 4
HEADS_PER_GROUP = 6
D_STATE = 128
CHUNK = 128
BC_WIDTH = N_GROUPS * D_STATE
N_EXPERTS = 16
EC_FACTOR = 2
N_MOD = 6
RMS_EPS = 1e-6
GROUP_COLS = HEADS_PER_GROUP * HEAD_DIM

FFT_N1 = 128
FFT_N2 = 32
FFT_KB = 8

VMEM_LIMIT = 56 * 1024 * 1024


def _cparams(n_axes, vmem=None):
    return pltpu.CompilerParams(dimension_semantics=("arbitrary",) * n_axes,
                                vmem_limit_bytes=vmem)


def _const_spec(shape):
    nd = len(shape)
    return pl.BlockSpec(shape, lambda *_: (0,) * nd, pipeline_mode=pl.Buffered(1))


def _rms(x, w):
    ms = jnp.mean(x * x, axis=-1, keepdims=True)
    return x * lax.rsqrt(ms + RMS_EPS) * w


def _sigmoid(x):
    return 1.0 / (1.0 + jnp.exp(-x))


def _silu(x):
    return x * _sigmoid(x)


ROW_TILE = (8, 128)


def _to_row_tiles(v):
    parts = [v[:, c * 128:(c + 1) * 128] for c in range(ROW_TILE[0])]
    return jnp.swapaxes(jnp.stack(parts, axis=0), 0, 1)


def _from_row_tiles(v3):
    t = jnp.swapaxes(v3, 0, 1)
    return jnp.concatenate([t[c] for c in range(ROW_TILE[0])], axis=1)


def _mod_kernel(c_ref, w_ref, b_ref, o_ref):
    c = c_ref[...]
    o_ref[...] = jnp.dot(_silu(c), w_ref[...], precision=HIGHEST,
                         preferred_element_type=F32) + b_ref[...]


def _modulation(cs, w_mod, b_mod):
    rows = cs.shape[0]
    n = w_mod.shape[1]
    blk = D_MODEL
    return pl.pallas_call(
        _mod_kernel,
        out_shape=jax.ShapeDtypeStruct((rows, n), F32),
        grid=(n // blk,),
        in_specs=[pl.BlockSpec((rows, D_MODEL), lambda j: (0, 0)),
                  pl.BlockSpec((D_MODEL, blk), lambda j: (0, j)),
                  pl.BlockSpec((1, blk), lambda j: (0, j))],
        out_specs=pl.BlockSpec((rows, blk), lambda j: (0, j)),
        compiler_params=_cparams(1),
        name="modulation",
    )(cs, w_mod, b_mod.reshape(1, n))


def _conv_silu(xbc, cw_ref, cb_ref, row_len):
    tm = xbc.shape[0]
    pos = lax.broadcasted_iota(jnp.int32, (tm, 1), 0) % row_len
    prev = jnp.where(pos == 0, 0.0, pltpu.roll(xbc, 1, axis=0))
    nxt = jnp.where(pos == row_len - 1, 0.0, pltpu.roll(xbc, tm - 1, axis=0))
    out = prev * cw_ref[0:1, :] + xbc * cw_ref[1:2, :] + nxt * cw_ref[2:3, :] + cb_ref[...]
    return _silu(out)


INPROJ_SUB = 256


def _softplus(x):
    return jnp.maximum(x, 0.0) + jnp.log(1.0 + jnp.exp(-jnp.abs(x)))


def _inproj_kernel(x_ref, sh_ref, sc_ref, gain_ref, wuf_ref, wz_ref, wxs_ref, wb_ref, wc_ref,
                   wdt_ref, wg_ref, dft_ref, cwx_ref, cbx_ref, cwb_ref, cbb_ref, cwc_ref, cbc_ref,
                   dtb_ref, bg_ref,
                   zr_ref, zi_ref, z_ref, xs_ref, bm_ref, cm_ref, dt_ref, g_ref, *, row_len):
    dot = functools.partial(jnp.dot, preferred_element_type=F32)
    tm = x_ref.shape[1]
    sub = min(tm, INPROJ_SUB)
    for r0 in range(0, tm, sub):
        rs = slice(r0, r0 + sub)
        h = _rms(x_ref[0, rs], gain_ref[...]) * (1.0 + sc_ref[0]) + sh_ref[0]
        hb = h.astype(BF16)
        xs_ref[0, rs] = _conv_silu(dot(hb, wxs_ref[...]), cwx_ref, cbx_ref, row_len).astype(BF16)
        bm_ref[0, rs] = _conv_silu(dot(hb, wb_ref[...]), cwb_ref, cbb_ref, row_len).astype(BF16)
        cm_ref[0, rs] = _conv_silu(dot(hb, wc_ref[...]), cwc_ref, cbc_ref, row_len).astype(BF16)
        dt_ref[0, rs] = _softplus(dot(hb, wdt_ref[...]) + dtb_ref[...])
        if zr_ref is None:
            continue
        uf = dot(hb, wuf_ref[...]).astype(BF16)
        for g in range(F_WIDTH // F_GROUP_W):
            sl = slice(g * F_GROUP_W, (g + 1) * F_GROUP_W)
            zz = dot(uf[:, sl], dft_ref[...].astype(BF16))
            zr_ref[0, rs, sl] = zz[:, :F_GROUP_W].astype(BF16)
            zi_ref[0, rs, sl] = zz[:, F_GROUP_W:].astype(BF16)
        z_ref[0, rs] = dot(hb, wz_ref[...]).astype(BF16)
        g_ref[0, rs] = _sigmoid(dot(hb, wg_ref[...]) + bg_ref[...]).astype(BF16)


def _inproj_ctx_kernel(x_ref, sh_ref, sc_ref, gain_ref, wxs_ref, wb_ref, wc_ref, wdt_ref,
                       cwx_ref, cbx_ref, cwb_ref, cbb_ref, cwc_ref, cbc_ref, dtb_ref,
                       xs_ref, bm_ref, cm_ref, dt_ref, *, row_len):
    _inproj_kernel(x_ref, sh_ref, sc_ref, gain_ref, None, None, wxs_ref, wb_ref, wc_ref,
                   wdt_ref, None, None, cwx_ref, cbx_ref, cwb_ref, cbb_ref, cwc_ref, cbc_ref,
                   dtb_ref, None, None, None, None, xs_ref, bm_ref, cm_ref, dt_ref, None,
                   row_len=row_len)


def _in_projection(x, shift, scale, gain, w, row_len, tm, full):
    bn, length, d = x.shape
    grid = (bn, length // tm)
    tok = lambda width: pl.BlockSpec((1, tm, width), lambda b, i: (b, i, 0))
    vec = pl.BlockSpec((1, 1, d), lambda b, i: (b, 0, 0))
    out = lambda width, dt: jax.ShapeDtypeStruct((bn, length, width), dt)
    conv_ops = [w["cwx"], w["cbx"], w["cwb"], w["cbb"], w["cwc"], w["cbc"]]
    ssd_shapes = [out(D_INNER, BF16), out(BC_WIDTH, BF16), out(BC_WIDTH, BF16), out(256, F32)]
    ssd_specs = [tok(D_INNER), tok(BC_WIDTH), tok(BC_WIDTH), tok(256)]
    if full:
        ops = [x, shift, scale, gain, w["wuf"], w["wz"], w["wxs"], w["wb"], w["wc"], w["wdt"], w["wg"],
               w["dft"]] + conv_ops + [w["dtb"], w["bg"]]
        kern = functools.partial(_inproj_kernel, row_len=row_len)
        out_shape = [out(F_WIDTH, BF16), out(F_WIDTH, BF16), out(D_INNER, BF16)] + ssd_shapes + \
                    [out(2 * D_MODEL, BF16)]
        out_specs = [tok(F_WIDTH), tok(F_WIDTH), tok(D_INNER)] + ssd_specs + [tok(2 * D_MODEL)]
        name = "in_projection"
    else:
        ops = [x, shift, scale, gain, w["wxs"], w["wb"], w["wc"], w["wdt"]] + conv_ops + [w["dtb"]]
        kern = functools.partial(_inproj_ctx_kernel, row_len=row_len)
        out_shape = ssd_shapes
        out_specs = ssd_specs
        name = "in_projection_ctx"
    in_specs = [tok(d), vec, vec] + [_const_spec(o.shape) for o in ops[3:]]
    return pl.pallas_call(kern, out_shape=out_shape, grid=grid, in_specs=in_specs,
                          out_specs=out_specs, compiler_params=_cparams(2, VMEM_LIMIT),
                          name=name)(*ops)


def _dft_constants():
    j = np.arange(F_GROUP_W)
    ang = 2.0 * np.pi * np.outer(j, j) / F_GROUP_W
    dft_c = np.concatenate([np.cos(ang), -np.sin(ang)], axis=1)
    k1 = np.arange(FFT_N1)
    a1 = 2.0 * np.pi * np.outer(k1, k1) / FFT_N1
    c1, s1 = np.cos(a1), np.sin(a1)
    w1 = np.block([[c1, s1], [-s1, c1]])
    n2 = np.arange(FFT_N2)
    at = 2.0 * np.pi * np.outer(k1, n2) / (FFT_N1 * FFT_N2)
    tw_c = np.repeat(np.cos(at), 128, axis=1)
    tw_s = np.repeat(np.sin(at), 128, axis=1)
    a2 = 2.0 * np.pi * np.outer(n2, n2) / FFT_N2
    w2 = np.zeros((FFT_N2, FFT_KB, 2, FFT_KB, FFT_N2))
    for jj in range(FFT_KB):
        w2[:, jj, 0, jj, :] = np.cos(a2)
        w2[:, jj, 1, jj, :] = np.sin(a2)
    w2 = w2.reshape(FFT_N2 * FFT_KB, 2 * FFT_KB * FFT_N2)
    return (jnp.asarray(dft_c, F32), jnp.asarray(w1, F32), jnp.asarray(tw_c, F32),
            jnp.asarray(tw_s, F32), jnp.asarray(w2, F32))


FFT_QB = 16


def _seq_dft_kernel(zr_ref, zi_ref, w1_ref, twc_ref, tws_ref, w2_ref, o_ref, qr_s, qi_s, tr_s, ti_s, *, scale):
    w1 = w1_ref[...].astype(BF16)
    reps = F_WIDTH // 128
    for qb in range(FFT_N2 // FFT_QB):
        qs = slice(qb * FFT_QB, (qb + 1) * FFT_QB)
        zr_t = jnp.swapaxes(zr_ref[0, :, qs, :], 0, 1)
        zi_t = jnp.swapaxes(zi_ref[0, :, qs, :], 0, 1)
        for q in range(FFT_QB):
            n2 = qb * FFT_QB + q
            rhs = jnp.concatenate([zr_t[q], zi_t[q]], axis=0)
            t = jnp.dot(w1, rhs, preferred_element_type=F32)
            a, b = t[:FFT_N1], t[FFT_N1:]
            cw = jnp.concatenate([twc_ref[:, n2 * 128:(n2 + 1) * 128]] * reps, axis=1)
            sw = jnp.concatenate([tws_ref[:, n2 * 128:(n2 + 1) * 128]] * reps, axis=1)
            qr_s[q] = (a * cw + b * sw).astype(BF16)
            qi_s[q] = (b * cw - a * sw).astype(BF16)
        tr_s[:, qs, :] = jnp.swapaxes(qr_s[...], 0, 1)
        ti_s[:, qs, :] = jnp.swapaxes(qi_s[...], 0, 1)
    w2 = w2_ref[...].astype(BF16)
    rows = FFT_KB * FFT_N2
    for kb in range(FFT_N1 // FFT_KB):
        ks = slice(kb * FFT_KB, (kb + 1) * FFT_KB)
        rhs = jnp.concatenate([tr_s[ks].reshape(rows, F_WIDTH), ti_s[ks].reshape(rows, F_WIDTH)], axis=0)
        y = jnp.dot(w2, rhs, preferred_element_type=F32) * scale
        o_ref[0, :, ks, :] = y.reshape(FFT_N2, FFT_KB, F_WIDTH)


def _sequence_dft_real(zr, zi, w1, tw_c, tw_s, w2):
    bn, length, width = zr.shape
    blk = pl.BlockSpec((1, FFT_N1, FFT_N2, width), lambda b: (b, 0, 0, 0))
    scale = 1.0 / float(np.sqrt(length * F_GROUP_W))
    f = pl.pallas_call(
        functools.partial(_seq_dft_kernel, scale=scale),
        out_shape=jax.ShapeDtypeStruct((bn, FFT_N2, FFT_N1, width), F32),
        grid=(bn,),
        in_specs=[blk, blk, _const_spec(w1.shape), _const_spec(tw_c.shape), _const_spec(tw_s.shape),
                  _const_spec(w2.shape)],
        out_specs=pl.BlockSpec((1, FFT_N2, FFT_N1, width), lambda b: (b, 0, 0, 0)),
        scratch_shapes=[pltpu.VMEM((FFT_QB, FFT_N1, width), BF16)] * 2
                       + [pltpu.VMEM((FFT_N1, FFT_N2, width), BF16)] * 2,
        compiler_params=_cparams(1, VMEM_LIMIT),
        name="seq_dft",
    )(zr.reshape(bn, FFT_N1, FFT_N2, width), zi.reshape(bn, FFT_N1, FFT_N2, width), w1, tw_c, tw_s, w2)
    return f.reshape(bn, length, width)


SSD_SUB = 64
LOG2E = 1.4426950408889634


class _Dir:
    pass


def _ssd_prepare(dt_ref, alog_ref, reverse):
    q, sb = CHUNK, SSD_SUB
    d = _Dir()
    d.reverse = reverse
    r_i = lax.broadcasted_iota(jnp.int32, (q, q), 0)
    c_i = lax.broadcasted_iota(jnp.int32, (q, q), 1)
    tri = ((r_i <= c_i) if reverse else (r_i >= c_i)).astype(F32)
    a = -jnp.exp(alog_ref[0])
    d.dt = dt_ref[0]
    da = d.dt * (a * LOG2E)
    d.acum = jnp.dot(tri, da, precision=HIGHEST, preferred_element_type=F32)
    lane = lax.broadcasted_iota(jnp.int32, (sb, 128), 1)
    d.lo = lane < HEAD_DIM
    lo_h = lax.broadcasted_iota(jnp.int32, (q, 128), 1) < HEAD_DIM
    row_l = lax.broadcasted_iota(jnp.int32, (sb, 128), 0)
    s_l = jnp.where(d.lo, lane, lane - HEAD_DIM)
    d.mask = (s_l >= row_l) if reverse else (s_l <= row_l)
    d.order = (1, 0) if reverse else (0, 1)

    def halves(m):
        m_t = m.T
        m_sw = pltpu.roll(m_t, sb, axis=1)
        return (jnp.where(lo_h, m_t, m_sw), jnp.where(lo_h, m_sw, m_t))

    d.rt = halves(d.acum)
    d.rt_dt = halves(d.dt)
    return d


def _ssd_block_rows(d, step):
    sb = SSD_SUB
    blk = d.order[step]
    end_r = blk * sb if d.reverse else blk * sb + sb - 1
    end_row = d.acum[end_r:end_r + 1, :]
    if step == 0:
        base_row = jnp.zeros((1, 128), F32)
    else:
        pr = d.order[0] * sb if d.reverse else d.order[0] * sb + sb - 1
        base_row = d.acum[pr:pr + 1, :]
    d.rows8 = jnp.concatenate([base_row, jnp.exp2(end_row - base_row), jnp.zeros((6, 128), F32)], axis=0)
    d.rs = slice(blk * sb, (blk + 1) * sb)
    d.blk = blk
    d.acum_b = d.acum[d.rs]
    d.dw_b = d.dt[d.rs] * jnp.exp2(end_row - d.acum_b)


def _ssd_group_begin(d, refs, g):
    xs_ref, bm_ref, cm_ref, y_ref, st_ref = refs
    gs = slice(g * D_STATE, (g + 1) * D_STATE)
    bg = bm_ref[0, d.rs, gs]
    cg = cm_ref[0, d.rs, gs]
    bg_t = bg.astype(F32).T.astype(BF16)
    cb2 = jnp.dot(cg, jnp.concatenate([bg_t, bg_t], axis=1), preferred_element_type=F32)
    h_t = st_ref[g]
    yoff = jnp.dot(cg, h_t.astype(BF16), preferred_element_type=F32)
    d.grp = dict(bg_t=bg_t, cb2=cb2, h_t=h_t, yoff=yoff, xw=[], cd=[])


def _ssd_pair(d, refs, g, pr_i):
    xs_ref, bm_ref, cm_ref, y_ref, st_ref = refs
    lo = d.lo
    lo_row = lo[0:1]
    h0 = g * HEADS_PER_GROUP + 2 * pr_i
    h1 = h0 + 1
    ps = slice((g * 3 + pr_i) * 128, (g * 3 + pr_i + 1) * 128)
    pat = jnp.where(lo, h0, h1)
    colp = jnp.take_along_axis(d.acum_b, pat, axis=1)
    dwp = jnp.take_along_axis(d.dw_b, pat, axis=1)
    r8 = jnp.take_along_axis(d.rows8, pat[0:8], axis=1)
    base_p, cd_p = r8[0:1], r8[1:2]
    rt, rt_dt = d.rt[d.blk], d.rt_dt[d.blk]
    rowp = jnp.where(lo_row, rt[h0:h0 + 1, :], rt[h1:h1 + 1, :])
    dt_row = jnp.where(lo_row, rt_dt[h0:h0 + 1, :], rt_dt[h1:h1 + 1, :])
    gd = d.grp
    m = (jnp.exp2(jnp.where(d.mask, colp - rowp, -1e30)) * (gd["cb2"] * dt_row)).astype(BF16)
    x = xs_ref[0, d.rs, ps]
    zero = jnp.zeros_like(x)
    rhs = jnp.concatenate([jnp.where(lo, x, zero), jnp.where(lo, zero, x)], axis=0)
    ydiag = jnp.dot(m, rhs, preferred_element_type=F32)
    y = ydiag + gd["yoff"][:, pr_i * 128:(pr_i + 1) * 128] * jnp.exp2(colp - base_p)
    y_ref[0, d.rs, ps] = y.astype(BF16)
    gd["xw"].append((x.astype(F32) * dwp).astype(BF16))
    gd["cd"].append(cd_p)


def _ssd_group_end(d, refs, g):
    st_ref = refs[4]
    gd = d.grp
    xw = jnp.concatenate(gd["xw"], axis=1)
    cd = jnp.concatenate(gd["cd"], axis=1)
    st_ref[g] = gd["h_t"] * cd + jnp.dot(gd["bg_t"], xw, preferred_element_type=F32)


def _ssd_kernel(xf, bf, cf, df, af, hf, xb, bb, cb_, db, ab, hb, yf, hff, yb, hfb, stf, stb, *, nchunks):
    c = pl.program_id(1)

    @pl.when(c == 0)
    def _():
        stf[...] = hf[0]
        stb[...] = hb[0]

    refs = ((xf, bf, cf, yf, stf), (xb, bb, cb_, yb, stb))
    dirs = (_ssd_prepare(df, af, False), _ssd_prepare(db, ab, True))
    for step in range(CHUNK // SSD_SUB):
        for d in dirs:
            _ssd_block_rows(d, step)
        for g in range(N_GROUPS):
            for d, r in zip(dirs, refs):
                _ssd_group_begin(d, r, g)
            for pr_i in range(HEADS_PER_GROUP // 2):
                for d, r in zip(dirs, refs):
                    _ssd_pair(d, r, g, pr_i)
            for d, r in zip(dirs, refs):
                _ssd_group_end(d, r, g)

    @pl.when(c == nchunks - 1)
    def _():
        hff[0] = stf[...]
        hfb[0] = stb[...]


def _ssd_scan(xs, bm, cm, dt, alog_pad, h0f, h0b):
    bn, length, _ = xs.shape
    nchunks = length // CHUNK
    st_shape = (N_GROUPS, D_STATE, GROUP_COLS)
    st_spec = pl.BlockSpec((1,) + st_shape, lambda b, c: (b, 0, 0, 0))

    def specs(reverse):
        cc = (lambda c: nchunks - 1 - c) if reverse else (lambda c: c)
        di = 1 if reverse else 0
        tok = lambda width: pl.BlockSpec((1, CHUNK, width), lambda b, c: (b, cc(c), 0))
        ins = [tok(D_INNER), tok(BC_WIDTH), tok(BC_WIDTH),
               pl.BlockSpec((1, CHUNK, 128), lambda b, c: (b, cc(c), di)),
               pl.BlockSpec((1, 1, 128), lambda b, c: (di, 0, 0)), st_spec]
        return ins, [tok(D_INNER), st_spec]

    in_f, out_f = specs(False)
    in_b, out_b = specs(True)
    y_shape = jax.ShapeDtypeStruct((bn, length, D_INNER), BF16)
    h_shape = jax.ShapeDtypeStruct((bn,) + st_shape, F32)
    return pl.pallas_call(
        functools.partial(_ssd_kernel, nchunks=nchunks),
        out_shape=[y_shape, h_shape, y_shape, h_shape],
        grid=(bn, nchunks),
        in_specs=in_f + in_b,
        out_specs=out_f + out_b,
        scratch_shapes=[pltpu.VMEM(st_shape, F32), pltpu.VMEM(st_shape, F32)],
        compiler_params=_cparams(2),
        name="ssd_scan",
    )(xs, bm, cm, dt, alog_pad, h0f, xs, bm, cm, dt, alog_pad, h0b)


MIX_SUB = 256


def _split_bf16(v):
    hi = v.astype(BF16)
    return hi, (v - hi.astype(F32)).astype(BF16)


def _mix_kernel(yf_ref, yb_ref, xs_ref, z_ref, f_ref, g_ref, x_ref, g1_ref, sh2_ref, sc2_ref,
                dsk_ref, nssd_ref, npost_ref, npre_ref, wso_ref, wf_ref, wo_ref, wr_ref,
                hx_ref, hm_ref, aff_ref):
    dot = functools.partial(jnp.dot, preferred_element_type=F32)
    wr_hi, wr_lo = _split_bf16(wr_ref[...])
    for r0 in range(0, x_ref.shape[1], MIX_SUB):
        rs = slice(r0, r0 + MIX_SUB)
        y = (yf_ref[0, rs].astype(F32) + yb_ref[0, rs].astype(F32)
             + dsk_ref[...] * xs_ref[0, rs].astype(F32))
        y = _rms(y * _silu(z_ref[0, rs].astype(F32)), nssd_ref[...])
        s_branch = dot(y.astype(BF16), wso_ref[...])
        f_branch = dot(f_ref[0, rs].astype(BF16), wf_ref[...])
        gates = g_ref[0, rs].astype(F32)
        merged = gates[:, :D_MODEL] * f_branch + gates[:, D_MODEL:] * s_branch
        mix = dot(merged.astype(BF16), wo_ref[...])
        hx = x_ref[0, rs] + g1_ref[0] * _rms(mix, npost_ref[...])
        hx_ref[0, rs] = hx
        hm = _rms(hx, npre_ref[...]) * (1.0 + sc2_ref[0]) + sh2_ref[0]
        hm_ref[0, rs] = _to_row_tiles(hm)
        hm_hi, hm_lo = _split_bf16(hm)
        logits = dot(hm_hi, wr_hi) + (dot(hm_hi, wr_lo) + dot(hm_lo, wr_hi))
        e = jnp.exp(logits - jnp.max(logits, axis=-1, keepdims=True))
        aff_ref[0, rs] = e / jnp.sum(e, axis=-1, keepdims=True)


def _mix(yf, yb, xs, z, f, gates, x, g1, sh2, sc2, consts, tm):
    bn, length, d = x.shape
    tok = lambda width: pl.BlockSpec((1, tm, width), lambda b, i: (b, i, 0))
    vec = pl.BlockSpec((1, 1, d), lambda b, i: (b, 0, 0))
    return pl.pallas_call(
        _mix_kernel,
        out_shape=[jax.ShapeDtypeStruct((bn, length, d), F32),
                   jax.ShapeDtypeStruct((bn, length) + ROW_TILE, F32),
                   jax.ShapeDtypeStruct((bn, length, N_EXPERTS), F32)],
        grid=(bn, length // tm),
        in_specs=[tok(D_INNER), tok(D_INNER), tok(D_INNER), tok(D_INNER), tok(F_WIDTH), tok(2 * D_MODEL), tok(d),
                  vec, vec, vec] + [_const_spec(c.shape) for c in consts],
        out_specs=[tok(d), pl.BlockSpec((1, tm) + ROW_TILE, lambda b, i: (b, i, 0, 0)), tok(N_EXPERTS)],
        compiler_params=_cparams(2, VMEM_LIMIT),
        name="merge_out_router",
    )(yf, yb, xs, z, f, gates, x, g1, sh2, sc2, *consts)


def _expert_kernel(idx_ref, idxn_ref, val_ref, hm_ref, wg_ref, wu_ref, wd_ref, ye_ref, buf0, buf1, sem,
                   *, cap, nb, nsteps):
    s = pl.program_id(0)
    bufs = (buf0, buf1)
    dot = functools.partial(jnp.dot, preferred_element_type=F32)

    def row_copy(k, j, b, r):
        return pltpu.make_async_copy(hm_ref.at[b, pl.ds(r, 1)], bufs[k].at[pl.ds(j, 1)], sem.at[k])

    def wait_rows(k):
        pltpu.make_async_copy(hm_ref.at[0, pl.ds(0, cap)], bufs[k], sem.at[k]).wait()

    b0 = (2 * s) % nb

    @pl.when(s == 0)
    def _():
        def issue(j, carry):
            row_copy(0, j, b0, idx_ref[0, 0, j]).start()
            return carry
        lax.fori_loop(0, cap, issue, 0)

    for k in range(2):
        wait_rows(k)
        x = _from_row_tiles(bufs[k][...]).astype(BF16)
        if k == 0:
            for j in range(cap):
                row_copy(1, j, b0 + 1, idx_ref[1, 0, j]).start()
            for j in range(cap):
                row_copy(0, j, (b0 + 2) % nb, idxn_ref[0, 0, j]).start()
        hid = (_silu(dot(x, wg_ref[0])) * dot(x, wu_ref[0])).astype(BF16)
        ye_ref[k] = _to_row_tiles(dot(hid, wd_ref[0]) * val_ref[k])

    @pl.when(s == nsteps - 1)
    def _():
        wait_rows(0)


def _experts(idx, vals, hm, wg, wu, wd):
    bn = hm.shape[0]
    ne, d, ff = wg.shape
    cap = idx.shape[-1]
    per = bn // 2
    nsteps = ne * per
    cur = lambda s: (s, 0, 0)
    nxt = lambda s: ((s + 1) % nsteps, 0, 0)
    wspec = pl.BlockSpec((1, d, ff), lambda s: (s // per, 0, 0))
    return pl.pallas_call(
        functools.partial(_expert_kernel, cap=cap, nb=bn, nsteps=nsteps),
        out_shape=jax.ShapeDtypeStruct((ne * bn, cap) + ROW_TILE, F32),
        grid=(nsteps,),
        in_specs=[pl.BlockSpec((2, 1, cap), cur, memory_space=pltpu.SMEM),
                  pl.BlockSpec((2, 1, cap), nxt, memory_space=pltpu.SMEM),
                  pl.BlockSpec((2, cap, 1), cur),
                  pl.BlockSpec(memory_space=pl.ANY),
                  wspec, wspec, pl.BlockSpec((1, ff, d), lambda s: (s // per, 0, 0))],
        out_specs=pl.BlockSpec((2, cap) + ROW_TILE, lambda s: (s, 0, 0, 0)),
        scratch_shapes=[pltpu.VMEM((cap,) + ROW_TILE, F32), pltpu.VMEM((cap,) + ROW_TILE, F32),
                        pltpu.SemaphoreType.DMA((2,))],
        compiler_params=_cparams(1, VMEM_LIMIT),
        name="expert_ffn",
    )(idx, idx, vals, hm, wg, wu, wd)


COMBINE_ROWS = 16


def _combine_kernel(idx_ref, ye_ref, o_ref, *, cap):
    @pl.when(pl.program_id(1) == 0)
    def _():
        o_ref[...] = jnp.zeros_like(o_ref)

    def body(i, carry):
        j0 = i * COMBINE_ROWS
        rows = [idx_ref[0, 0, j0 + u] for u in range(COMBINE_ROWS)]
        new = [o_ref[0, rows[u]] + ye_ref[0, j0 + u] for u in range(COMBINE_ROWS)]
        for u in range(COMBINE_ROWS):
            o_ref[0, rows[u]] = new[u]
        return carry

    lax.fori_loop(0, cap // COMBINE_ROWS, body, 0)


def _combine(idx, ye, bn, length):
    ne = N_EXPERTS
    cap = idx.shape[-1]
    return pl.pallas_call(
        functools.partial(_combine_kernel, cap=cap),
        out_shape=jax.ShapeDtypeStruct((bn, length) + ROW_TILE, F32),
        grid=(bn, ne),
        in_specs=[pl.BlockSpec((1, 1, cap), lambda b, e: (e * bn + b, 0, 0), memory_space=pltpu.SMEM),
                  pl.BlockSpec((1, cap) + ROW_TILE, lambda b, e: (e * bn + b, 0, 0, 0))],
        out_specs=pl.BlockSpec((1, length) + ROW_TILE, lambda b, e: (b, 0, 0, 0)),
        compiler_params=_cparams(2, VMEM_LIMIT),
        name="expert_combine",
    )(idx, ye)


def _final_kernel(hx_ref, ffn_ref, g2_ref, w_ref, o_ref):
    f3 = ffn_ref[0]
    ms = jnp.mean(f3 * f3, axis=(1, 2), keepdims=True)
    y = _from_row_tiles(f3 * lax.rsqrt(ms + RMS_EPS) * w_ref[...])
    o_ref[0] = hx_ref[0] + g2_ref[0] * y


def _final(hx, ffn, g2, w, tm):
    bn, length, d = hx.shape
    tok = pl.BlockSpec((1, tm, d), lambda b, i: (b, i, 0))
    w3 = w.reshape((1,) + ROW_TILE)
    return pl.pallas_call(
        _final_kernel,
        out_shape=jax.ShapeDtypeStruct((bn, length, d), F32),
        grid=(bn, length // tm),
        in_specs=[tok, pl.BlockSpec((1, tm) + ROW_TILE, lambda b, i: (b, i, 0, 0)),
                  pl.BlockSpec((1, 1, d), lambda b, i: (b, 0, 0)), _const_spec(w3.shape)],
        out_specs=tok,
        compiler_params=_cparams(2),
        name="ffn_residual",
    )(hx, ffn, g2, w3)


def kernel(x, c, ctx, c_ctx, w_mod, b_mod, norm_mix_pre, norm_mix_post, norm_ffn_pre, norm_ffn_post,
           w_in, conv_w, conv_b, dt_bias, a_log, d_skip, ssd_norm, w_fourier, w_ssd_out, b_gate, w_out,
           w_router, w_e_gate, w_e_up, w_e_down):
    bn, length, d = x.shape
    ctx_len = ctx.shape[1]
    l = 0
    row = lambda v: v.reshape(1, -1)

    cs = jnp.concatenate([c, c_ctx[None], jnp.zeros((7, d), F32)], axis=0)
    mods = _modulation(cs, w_mod[l], b_mod[l])
    mod_x = mods[:bn].reshape(bn, N_MOD, 1, d)
    mod_c = jnp.broadcast_to(mods[bn].reshape(1, N_MOD, 1, d), (bn, N_MOD, 1, d))
    sh1, sc1, g1, sh2, sc2, g2 = [mod_x[:, i] for i in range(N_MOD)]

    i1 = F_WIDTH
    i2 = i1 + D_INNER
    i3 = i2 + D_INNER
    i4 = i3 + BC_WIDTH
    i5 = i4 + BC_WIDTH
    i6 = i5 + 2 * N_HEADS
    wi = w_in[l]
    wdt = jnp.zeros((d, 256), F32).at[:, :N_HEADS].set(wi[:, i5:i5 + N_HEADS])
    wdt = wdt.at[:, 128:128 + N_HEADS].set(wi[:, i5 + N_HEADS:i6])
    dtb = jnp.zeros((1, 256), F32).at[0, :N_HEADS].set(dt_bias[l, 0]).at[0, 128:128 + N_HEADS].set(dt_bias[l, 1])
    dft_c, w1, tw_c, tw_s, w2 = _dft_constants()
    cw, cb = conv_w[l], conv_b[l]
    j1, j2 = D_INNER, D_INNER + BC_WIDTH
    w = dict(
        wuf=wi[:, :i1].astype(BF16), wz=wi[:, i1:i2].astype(BF16), wxs=wi[:, i2:i3].astype(BF16),
        wb=wi[:, i3:i4].astype(BF16), wc=wi[:, i4:i5].astype(BF16), wdt=wdt.astype(BF16),
        wg=wi[:, i6:].astype(BF16), dft=dft_c,
        cwx=cw[:, :j1], cbx=row(cb[:j1]), cwb=cw[:, j1:j2], cbb=row(cb[j1:j2]),
        cwc=cw[:, j2:], cbc=row(cb[j2:]), dtb=dtb, bg=row(b_gate[l]))
    gain_pre = row(norm_mix_pre[l])

    alog_pad = jnp.zeros((2, 1, 128), F32).at[:, 0, :N_HEADS].set(a_log[l])

    cxs, cbm, ccm, cdt = _in_projection(ctx, mod_c[:, 0], mod_c[:, 1], gain_pre, w, ctx_len, ctx_len, False)
    h_zero = jnp.zeros((bn, N_GROUPS, D_STATE, GROUP_COLS), F32)
    _, hc_f, _, hc_b = _ssd_scan(cxs, cbm, ccm, cdt, alog_pad, h_zero, h_zero)

    zr, zi, z, xs, bm, cm, dt, gates = _in_projection(x, sh1, sc1, gain_pre, w, GRID_W, 2 * INPROJ_SUB, True)
    f = _sequence_dft_real(zr, zi, w1, tw_c, tw_s, w2)
    yf, _, yb, _ = _ssd_scan(xs, bm, cm, dt, alog_pad, hc_f, hc_b)

    consts = [row(jnp.repeat(d_skip[l], HEAD_DIM)), row(ssd_norm[l]), row(norm_mix_post[l]),
              row(norm_ffn_pre[l]), w_ssd_out[l].astype(BF16), w_fourier[l].astype(BF16),
              w_out[l].astype(BF16), w_router[l]]
    hx, hm, aff = _mix(yf, yb, xs, z, f, gates, x, g1, sh2, sc2, consts, 2 * MIX_SUB)

    cap = EC_FACTOR * length // N_EXPERTS
    top_aff, top_idx = lax.top_k(jnp.transpose(aff, (2, 0, 1)), cap)
    idx = top_idx.reshape(N_EXPERTS * bn, 1, cap).astype(jnp.int32)
    vals = top_aff.reshape(N_EXPERTS * bn, cap, 1)
    ye = _experts(idx, vals, hm, w_e_gate[l].astype(BF16), w_e_up[l].astype(BF16),
                  w_e_down[l].astype(BF16))
    ffn = _combine(idx, ye, bn, length)
    return _final(hx, ffn, g2, row(norm_ffn_post[l]), 512)
```

```python
import functools

import numpy as np
import jax
import jax.numpy as jnp
from jax import lax
from jax.experimental import pallas as pl
from jax.experimental.pallas import tpu as pltpu

F32 = jnp.float32
BF16 = jnp.bfloat16
HIGHEST = lax.Precision.HIGHEST

D_MODEL = 1024
GRID_W = 64
F_GROUP_W = 128
F_WIDTH = 512
D_INNER = 1536
HEAD_DIM = 64
N_HEADS = 24
N_GROUPS = 4
HEADS_PER_GROUP = 6
D_STATE = 128
CHUNK = 128
BC_WIDTH = N_GROUPS * D_STATE
N_EXPERTS = 16
EC_FACTOR = 2
N_MOD = 6
RMS_EPS = 1e-6
GROUP_COLS = HEADS_PER_GROUP * HEAD_DIM

FFT_N1 = 128
FFT_N2 = 32
FFT_KB = 8

VMEM_LIMIT = 56 * 1024 * 1024


def _cparams(n_axes, vmem=None):
    return pltpu.CompilerParams(dimension_semantics=("arbitrary",) * n_axes,
                                vmem_limit_bytes=vmem)


def _const_spec(shape):
    nd = len(shape)
    return pl.BlockSpec(shape, lambda *_: (0,) * nd, pipeline_mode=pl.Buffered(1))


def _rms(x, w):
    ms = jnp.mean(x * x, axis=-1, keepdims=True)
    return x * lax.rsqrt(ms + RMS_EPS) * w


def _sigmoid(x):
    return 1.0 / (1.0 + jnp.exp(-x))


def _silu(x):
    return x * _sigmoid(x)


ROW_TILE = (8, 128)


def _to_row_tiles(v):
    parts = [v[:, c * 128:(c + 1) * 128] for c in range(ROW_TILE[0])]
    return jnp.swapaxes(jnp.stack(parts, axis=0), 0, 1)


def _from_row_tiles(v3):
    t = jnp.swapaxes(v3, 0, 1)
    return jnp.concatenate([t[c] for c in range(ROW_TILE[0])], axis=1)


def _mod_kernel(c_ref, w_ref, b_ref, o_ref):
    c = c_ref[...]
    o_ref[...] = jnp.dot(_silu(c), w_ref[...], precision=HIGHEST,
                         preferred_element_type=F32) + b_ref[...]


def _modulation(cs, w_mod, b_mod):
    rows = cs.shape[0]
    n = w_mod.shape[1]
    blk = D_MODEL
    return pl.pallas_call(
        _mod_kernel,
        out_shape=jax.ShapeDtypeStruct((rows, n), F32),
        grid=(n // blk,),
        in_specs=[pl.BlockSpec((rows, D_MODEL), lambda j: (0, 0)),
                  pl.BlockSpec((D_MODEL, blk), lambda j: (0, j)),
                  pl.BlockSpec((1, blk), lambda j: (0, j))],
        out_specs=pl.BlockSpec((rows, blk), lambda j: (0, j)),
        compiler_params=_cparams(1),
        name="modulation",
    )(cs, w_mod, b_mod.reshape(1, n))


def _conv_silu(xbc, cw_ref, cb_ref, row_len):
    tm = xbc.shape[0]
    pos = lax.broadcasted_iota(jnp.int32, (tm, 1), 0) % row_len
    prev = jnp.where(pos == 0, 0.0, pltpu.roll(xbc, 1, axis=0))
    nxt = jnp.where(pos == row_len - 1, 0.0, pltpu.roll(xbc, tm - 1, axis=0))
    out = prev * cw_ref[0:1, :] + xbc * cw_ref[1:2, :] + nxt * cw_ref[2:3, :] + cb_ref[...]
    return _silu(out)


INPROJ_SUB = 256


def _softplus(x):
    return jnp.maximum(x, 0.0) + jnp.log(1.0 + jnp.exp(-jnp.abs(x)))


def _inproj_kernel(x_ref, sh_ref, sc_ref, gain_ref, wuf_ref, wz_ref, wxs_ref, wb_ref, wc_ref,
                   wdt_ref, wg_ref, dft_ref, cwx_ref, cbx_ref, cwb_ref, cbb_ref, cwc_ref, cbc_ref,
                   dtb_ref, bg_ref,
                   zr_ref, zi_ref, z_ref, xs_ref, bm_ref, cm_ref, dt_ref, g_ref, *, row_len):
    dot = functools.partial(jnp.dot, preferred_element_type=F32)
    tm = x_ref.shape[1]
    sub = min(tm, INPROJ_SUB)
    for r0 in range(0, tm, sub):
        rs = slice(r0, r0 + sub)
        h = _rms(x_ref[0, rs], gain_ref[...]) * (1.0 + sc_ref[0]) + sh_ref[0]
        hb = h.astype(BF16)
        xs_ref[0, rs] = _conv_silu(dot(hb, wxs_ref[...]), cwx_ref, cbx_ref, row_len).astype(BF16)
        bm_ref[0, rs] = _conv_silu(dot(hb, wb_ref[...]), cwb_ref, cbb_ref, row_len).astype(BF16)
        cm_ref[0, rs] = _conv_silu(dot(hb, wc_ref[...]), cwc_ref, cbc_ref, row_len).astype(BF16)
        dt_ref[0, rs] = _softplus(dot(hb, wdt_ref[...]) + dtb_ref[...])
        if zr_ref is None:
            continue
        uf = dot(hb, wuf_ref[...]).astype(BF16)
        for g in range(F_WIDTH // F_GROUP_W):
            sl = slice(g * F_GROUP_W, (g + 1) * F_GROUP_W)
            zz = dot(uf[:, sl], dft_ref[...].astype(BF16))
            zr_ref[0, rs, sl] = zz[:, :F_GROUP_W].astype(BF16)
            zi_ref[0, rs, sl] = zz[:, F_GROUP_W:].astype(BF16)
        z_ref[0, rs] = dot(hb, wz_ref[...]).astype(BF16)
        g_ref[0, rs] = _sigmoid(dot(hb, wg_ref[...]) + bg_ref[...]).astype(BF16)


def _inproj_ctx_kernel(x_ref, sh_ref, sc_ref, gain_ref, wxs_ref, wb_ref, wc_ref, wdt_ref,
                       cwx_ref, cbx_ref, cwb_ref, cbb_ref, cwc_ref, cbc_ref, dtb_ref,
                       xs_ref, bm_ref, cm_ref, dt_ref, *, row_len):
    _inproj_kernel(x_ref, sh_ref, sc_ref, gain_ref, None, None, wxs_ref, wb_ref, wc_ref,
                   wdt_ref, None, None, cwx_ref, cbx_ref, cwb_ref, cbb_ref, cwc_ref, cbc_ref,
                   dtb_ref, None, None, None, None, xs_ref, bm_ref, cm_ref, dt_ref, None,
                   row_len=row_len)


def _in_projection(x, shift, scale, gain, w, row_len, tm, full):
    bn, length, d = x.shape
    grid = (bn, length // tm)
    tok = lambda width: pl.BlockSpec((1, tm, width), lambda b, i: (b, i, 0))
    vec = pl.BlockSpec((1, 1, d), lambda b, i: (b, 0, 0))
    out = lambda width, dt: jax.ShapeDtypeStruct((bn, length, width), dt)
    conv_ops = [w["cwx"], w["cbx"], w["cwb"], w["cbb"], w["cwc"], w["cbc"]]
    ssd_shapes = [out(D_INNER, BF16), out(BC_WIDTH, BF16), out(BC_WIDTH, BF16), out(256, F32)]
    ssd_specs = [tok(D_INNER), tok(BC_WIDTH), tok(BC_WIDTH), tok(256)]
    if full:
        ops = [x, shift, scale, gain, w["wuf"], w["wz"], w["wxs"], w["wb"], w["wc"], w["wdt"], w["wg"],
               w["dft"]] + conv_ops + [w["dtb"], w["bg"]]
        kern = functools.partial(_inproj_kernel, row_len=row_len)
        out_shape = [out(F_WIDTH, BF16), out(F_WIDTH, BF16), out(D_INNER, BF16)] + ssd_shapes + \
                    [out(2 * D_MODEL, BF16)]
        out_specs = [tok(F_WIDTH), tok(F_WIDTH), tok(D_INNER)] + ssd_specs + [tok(2 * D_MODEL)]
        name = "in_projection"
    else:
        ops = [x, shift, scale, gain, w["wxs"], w["wb"], w["wc"], w["wdt"]] + conv_ops + [w["dtb"]]
        kern = functools.partial(_inproj_ctx_kernel, row_len=row_len)
        out_shape = ssd_shapes
        out_specs = ssd_specs
        name = "in_projection_ctx"
    in_specs = [tok(d), vec, vec] + [_const_spec(o.shape) for o in ops[3:]]
    return pl.pallas_call(kern, out_shape=out_shape, grid=grid, in_specs=in_specs,
                          out_specs=out_specs, compiler_params=_cparams(2, VMEM_LIMIT),
                          name=name)(*ops)


def _dft_constants():
    j = np.arange(F_GROUP_W)
    ang = 2.0 * np.pi * np.outer(j, j) / F_GROUP_W
    dft_c = np.concatenate([np.cos(ang), -np.sin(ang)], axis=1)
    k1 = np.arange(FFT_N1)
    a1 = 2.0 * np.pi * np.outer(k1, k1) / FFT_N1
    c1, s1 = np.cos(a1), np.sin(a1)
    w1 = np.block([[c1, s1], [-s1, c1]])
    n2 = np.arange(FFT_N2)
    at = 2.0 * np.pi * np.outer(k1, n2) / (FFT_N1 * FFT_N2)
    tw_c = np.repeat(np.cos(at), 128, axis=1)
    tw_s = np.repeat(np.sin(at), 128, axis=1)
    a2 = 2.0 * np.pi * np.outer(n2, n2) / FFT_N2
    w2 = np.zeros((FFT_N2, FFT_KB, 2, FFT_KB, FFT_N2))
    for jj in range(FFT_KB):
        w2[:, jj, 0, jj, :] = np.cos(a2)
        w2[:, jj, 1, jj, :] = np.sin(a2)
    w2 = w2.reshape(FFT_N2 * FFT_KB, 2 * FFT_KB * FFT_N2)
    return (jnp.asarray(dft_c, F32), jnp.asarray(w1, F32), jnp.asarray(tw_c, F32),
            jnp.asarray(tw_s, F32), jnp.asarray(w2, F32))


FFT_QB = 16


def _seq_dft_kernel(zr_ref, zi_ref, w1_ref, twc_ref, tws_ref, w2_ref, o_ref, qr_s, qi_s, tr_s, ti_s, *, scale):
    w1 = w1_ref[...].astype(BF16)
    reps = F_WIDTH // 128
    for qb in range(FFT_N2 // FFT_QB):
        qs = slice(qb * FFT_QB, (qb + 1) * FFT_QB)
        zr_t = jnp.swapaxes(zr_ref[0, :, qs, :], 0, 1)
        zi_t = jnp.swapaxes(zi_ref[0, :, qs, :], 0, 1)
        for q in range(FFT_QB):
            n2 = qb * FFT_QB + q
            rhs = jnp.concatenate([zr_t[q], zi_t[q]], axis=0)
            t = jnp.dot(w1, rhs, preferred_element_type=F32)
            a, b = t[:FFT_N1], t[FFT_N1:]
            cw = jnp.concatenate([twc_ref[:, n2 * 128:(n2 + 1) * 128]] * reps, axis=1)
            sw = jnp.concatenate([tws_ref[:, n2 * 128:(n2 + 1) * 128]] * reps, axis=1)
            qr_s[q] = (a * cw + b * sw).astype(BF16)
            qi_s[q] = (b * cw - a * sw).astype(BF16)
        tr_s[:, qs, :] = jnp.swapaxes(qr_s[...], 0, 1)
        ti_s[:, qs, :] = jnp.swapaxes(qi_s[...], 0, 1)
    w2 = w2_ref[...].astype(BF16)
    rows = FFT_KB * FFT_N2
    for kb in range(FFT_N1 // FFT_KB):
        ks = slice(kb * FFT_KB, (kb + 1) * FFT_KB)
        rhs = jnp.concatenate([tr_s[ks].reshape(rows, F_WIDTH), ti_s[ks].reshape(rows, F_WIDTH)], axis=0)
        y = jnp.dot(w2, rhs, preferred_element_type=F32) * scale
        o_ref[0, :, ks, :] = y.reshape(FFT_N2, FFT_KB, F_WIDTH)


def _sequence_dft_real(zr, zi, w1, tw_c, tw_s, w2):
    bn, length, width = zr.shape
    blk = pl.BlockSpec((1, FFT_N1, FFT_N2, width), lambda b: (b, 0, 0, 0))
    scale = 1.0 / float(np.sqrt(length * F_GROUP_W))
    f = pl.pallas_call(
        functools.partial(_seq_dft_kernel, scale=scale),
        out_shape=jax.ShapeDtypeStruct((bn, FFT_N2, FFT_N1, width), F32),
        grid=(bn,),
        in_specs=[blk, blk, _const_spec(w1.shape), _const_spec(tw_c.shape), _const_spec(tw_s.shape),
                  _const_spec(w2.shape)],
        out_specs=pl.BlockSpec((1, FFT_N2, FFT_N1, width), lambda b: (b, 0, 0, 0)),
        scratch_shapes=[pltpu.VMEM((FFT_QB, FFT_N1, width), BF16)] * 2
                       + [pltpu.VMEM((FFT_N1, FFT_N2, width), BF16)] * 2,
        compiler_params=_cparams(1, VMEM_LIMIT),
        name="seq_dft",
    )(zr.reshape(bn, FFT_N1, FFT_N2, width), zi.reshape(bn, FFT_N1, FFT_N2, width), w1, tw_c, tw_s, w2)
    return f.reshape(bn, length, width)


SSD_SUB = 64
LOG2E = 1.4426950408889634


class _Dir:
    pass


def _ssd_prepare(dt_ref, alog_ref, reverse):
    q, sb = CHUNK, SSD_SUB
    d = _Dir()
    d.reverse = reverse
    r_i = lax.broadcasted_iota(jnp.int32, (q, q), 0)
    c_i = lax.broadcasted_iota(jnp.int32, (q, q), 1)
    tri = ((r_i <= c_i) if reverse else (r_i >= c_i)).astype(F32)
    a = -jnp.exp(alog_ref[0])
    d.dt = dt_ref[0]
    da = d.dt * (a * LOG2E)
    d.acum = jnp.dot(tri, da, precision=HIGHEST, preferred_element_type=F32)
    lane = lax.broadcasted_iota(jnp.int32, (sb, 128), 1)
    d.lo = lane < HEAD_DIM
    lo_h = lax.broadcasted_iota(jnp.int32, (q, 128), 1) < HEAD_DIM
    row_l = lax.broadcasted_iota(jnp.int32, (sb, 128), 0)
    s_l = jnp.where(d.lo, lane, lane - HEAD_DIM)
    d.mask = (s_l >= row_l) if reverse else (s_l <= row_l)
    d.order = (1, 0) if reverse else (0, 1)

    def halves(m):
        m_t = m.T
        m_sw = pltpu.roll(m_t, sb, axis=1)
        return (jnp.where(lo_h, m_t, m_sw), jnp.where(lo_h, m_sw, m_t))

    d.rt = halves(d.acum)
    d.rt_dt = halves(d.dt)
    return d


def _ssd_block_rows(d, step):
    sb = SSD_SUB
    blk = d.order[step]
    end_r = blk * sb if d.reverse else blk * sb + sb - 1
    end_row = d.acum[end_r:end_r + 1, :]
    if step == 0:
        base_row = jnp.zeros((1, 128), F32)
    else:
        pr = d.order[0] * sb if d.reverse else d.order[0] * sb + sb - 1
        base_row = d.acum[pr:pr + 1, :]
    d.rows8 = jnp.concatenate([base_row, jnp.exp2(end_row - base_row), jnp.zeros((6, 128), F32)], axis=0)
    d.rs = slice(blk * sb, (blk + 1) * sb)
    d.blk = blk
    d.acum_b = d.acum[d.rs]
    d.dw_b = d.dt[d.rs] * jnp.exp2(end_row - d.acum_b)


def _ssd_group_begin(d, refs, g):
    xs_ref, bm_ref, cm_ref, y_ref, st_ref = refs
    gs = slice(g * D_STATE, (g + 1) * D_STATE)
    bg = bm_ref[0, d.rs, gs]
    cg = cm_ref[0, d.rs, gs]
    bg_t = bg.astype(F32).T.astype(BF16)
    cb2 = jnp.dot(cg, jnp.concatenate([bg_t, bg_t], axis=1), preferred_element_type=F32)
    h_t = st_ref[g]
    yoff = jnp.dot(cg, h_t.astype(BF16), preferred_element_type=F32)
    d.grp = dict(bg_t=bg_t, cb2=cb2, h_t=h_t, yoff=yoff, xw=[], cd=[])


def _ssd_pair(d, refs, g, pr_i):
    xs_ref, bm_ref, cm_ref, y_ref, st_ref = refs
    lo = d.lo
    lo_row = lo[0:1]
    h0 = g * HEADS_PER_GROUP + 2 * pr_i
    h1 = h0 + 1
    ps = slice((g * 3 + pr_i) * 128, (g * 3 + pr_i + 1) * 128)
    pat = jnp.where(lo, h0, h1)
    colp = jnp.take_along_axis(d.acum_b, pat, axis=1)
    dwp = jnp.take_along_axis(d.dw_b, pat, axis=1)
    r8 = jnp.take_along_axis(d.rows8, pat[0:8], axis=1)
    base_p, cd_p = r8[0:1], r8[1:2]
    rt, rt_dt = d.rt[d.blk], d.rt_dt[d.blk]
    rowp = jnp.where(lo_row, rt[h0:h0 + 1, :], rt[h1:h1 + 1, :])
    dt_row = jnp.where(lo_row, rt_dt[h0:h0 + 1, :], rt_dt[h1:h1 + 1, :])
    gd = d.grp
    m = (jnp.exp2(jnp.where(d.mask, colp - rowp, -1e30)) * (gd["cb2"] * dt_row)).astype(BF16)
    x = xs_ref[0, d.rs, ps]
    zero = jnp.zeros_like(x)
    rhs = jnp.concatenate([jnp.where(lo, x, zero), jnp.where(lo, zero, x)], axis=0)
    ydiag = jnp.dot(m, rhs, preferred_element_type=F32)
    y = ydiag + gd["yoff"][:, pr_i * 128:(pr_i + 1) * 128] * jnp.exp2(colp - base_p)
    y_ref[0, d.rs, ps] = y.astype(BF16)
    gd["xw"].append((x.astype(F32) * dwp).astype(BF16))
    gd["cd"].append(cd_p)


def _ssd_group_end(d, refs, g):
    st_ref = refs[4]
    gd = d.grp
    xw = jnp.concatenate(gd["xw"], axis=1)
    cd = jnp.concatenate(gd["cd"], axis=1)
    st_ref[g] = gd["h_t"] * cd + jnp.dot(gd["bg_t"], xw, preferred_element_type=F32)


def _ssd_kernel(xf, bf, cf, df, af, hf, xb, bb, cb_, db, ab, hb, yf, hff, yb, hfb, stf, stb, *, nchunks):
    c = pl.program_id(1)

    @pl.when(c == 0)
    def _():
        stf[...] = hf[0]
        stb[...] = hb[0]

    refs = ((xf, bf, cf, yf, stf), (xb, bb, cb_, yb, stb))
    dirs = (_ssd_prepare(df, af, False), _ssd_prepare(db, ab, True))
    for step in range(CHUNK // SSD_SUB):
        for d in dirs:
            _ssd_block_rows(d, step)
        for g in range(N_GROUPS):
            for d, r in zip(dirs, refs):
                _ssd_group_begin(d, r, g)
            for pr_i in range(HEADS_PER_GROUP // 2):
                for d, r in zip(dirs, refs):
                    _ssd_pair(d, r, g, pr_i)
            for d, r in zip(dirs, refs):
                _ssd_group_end(d, r, g)

    @pl.when(c == nchunks - 1)
    def _():
        hff[0] = stf[...]
        hfb[0] = stb[...]


def _ssd_scan(xs, bm, cm, dt, alog_pad, h0f, h0b):
    bn, length, _ = xs.shape
    nchunks = length // CHUNK
    st_shape = (N_GROUPS, D_STATE, GROUP_COLS)
    st_spec = pl.BlockSpec((1,) + st_shape, lambda b, c: (b, 0, 0, 0))

    def specs(reverse):
        cc = (lambda c: nchunks - 1 - c) if reverse else (lambda c: c)
        di = 1 if reverse else 0
        tok = lambda width: pl.BlockSpec((1, CHUNK, width), lambda b, c: (b, cc(c), 0))
        ins = [tok(D_INNER), tok(BC_WIDTH), tok(BC_WIDTH),
               pl.BlockSpec((1, CHUNK, 128), lambda b, c: (b, cc(c), di)),
               pl.BlockSpec((1, 1, 128), lambda b, c: (di, 0, 0)), st_spec]
        return ins, [tok(D_INNER), st_spec]

    in_f, out_f = specs(False)
    in_b, out_b = specs(True)
    y_shape = jax.ShapeDtypeStruct((bn, length, D_INNER), BF16)
    h_shape = jax.ShapeDtypeStruct((bn,) + st_shape, F32)
    return pl.pallas_call(
        functools.partial(_ssd_kernel, nchunks=nchunks),
        out_shape=[y_shape, h_shape, y_shape, h_shape],
        grid=(bn, nchunks),
        in_specs=in_f + in_b,
        out_specs=out_f + out_b,
        scratch_shapes=[pltpu.VMEM(st_shape, F32), pltpu.VMEM(st_shape, F32)],
        compiler_params=_cparams(2),
        name="ssd_scan",
    )(xs, bm, cm, dt, alog_pad, h0f, xs, bm, cm, dt, alog_pad, h0b)


MIX_SUB = 256


def _split_bf16(v):
    hi = v.astype(BF16)
    return hi, (v - hi.astype(F32)).astype(BF16)


def _mix_kernel(yf_ref, yb_ref, xs_ref, z_ref, f_ref, g_ref, x_ref, g1_ref, sh2_ref, sc2_ref,
                dsk_ref, nssd_ref, npost_ref, npre_ref, wso_ref, wf_ref, wo_ref, wr_ref,
                hx_ref, hm_ref, aff_ref):
    dot = functools.partial(jnp.dot, preferred_element_type=F32)
    wr_hi, wr_lo = _split_bf16(wr_ref[...])
    for r0 in range(0, x_ref.shape[1], MIX_SUB):
        rs = slice(r0, r0 + MIX_SUB)
        y = (yf_ref[0, rs].astype(F32) + yb_ref[0, rs].astype(F32)
             + dsk_ref[...] * xs_ref[0, rs].astype(F32))
        y = _rms(y * _silu(z_ref[0, rs].astype(F32)), nssd_ref[...])
        s_branch = dot(y.astype(BF16), wso_ref[...])
        f_branch = dot(f_ref[0, rs].astype(BF16), wf_ref[...])
        gates = g_ref[0, rs].astype(F32)
        merged = gates[:, :D_MODEL] * f_branch + gates[:, D_MODEL:] * s_branch
        mix = dot(merged.astype(BF16), wo_ref[...])
        hx = x_ref[0, rs] + g1_ref[0] * _rms(mix, npost_ref[...])
        hx_ref[0, rs] = hx
        hm = _rms(hx, npre_ref[...]) * (1.0 + sc2_ref[0]) + sh2_ref[0]
        hm_ref[0, rs] = _to_row_tiles(hm)
        hm_hi, hm_lo = _split_bf16(hm)
        logits = dot(hm_hi, wr_hi) + (dot(hm_hi, wr_lo) + dot(hm_lo, wr_hi))
        real = lax.broadcasted_iota(jnp.int32, logits.shape, 1) < N_EXPERTS
        logits = jnp.where(real, logits, -1e30)
        e = jnp.exp(logits - jnp.max(logits, axis=-1, keepdims=True))
        aff = e / jnp.sum(e, axis=-1, keepdims=True)
        aff_ref[0, :, rs] = aff.T[:N_EXPERTS]


def _mix(yf, yb, xs, z, f, gates, x, g1, sh2, sc2, consts, tm):
    bn, length, d = x.shape
    tok = lambda width: pl.BlockSpec((1, tm, width), lambda b, i: (b, i, 0))
    vec = pl.BlockSpec((1, 1, d), lambda b, i: (b, 0, 0))
    return pl.pallas_call(
        _mix_kernel,
        out_shape=[jax.ShapeDtypeStruct((bn, length, d), F32),
                   jax.ShapeDtypeStruct((bn, length) + ROW_TILE, F32),
                   jax.ShapeDtypeStruct((bn, N_EXPERTS, length), F32)],
        grid=(bn, length // tm),
        in_specs=[tok(D_INNER), tok(D_INNER), tok(D_INNER), tok(D_INNER), tok(F_WIDTH), tok(2 * D_MODEL), tok(d),
                  vec, vec, vec] + [_const_spec(c.shape) for c in consts],
        out_specs=[tok(d), pl.BlockSpec((1, tm) + ROW_TILE, lambda b, i: (b, i, 0, 0)),
                   pl.BlockSpec((1, N_EXPERTS, tm), lambda b, i: (b, 0, i))],
        compiler_params=_cparams(2, VMEM_LIMIT),
        name="merge_out_router",
    )(yf, yb, xs, z, f, gates, x, g1, sh2, sc2, *consts)


LANES = 128


def _to_tile_rows(v):
    return jnp.concatenate([v[:, k * LANES:(k + 1) * LANES] for k in range(v.shape[1] // LANES)], axis=0)


def _route_kernel(aff_ref, idx_ref, val_ref, *, cap):
    I32 = jnp.int32
    a = aff_ref[0]
    ne, length = a.shape
    nt = length // LANES
    rows = nt * ne

    def search(i, t):
        cand = t | lax.shift_left(jnp.int32(1), 30 - i)
        cnt = jnp.sum((a >= pltpu.bitcast(cand, F32)).astype(F32), axis=1, keepdims=True)
        return jnp.where(cnt >= cap, cand, t)

    thr = pltpu.bitcast(lax.fori_loop(0, 31, search, jnp.zeros((ne, 1), I32)), F32)
    n_gt = jnp.sum((a > thr).astype(F32), axis=1, keepdims=True)
    need = cap - n_gt

    val_r = _to_tile_rows(a)
    tile_rows = lambda v: jnp.concatenate([v] * nt, axis=0)
    thr_r = tile_rows(thr)
    need_r = tile_rows(need)
    li = lax.broadcasted_iota(I32, (LANES, LANES), 0)
    lj = lax.broadcasted_iota(I32, (LANES, LANES), 1)
    upper = (li < lj).astype(BF16)
    ri = lax.broadcasted_iota(I32, (rows, rows), 0)
    rj = lax.broadcasted_iota(I32, (rows, rows), 1)
    earlier = (((ri & (ne - 1)) == (rj & (ne - 1))) & (rj < ri)).astype(BF16)

    def prefix(mask):
        local = jnp.dot(mask.astype(BF16), upper, preferred_element_type=F32)
        total = jnp.sum(mask.astype(F32), axis=1, keepdims=True)
        tot_b = jnp.broadcast_to(total, (rows, LANES)).astype(BF16)
        offs = jnp.dot(earlier, tot_b, preferred_element_type=F32)[:, 0:1]
        return local, offs

    eq = val_r == thr_r
    eq_local, eq_off = prefix(eq)
    sel = (val_r > thr_r) | (eq & (eq_local + eq_off < need_r))
    sel_local, sel_off = prefix(sel)

    lane = lax.broadcasted_iota(I32, (rows, LANES), 1)
    k_row = lax.shift_right_logical(lax.broadcasted_iota(I32, (rows, LANES), 0), int(np.log2(ne)))
    tok = jnp.where(sel, k_row * LANES + lane, -1)
    val = val_r
    dist = jnp.where(sel, lane - sel_local.astype(I32), 0)
    for s in range(int(np.log2(LANES))):
        sh = 1 << s
        mv = (tok >= 0) & ((lax.shift_right_logical(dist, s) & 1) == 1)
        in_tok = pltpu.roll(jnp.where(mv, tok, -1), LANES - sh, axis=1)
        in_val = pltpu.roll(val, LANES - sh, axis=1)
        in_dist = pltpu.roll(dist, LANES - sh, axis=1)
        arrive = in_tok >= 0
        stay = (tok >= 0) & jnp.logical_not(mv)
        tok = jnp.where(arrive, in_tok, jnp.where(stay, tok, -1))
        val = jnp.where(arrive, in_val, val)
        dist = jnp.where(arrive, in_dist, dist)

    off = sel_off.astype(I32)
    lane_e = lax.broadcasted_iota(I32, (ne, LANES), 1)
    ncol = cap // LANES
    out_tok = [jnp.zeros((ne, LANES), I32) for _ in range(ncol)]
    out_val = [jnp.zeros((ne, LANES), F32) for _ in range(ncol)]
    for k in range(nt):
        rs = slice(k * ne, (k + 1) * ne)
        o = off[rs]
        sh, col = o & (LANES - 1), lax.shift_right_logical(o, int(np.log2(LANES)))
        pat = (lane_e - sh) & (LANES - 1)
        r_tok = jnp.take_along_axis(tok[rs], pat, axis=1)
        r_val = jnp.take_along_axis(val[rs], pat, axis=1)
        ok = r_tok >= 0
        for c in range(ncol):
            here = ok & (((col == c) & (lane_e >= sh)) | ((col + 1 == c) & (lane_e < sh)))
            out_tok[c] = jnp.where(here, r_tok, out_tok[c])
            out_val[c] = jnp.where(here, r_val, out_val[c])
    idx_ref[0] = jnp.concatenate(out_tok, axis=1)
    val_ref[0] = jnp.concatenate(out_val, axis=1)


def _route(aff_t, cap):
    bn, ne, length = aff_t.shape
    assert ne & (ne - 1) == 0 and cap % LANES == 0 and length % LANES == 0
    return pl.pallas_call(
        functools.partial(_route_kernel, cap=cap),
        out_shape=[jax.ShapeDtypeStruct((bn, ne, cap), jnp.int32), jax.ShapeDtypeStruct((bn, ne, cap), F32)],
        grid=(bn,),
        in_specs=[pl.BlockSpec((1, ne, length), lambda b: (b, 0, 0))],
        out_specs=[pl.BlockSpec((1, ne, cap), lambda b: (b, 0, 0))] * 2,
        compiler_params=_cparams(1),
        name="expert_choice_route",
    )(aff_t)


def _expert_kernel(idx_ref, idxn_ref, val_ref, hm_ref, wg_ref, wu_ref, wd_ref, ye_ref, buf0, buf1, sem,
                   *, cap, nb, nsteps):
    s = pl.program_id(0)
    bufs = (buf0, buf1)
    dot = functools.partial(jnp.dot, preferred_element_type=F32)

    def row_copy(k, j, b, r):
        return pltpu.make_async_copy(hm_ref.at[b, pl.ds(r, 1)], bufs[k].at[pl.ds(j, 1)], sem.at[k])

    def wait_rows(k):
        pltpu.make_async_copy(hm_ref.at[0, pl.ds(0, cap)], bufs[k], sem.at[k]).wait()

    b0 = (2 * s) % nb

    @pl.when(s == 0)
    def _():
        def issue(j, carry):
            row_copy(0, j, b0, idx_ref[0, 0, j]).start()
            return carry
        lax.fori_loop(0, cap, issue, 0)

    for k in range(2):
        wait_rows(k)
        x = _from_row_tiles(bufs[k][...]).astype(BF16)
        if k == 0:
            for j in range(cap):
                row_copy(1, j, b0 + 1, idx_ref[1, 0, j]).start()
            for j in range(cap):
                row_copy(0, j, (b0 + 2) % nb, idxn_ref[0, 0, j]).start()
        hid = (_silu(dot(x, wg_ref[0])) * dot(x, wu_ref[0])).astype(BF16)
        ye_ref[k] = _to_row_tiles(dot(hid, wd_ref[0]) * val_ref[k])

    @pl.when(s == nsteps - 1)
    def _():
        wait_rows(0)


def _experts(idx, vals, hm, wg, wu, wd):
    bn = hm.shape[0]
    ne, d, ff = wg.shape
    cap = idx.shape[-1]
    per = bn // 2
    nsteps = ne * per
    cur = lambda s: (s, 0, 0)
    nxt = lambda s: ((s + 1) % nsteps, 0, 0)
    wspec = pl.BlockSpec((1, d, ff), lambda s: (s // per, 0, 0))
    return pl.pallas_call(
        functools.partial(_expert_kernel, cap=cap, nb=bn, nsteps=nsteps),
        out_shape=jax.ShapeDtypeStruct((ne * bn, cap) + ROW_TILE, F32),
        grid=(nsteps,),
        in_specs=[pl.BlockSpec((2, 1, cap), cur, memory_space=pltpu.SMEM),
                  pl.BlockSpec((2, 1, cap), nxt, memory_space=pltpu.SMEM),
                  pl.BlockSpec((2, cap, 1), cur),
                  pl.BlockSpec(memory_space=pl.ANY),
                  wspec, wspec, pl.BlockSpec((1, ff, d), lambda s: (s // per, 0, 0))],
        out_specs=pl.BlockSpec((2, cap) + ROW_TILE, lambda s: (s, 0, 0, 0)),
        scratch_shapes=[pltpu.VMEM((cap,) + ROW_TILE, F32), pltpu.VMEM((cap,) + ROW_TILE, F32),
                        pltpu.SemaphoreType.DMA((2,))],
        compiler_params=_cparams(1, VMEM_LIMIT),
        name="expert_ffn",
    )(idx, idx, vals, hm, wg, wu, wd)


COMBINE_ROWS = 16


def _combine_kernel(idx_ref, ye_ref, hx_ref, g2_ref, w_ref, o_ref, acc, *, cap, nb, rows_out):
    b = pl.program_id(0)
    e = pl.program_id(1)
    slot = b % 2

    @pl.when(jnp.logical_and(e == 0, b < nb))
    def _():
        acc[slot] = jnp.zeros(acc.shape[1:], F32)

    @pl.when(b < nb)
    def _():
        def body(i, carry):
            j0 = i * COMBINE_ROWS
            rows = [idx_ref[0, 0, j0 + u] for u in range(COMBINE_ROWS)]
            new = [acc[slot, rows[u]] + ye_ref[0, j0 + u] for u in range(COMBINE_ROWS)]
            for u in range(COMBINE_ROWS):
                acc[slot, rows[u]] = new[u]
            return carry

        lax.fori_loop(0, cap // COMBINE_ROWS, body, 0)

    @pl.when(b > 0)
    def _():
        f3 = acc[1 - slot, pl.ds(pl.multiple_of(e * rows_out, rows_out), rows_out)]
        ms = jnp.mean(f3 * f3, axis=(1, 2), keepdims=True)
        y = _from_row_tiles(f3 * lax.rsqrt(ms + RMS_EPS) * w_ref[...])
        o_ref[0] = hx_ref[0] + g2_ref[0] * y


def _combine_residual(idx, ye, hx, g2, w):
    bn, length, d = hx.shape
    ne = N_EXPERTS
    cap = idx.shape[-1]
    rows_out = length // ne
    w3 = w.reshape((1,) + ROW_TILE)
    cur = lambda b: jnp.minimum(b, bn - 1)
    prev = lambda b: jnp.maximum(b - 1, 0)
    tok_prev = pl.BlockSpec((1, rows_out, d), lambda b, e: (prev(b), jnp.where(b > 0, e, 0), 0))
    return pl.pallas_call(
        functools.partial(_combine_kernel, cap=cap, nb=bn, rows_out=rows_out),
        out_shape=jax.ShapeDtypeStruct((bn, length, d), F32),
        grid=(bn + 1, ne),
        in_specs=[pl.BlockSpec((1, 1, cap), lambda b, e: (e * bn + cur(b), 0, 0), memory_space=pltpu.SMEM),
                  pl.BlockSpec((1, cap) + ROW_TILE, lambda b, e: (e * bn + cur(b), 0, 0, 0)),
                  tok_prev,
                  pl.BlockSpec((1, 1, d), lambda b, e: (prev(b), 0, 0)),
                  _const_spec(w3.shape)],
        out_specs=tok_prev,
        scratch_shapes=[pltpu.VMEM((2, length) + ROW_TILE, F32)],
        compiler_params=_cparams(2, VMEM_LIMIT),
        name="expert_combine_residual",
    )(idx, ye, hx, g2, w3)


def kernel(x, c, ctx, c_ctx, w_mod, b_mod, norm_mix_pre, norm_mix_post, norm_ffn_pre, norm_ffn_post,
           w_in, conv_w, conv_b, dt_bias, a_log, d_skip, ssd_norm, w_fourier, w_ssd_out, b_gate, w_out,
           w_router, w_e_gate, w_e_up, w_e_down):
    bn, length, d = x.shape
    ctx_len = ctx.shape[1]
    l = 0
    row = lambda v: v.reshape(1, -1)

    cs = jnp.concatenate([c, c_ctx[None], jnp.zeros((7, d), F32)], axis=0)
    mods = _modulation(cs, w_mod[l], b_mod[l])
    mod_x = mods[:bn].reshape(bn, N_MOD, 1, d)
    mod_c = jnp.broadcast_to(mods[bn].reshape(1, N_MOD, 1, d), (bn, N_MOD, 1, d))
    sh1, sc1, g1, sh2, sc2, g2 = [mod_x[:, i] for i in range(N_MOD)]

    i1 = F_WIDTH
    i2 = i1 + D_INNER
    i3 = i2 + D_INNER
    i4 = i3 + BC_WIDTH
    i5 = i4 + BC_WIDTH
    i6 = i5 + 2 * N_HEADS
    wi = w_in[l]
    wdt = jnp.zeros((d, 256), F32).at[:, :N_HEADS].set(wi[:, i5:i5 + N_HEADS])
    wdt = wdt.at[:, 128:128 + N_HEADS].set(wi[:, i5 + N_HEADS:i6])
    dtb = jnp.zeros((1, 256), F32).at[0, :N_HEADS].set(dt_bias[l, 0]).at[0, 128:128 + N_HEADS].set(dt_bias[l, 1])
    dft_c, w1, tw_c, tw_s, w2 = _dft_constants()
    cw, cb = conv_w[l], conv_b[l]
    j1, j2 = D_INNER, D_INNER + BC_WIDTH
    w = dict(
        wuf=wi[:, :i1].astype(BF16), wz=wi[:, i1:i2].astype(BF16), wxs=wi[:, i2:i3].astype(BF16),
        wb=wi[:, i3:i4].astype(BF16), wc=wi[:, i4:i5].astype(BF16), wdt=wdt.astype(BF16),
        wg=wi[:, i6:].astype(BF16), dft=dft_c,
        cwx=cw[:, :j1], cbx=row(cb[:j1]), cwb=cw[:, j1:j2], cbb=row(cb[j1:j2]),
        cwc=cw[:, j2:], cbc=row(cb[j2:]), dtb=dtb, bg=row(b_gate[l]))
    gain_pre = row(norm_mix_pre[l])

    alog_pad = jnp.zeros((2, 1, 128), F32).at[:, 0, :N_HEADS].set(a_log[l])

    cxs, cbm, ccm, cdt = _in_projection(ctx, mod_c[:, 0], mod_c[:, 1], gain_pre, w, ctx_len, ctx_len, False)
    h_zero = jnp.zeros((bn, N_GROUPS, D_STATE, GROUP_COLS), F32)
    _, hc_f, _, hc_b = _ssd_scan(cxs, cbm, ccm, cdt, alog_pad, h_zero, h_zero)

    zr, zi, z, xs, bm, cm, dt, gates = _in_projection(x, sh1, sc1, gain_pre, w, GRID_W, 2 * INPROJ_SUB, True)
    f = _sequence_dft_real(zr, zi, w1, tw_c, tw_s, w2)
    yf, _, yb, _ = _ssd_scan(xs, bm, cm, dt, alog_pad, hc_f, hc_b)

    consts = [row(jnp.repeat(d_skip[l], HEAD_DIM)), row(ssd_norm[l]), row(norm_mix_post[l]),
              row(norm_ffn_pre[l]), w_ssd_out[l].astype(BF16), w_fourier[l].astype(BF16),
              w_out[l].astype(BF16), jnp.pad(w_router[l], ((0, 0), (0, LANES - N_EXPERTS)))]
    hx, hm, aff_t = _mix(yf, yb, xs, z, f, gates, x, g1, sh2, sc2, consts, 2 * MIX_SUB)

    cap = EC_FACTOR * length // N_EXPERTS
    top_idx, top_aff = _route(aff_t, cap)
    idx = jnp.swapaxes(top_idx, 0, 1).reshape(N_EXPERTS * bn, 1, cap)
    vals = jnp.swapaxes(top_aff, 0, 1).reshape(N_EXPERTS * bn, cap, 1)
    ye = _experts(idx, vals, hm, w_e_gate[l].astype(BF16), w_e_up[l].astype(BF16),
                  w_e_down[l].astype(BF16))
    return _combine_residual(idx, ye, hx, g2, row(norm_ffn_post[l]))
```

```python
import functools

import numpy as np
import jax
import jax.numpy as jnp
from jax import lax
from jax.experimental import pallas as pl
from jax.experimental.pallas import tpu as pltpu

F32 = jnp.float32
BF16 = jnp.bfloat16
HIGHEST = lax.Precision.HIGHEST

D_MODEL = 1024
GRID_W = 64
F_GROUP_W = 128
F_WIDTH = 512
D_INNER = 1536
HEAD_DIM = 64
N_HEADS = 24
N_GROUPS = 4
HEADS_PER_GROUP = 6
D_STATE = 128
CHUNK = 128
BC_WIDTH = N_GROUPS * D_STATE
N_EXPERTS = 16
EC_FACTOR = 2
N_MOD = 6
RMS_EPS = 1e-6
GROUP_COLS = HEADS_PER_GROUP * HEAD_DIM

FFT_N1 = 128
FFT_N2 = 32
FFT_KB = 8

VMEM_LIMIT = 56 * 1024 * 1024
LANES = 128


def _cparams(n_axes, vmem=None):
    return pltpu.CompilerParams(dimension_semantics=("arbitrary",) * n_axes,
                                vmem_limit_bytes=vmem)


def _const_spec(shape):
    nd = len(shape)
    return pl.BlockSpec(shape, lambda *_: (0,) * nd, pipeline_mode=pl.Buffered(1))


def _rms(x, w):
    ms = jnp.mean(x * x, axis=-1, keepdims=True)
    return x * lax.rsqrt(ms + RMS_EPS) * w


def _sigmoid(x):
    return 1.0 / (1.0 + jnp.exp(-x))


def _silu(x):
    return x * _sigmoid(x)


ROW_TILE = (8, 128)


def _to_row_tiles(v):
    parts = [v[:, c * 128:(c + 1) * 128] for c in range(ROW_TILE[0])]
    return jnp.swapaxes(jnp.stack(parts, axis=0), 0, 1)


def _from_row_tiles(v3):
    t = jnp.swapaxes(v3, 0, 1)
    return jnp.concatenate([t[c] for c in range(ROW_TILE[0])], axis=1)


def _mod_kernel(c_ref, w_ref, b_ref, o_ref):
    c = c_ref[...]
    o_ref[...] = jnp.dot(_silu(c), w_ref[...], precision=HIGHEST,
                         preferred_element_type=F32) + b_ref[...]


def _modulation(cs, w_mod, b_mod):
    rows = cs.shape[0]
    n = w_mod.shape[1]
    blk = D_MODEL
    return pl.pallas_call(
        _mod_kernel,
        out_shape=jax.ShapeDtypeStruct((rows, n), F32),
        grid=(n // blk,),
        in_specs=[pl.BlockSpec((rows, D_MODEL), lambda j: (0, 0)),
                  pl.BlockSpec((D_MODEL, blk), lambda j: (0, j)),
                  pl.BlockSpec((1, blk), lambda j: (0, j))],
        out_specs=pl.BlockSpec((rows, blk), lambda j: (0, j)),
        compiler_params=_cparams(1),
        name="modulation",
    )(cs, w_mod, b_mod.reshape(1, n))


def _conv_silu(xbc, cw_ref, cb_ref, row_len):
    tm = xbc.shape[0]
    pos = lax.broadcasted_iota(jnp.int32, (tm, 1), 0) % row_len
    prev = jnp.where(pos == 0, 0.0, pltpu.roll(xbc, 1, axis=0))
    nxt = jnp.where(pos == row_len - 1, 0.0, pltpu.roll(xbc, tm - 1, axis=0))
    out = prev * cw_ref[0:1, :] + xbc * cw_ref[1:2, :] + nxt * cw_ref[2:3, :] + cb_ref[...]
    return _silu(out)


INPROJ_SUB = 256


def _softplus(x):
    return jnp.maximum(x, 0.0) + jnp.log(1.0 + jnp.exp(-jnp.abs(x)))


def _inproj_kernel(x_ref, sh_ref, sc_ref, gain_ref, wuf_ref, wz_ref, wxs_ref, wb_ref, wc_ref,
                   wdt_ref, wg_ref, dft_ref, cwx_ref, cbx_ref, cwb_ref, cbb_ref, cwc_ref, cbc_ref,
                   dtb_ref, bg_ref,
                   zr_ref, zi_ref, z_ref, xs_ref, bm_ref, cm_ref, dt_ref, g_ref, *, row_len):
    dot = functools.partial(jnp.dot, preferred_element_type=F32)
    tm = x_ref.shape[1]
    sub = min(tm, INPROJ_SUB)
    for r0 in range(0, tm, sub):
        rs = slice(r0, r0 + sub)
        h = _rms(x_ref[0, rs], gain_ref[...]) * (1.0 + sc_ref[0]) + sh_ref[0]
        hb = h.astype(BF16)
        xs_ref[0, rs] = _conv_silu(dot(hb, wxs_ref[...]), cwx_ref, cbx_ref, row_len).astype(BF16)
        bm_ref[0, rs] = _conv_silu(dot(hb, wb_ref[...]), cwb_ref, cbb_ref, row_len).astype(BF16)
        cm_ref[0, rs] = _conv_silu(dot(hb, wc_ref[...]), cwc_ref, cbc_ref, row_len).astype(BF16)
        dt_ref[0, rs] = _softplus(dot(hb, wdt_ref[...]) + dtb_ref[...])
        if zr_ref is None:
            continue
        uf = dot(hb, wuf_ref[...]).astype(BF16)
        for g in range(F_WIDTH // F_GROUP_W):
            sl = slice(g * F_GROUP_W, (g + 1) * F_GROUP_W)
            zz = dot(uf[:, sl], dft_ref[...].astype(BF16))
            zr_ref[0, rs, sl] = zz[:, :F_GROUP_W].astype(BF16)
            zi_ref[0, rs, sl] = zz[:, F_GROUP_W:].astype(BF16)
        z_ref[0, rs] = dot(hb, wz_ref[...]).astype(BF16)
        g_ref[0, rs] = _sigmoid(dot(hb, wg_ref[...]) + bg_ref[...]).astype(BF16)


def _inproj_ctx_kernel(x_ref, sh_ref, sc_ref, gain_ref, wxs_ref, wb_ref, wc_ref, wdt_ref,
                       cwx_ref, cbx_ref, cwb_ref, cbb_ref, cwc_ref, cbc_ref, dtb_ref,
                       xs_ref, bm_ref, cm_ref, dt_ref, *, row_len):
    _inproj_kernel(x_ref, sh_ref, sc_ref, gain_ref, None, None, wxs_ref, wb_ref, wc_ref,
                   wdt_ref, None, None, cwx_ref, cbx_ref, cwb_ref, cbb_ref, cwc_ref, cbc_ref,
                   dtb_ref, None, None, None, None, xs_ref, bm_ref, cm_ref, dt_ref, None,
                   row_len=row_len)


def _in_projection(x, shift, scale, gain, w, row_len, tm, full):
    bn, length, d = x.shape
    grid = (bn, length // tm)
    tok = lambda width: pl.BlockSpec((1, tm, width), lambda b, i: (b, i, 0))
    vec = pl.BlockSpec((1, 1, d), lambda b, i: (b, 0, 0))
    out = lambda width, dt: jax.ShapeDtypeStruct((bn, length, width), dt)
    conv_ops = [w["cwx"], w["cbx"], w["cwb"], w["cbb"], w["cwc"], w["cbc"]]
    ssd_shapes = [out(D_INNER, BF16), out(BC_WIDTH, BF16), out(BC_WIDTH, BF16), out(256, F32)]
    ssd_specs = [tok(D_INNER), tok(BC_WIDTH), tok(BC_WIDTH), tok(256)]
    if full:
        ops = [x, shift, scale, gain, w["wuf"], w["wz"], w["wxs"], w["wb"], w["wc"], w["wdt"], w["wg"],
               w["dft"]] + conv_ops + [w["dtb"], w["bg"]]
        kern = functools.partial(_inproj_kernel, row_len=row_len)
        out_shape = [out(F_WIDTH, BF16), out(F_WIDTH, BF16), out(D_INNER, BF16)] + ssd_shapes + \
                    [out(2 * D_MODEL, BF16)]
        out_specs = [tok(F_WIDTH), tok(F_WIDTH), tok(D_INNER)] + ssd_specs + [tok(2 * D_MODEL)]
        name = "in_projection"
    else:
        ops = [x, shift, scale, gain, w["wxs"], w["wb"], w["wc"], w["wdt"]] + conv_ops + [w["dtb"]]
        kern = functools.partial(_inproj_ctx_kernel, row_len=row_len)
        out_shape = ssd_shapes
        out_specs = ssd_specs
        name = "in_projection_ctx"
    in_specs = [tok(d), vec, vec] + [_const_spec(o.shape) for o in ops[3:]]
    return pl.pallas_call(kern, out_shape=out_shape, grid=grid, in_specs=in_specs,
                          out_specs=out_specs, compiler_params=_cparams(2, VMEM_LIMIT),
                          name=name)(*ops)


def _dft_constants():
    j = np.arange(F_GROUP_W)
    ang = 2.0 * np.pi * np.outer(j, j) / F_GROUP_W
    dft_c = np.concatenate([np.cos(ang), -np.sin(ang)], axis=1)
    k1 = np.arange(FFT_N1)
    a1 = 2.0 * np.pi * np.outer(k1, k1) / FFT_N1
    c1, s1 = np.cos(a1), np.sin(a1)
    w1 = np.block([[c1, s1], [-s1, c1]])
    n2 = np.arange(FFT_N2)
    at = 2.0 * np.pi * np.outer(k1, n2) / (FFT_N1 * FFT_N2)
    tw_c = np.repeat(np.cos(at), 128, axis=1)
    tw_s = np.repeat(np.sin(at), 128, axis=1)
    a2 = 2.0 * np.pi * np.outer(n2, n2) / FFT_N2
    w2 = np.zeros((FFT_N2, FFT_KB, 2, FFT_KB, FFT_N2))
    for jj in range(FFT_KB):
        w2[:, jj, 0, jj, :] = np.cos(a2)
        w2[:, jj, 1, jj, :] = np.sin(a2)
    w2 = w2.reshape(FFT_N2 * FFT_KB, 2 * FFT_KB * FFT_N2)
    return (jnp.asarray(dft_c, F32), jnp.asarray(w1, F32), jnp.asarray(tw_c, F32),
            jnp.asarray(tw_s, F32), jnp.asarray(w2, F32))


FFT_QB = 16


def _seq_dft_kernel(zr_ref, zi_ref, w1_ref, twc_ref, tws_ref, w2_ref, o_ref, qr_s, qi_s, tr_s, ti_s, *, scale):
    w1 = w1_ref[...].astype(BF16)
    reps = F_WIDTH // 128
    for qb in range(FFT_N2 // FFT_QB):
        qs = slice(qb * FFT_QB, (qb + 1) * FFT_QB)
        zr_t = jnp.swapaxes(zr_ref[0, :, qs, :], 0, 1)
        zi_t = jnp.swapaxes(zi_ref[0, :, qs, :], 0, 1)
        for q in range(FFT_QB):
            n2 = qb * FFT_QB + q
            rhs = jnp.concatenate([zr_t[q], zi_t[q]], axis=0)
            t = jnp.dot(w1, rhs, preferred_element_type=F32)
            a, b = t[:FFT_N1], t[FFT_N1:]
            cw = jnp.concatenate([twc_ref[:, n2 * 128:(n2 + 1) * 128]] * reps, axis=1)
            sw = jnp.concatenate([tws_ref[:, n2 * 128:(n2 + 1) * 128]] * reps, axis=1)
            qr_s[q] = (a * cw + b * sw).astype(BF16)
            qi_s[q] = (b * cw - a * sw).astype(BF16)
        tr_s[:, qs, :] = jnp.swapaxes(qr_s[...], 0, 1)
        ti_s[:, qs, :] = jnp.swapaxes(qi_s[...], 0, 1)
    w2 = w2_ref[...].astype(BF16)
    rows = FFT_KB * FFT_N2
    for kb in range(FFT_N1 // FFT_KB):
        ks = slice(kb * FFT_KB, (kb + 1) * FFT_KB)
        rhs = jnp.concatenate([tr_s[ks].reshape(rows, F_WIDTH), ti_s[ks].reshape(rows, F_WIDTH)], axis=0)
        y = jnp.dot(w2, rhs, preferred_element_type=F32) * scale
        o_ref[0, :, ks, :] = y.reshape(FFT_N2, FFT_KB, F_WIDTH)


def _sequence_dft_real(zr, zi, w1, tw_c, tw_s, w2):
    bn, length, width = zr.shape
    blk = pl.BlockSpec((1, FFT_N1, FFT_N2, width), lambda b: (b, 0, 0, 0))
    scale = 1.0 / float(np.sqrt(length * F_GROUP_W))
    f = pl.pallas_call(
        functools.partial(_seq_dft_kernel, scale=scale),
        out_shape=jax.ShapeDtypeStruct((bn, FFT_N2, FFT_N1, width), F32),
        grid=(bn,),
        in_specs=[blk, blk, _const_spec(w1.shape), _const_spec(tw_c.shape), _const_spec(tw_s.shape),
                  _const_spec(w2.shape)],
        out_specs=pl.BlockSpec((1, FFT_N2, FFT_N1, width), lambda b: (b, 0, 0, 0)),
        scratch_shapes=[pltpu.VMEM((FFT_QB, FFT_N1, width), BF16)] * 2
                       + [pltpu.VMEM((FFT_N1, FFT_N2, width), BF16)] * 2,
        compiler_params=_cparams(1, VMEM_LIMIT),
        name="seq_dft",
    )(zr.reshape(bn, FFT_N1, FFT_N2, width), zi.reshape(bn, FFT_N1, FFT_N2, width), w1, tw_c, tw_s, w2)
    return f.reshape(bn, length, width)


SSD_SUB = 64
LOG2E = 1.4426950408889634


class _Dir:
    pass


def _ssd_prepare(dt_ref, alog_ref, reverse):
    q, sb = CHUNK, SSD_SUB
    d = _Dir()
    d.reverse = reverse
    r_i = lax.broadcasted_iota(jnp.int32, (q, q), 0)
    c_i = lax.broadcasted_iota(jnp.int32, (q, q), 1)
    tri = ((r_i <= c_i) if reverse else (r_i >= c_i)).astype(BF16)
    a = -jnp.exp(alog_ref[0])
    d.dt = dt_ref[0]
    da = d.dt * (a * LOG2E)
    p0 = da.astype(BF16)
    r1 = da - p0.astype(F32)
    p1 = r1.astype(BF16)
    p2 = (r1 - p1.astype(F32)).astype(BF16)
    parts = jnp.dot(tri, jnp.concatenate([p0, p1, p2], axis=1), preferred_element_type=F32)
    d.acum = parts[:, :LANES] + (parts[:, LANES:2 * LANES] + parts[:, 2 * LANES:])
    lane = lax.broadcasted_iota(jnp.int32, (sb, 128), 1)
    d.lo = lane < HEAD_DIM
    lo_h = lax.broadcasted_iota(jnp.int32, (q, 128), 1) < HEAD_DIM
    row_l = lax.broadcasted_iota(jnp.int32, (sb, 128), 0)
    s_l = jnp.where(d.lo, lane, lane - HEAD_DIM)
    d.mask = (s_l >= row_l) if reverse else (s_l <= row_l)
    d.order = (1, 0) if reverse else (0, 1)

    def halves(m):
        m_t = m.T
        m_sw = pltpu.roll(m_t, sb, axis=1)
        return (jnp.where(lo_h, m_t, m_sw), jnp.where(lo_h, m_sw, m_t))

    d.rt = halves(d.acum)
    d.rt_dt = halves(d.dt)
    return d


def _ssd_block_rows(d, step):
    sb = SSD_SUB
    blk = d.order[step]
    end_r = blk * sb if d.reverse else blk * sb + sb - 1
    end_row = d.acum[end_r:end_r + 1, :]
    if step == 0:
        base_row = jnp.zeros((1, 128), F32)
    else:
        pr = d.order[0] * sb if d.reverse else d.order[0] * sb + sb - 1
        base_row = d.acum[pr:pr + 1, :]
    d.rows8 = jnp.concatenate([base_row, jnp.exp2(end_row - base_row), jnp.zeros((6, 128), F32)], axis=0)
    d.rs = slice(blk * sb, (blk + 1) * sb)
    d.blk = blk
    d.acum_b = d.acum[d.rs]
    d.dw_b = d.dt[d.rs] * jnp.exp2(end_row - d.acum_b)


def _ssd_group_begin(d, refs, g):
    xs_ref, bm_ref, cm_ref, y_ref, st_ref = refs
    gs = slice(g * D_STATE, (g + 1) * D_STATE)
    bg = bm_ref[0, d.rs, gs]
    cg = cm_ref[0, d.rs, gs]
    bg_t = bg.astype(F32).T.astype(BF16)
    h_t = st_ref[g]
    d.grp = dict(bg_t=bg_t, h_t=h_t, xw=[], cd=[])
    if y_ref is not None:
        d.grp["cb2"] = jnp.dot(cg, jnp.concatenate([bg_t, bg_t], axis=1), preferred_element_type=F32)
        d.grp["yoff"] = jnp.dot(cg, h_t.astype(BF16), preferred_element_type=F32)


def _ssd_pair(d, refs, g, pr_i):
    xs_ref, bm_ref, cm_ref, y_ref, st_ref = refs
    lo = d.lo
    lo_row = lo[0:1]
    h0 = g * HEADS_PER_GROUP + 2 * pr_i
    h1 = h0 + 1
    ps = slice((g * 3 + pr_i) * 128, (g * 3 + pr_i + 1) * 128)
    pat = jnp.where(lo, h0, h1)
    dwp = jnp.take_along_axis(d.dw_b, pat, axis=1)
    r8 = jnp.take_along_axis(d.rows8, pat[0:8], axis=1)
    base_p, cd_p = r8[0:1], r8[1:2]
    x = xs_ref[0, d.rs, ps]
    gd = d.grp
    gd["xw"].append((x.astype(F32) * dwp).astype(BF16))
    gd["cd"].append(cd_p)
    if y_ref is None:
        return
    colp = jnp.take_along_axis(d.acum_b, pat, axis=1)
    rt, rt_dt = d.rt[d.blk], d.rt_dt[d.blk]
    rowp = jnp.where(lo_row, rt[h0:h0 + 1, :], rt[h1:h1 + 1, :])
    dt_row = jnp.where(lo_row, rt_dt[h0:h0 + 1, :], rt_dt[h1:h1 + 1, :])
    m = (jnp.exp2(jnp.where(d.mask, colp - rowp, -1e30)) * (gd["cb2"] * dt_row)).astype(BF16)
    zero = jnp.zeros_like(x)
    rhs = jnp.concatenate([jnp.where(lo, x, zero), jnp.where(lo, zero, x)], axis=0)
    ydiag = jnp.dot(m, rhs, preferred_element_type=F32)
    y = ydiag + gd["yoff"][:, pr_i * 128:(pr_i + 1) * 128] * jnp.exp2(colp - base_p)
    y_ref[0, d.rs, ps] = y.astype(BF16)


def _ssd_group_end(d, refs, g):
    st_ref = refs[4]
    gd = d.grp
    xw = jnp.concatenate(gd["xw"], axis=1)
    cd = jnp.concatenate(gd["cd"], axis=1)
    st_ref[g] = gd["h_t"] * cd + jnp.dot(gd["bg_t"], xw, preferred_element_type=F32)


def _ssd_kernel(xf, bf, cf, df, af, hf, xb, bb, cb_, db, ab, hb, *out_and_scratch, nchunks, emit_y):
    if emit_y:
        yf, hff, yb, hfb, stf, stb = out_and_scratch
    else:
        (hff, hfb, stf, stb), yf, yb = out_and_scratch, None, None
    c = pl.program_id(1)

    @pl.when(c == 0)
    def _():
        stf[...] = hf[0]
        stb[...] = hb[0]

    refs = ((xf, bf, cf, yf, stf), (xb, bb, cb_, yb, stb))
    dirs = (_ssd_prepare(df, af, False), _ssd_prepare(db, ab, True))
    for step in range(CHUNK // SSD_SUB):
        for d in dirs:
            _ssd_block_rows(d, step)
        for g in range(N_GROUPS):
            for d, r in zip(dirs, refs):
                _ssd_group_begin(d, r, g)
            for pr_i in range(HEADS_PER_GROUP // 2):
                for d, r in zip(dirs, refs):
                    _ssd_pair(d, r, g, pr_i)
            for d, r in zip(dirs, refs):
                _ssd_group_end(d, r, g)

    @pl.when(c == nchunks - 1)
    def _():
        hff[0] = stf[...]
        hfb[0] = stb[...]


def _ssd_scan(xs, bm, cm, dt, alog_pad, h0f, h0b, emit_y=True):
    bn, length, _ = xs.shape
    nchunks = length // CHUNK
    st_shape = (N_GROUPS, D_STATE, GROUP_COLS)
    st_spec = pl.BlockSpec((1,) + st_shape, lambda b, c: (b, 0, 0, 0))

    def specs(reverse):
        cc = (lambda c: nchunks - 1 - c) if reverse else (lambda c: c)
        di = 1 if reverse else 0
        tok = lambda width: pl.BlockSpec((1, CHUNK, width), lambda b, c: (b, cc(c), 0))
        ins = [tok(D_INNER), tok(BC_WIDTH), tok(BC_WIDTH),
               pl.BlockSpec((1, CHUNK, 128), lambda b, c: (b, cc(c), di)),
               pl.BlockSpec((1, 1, 128), lambda b, c: (di, 0, 0)), st_spec]
        return ins, ([tok(D_INNER)] if emit_y else []) + [st_spec]

    in_f, out_f = specs(False)
    in_b, out_b = specs(True)
    y_shape = [jax.ShapeDtypeStruct((bn, length, D_INNER), BF16)] if emit_y else []
    h_shape = [jax.ShapeDtypeStruct((bn,) + st_shape, F32)]
    return pl.pallas_call(
        functools.partial(_ssd_kernel, nchunks=nchunks, emit_y=emit_y),
        out_shape=y_shape + h_shape + y_shape + h_shape,
        grid=(bn, nchunks),
        in_specs=in_f + in_b,
        out_specs=out_f + out_b,
        scratch_shapes=[pltpu.VMEM(st_shape, F32), pltpu.VMEM(st_shape, F32)],
        compiler_params=_cparams(2),
        name="ssd_scan",
    )(xs, bm, cm, dt, alog_pad, h0f, xs, bm, cm, dt, alog_pad, h0b)


MIX_SUB = 128
MIX_TILE = 512


def _split_bf16(v):
    hi = v.astype(BF16)
    return hi, (v - hi.astype(F32)).astype(BF16)


def _mix_kernel(yf_ref, yb_ref, xs_ref, z_ref, f_ref, g_ref, x_ref, g1_ref, sh2_ref, sc2_ref,
                dsk_ref, nssd_ref, npost_ref, npre_ref, wso_ref, wf_ref, wo_ref, wr_ref,
                hx_ref, hm_ref, aff_ref):
    dot = functools.partial(jnp.dot, preferred_element_type=F32)
    wr_hi, wr_lo = _split_bf16(wr_ref[...])
    subs = [slice(r0, r0 + MIX_SUB) for r0 in range(0, x_ref.shape[1], MIX_SUB)]
    ys = []
    for rs in subs:
        y = (yf_ref[0, rs].astype(F32) + yb_ref[0, rs].astype(F32)
             + dsk_ref[...] * xs_ref[0, rs].astype(F32))
        ys.append(_rms(y * _silu(z_ref[0, rs].astype(F32)), nssd_ref[...]).astype(BF16))
    s_branches = [dot(y, wso_ref[...]) for y in ys]
    f_branches = [dot(f_ref[0, rs].astype(BF16), wf_ref[...]) for rs in subs]
    merged = []
    for rs, s_branch, f_branch in zip(subs, s_branches, f_branches):
        gates = g_ref[0, rs].astype(F32)
        merged.append((gates[:, :D_MODEL] * f_branch + gates[:, D_MODEL:] * s_branch).astype(BF16))
    mixes = [dot(m, wo_ref[...]) for m in merged]
    hms = []
    for rs, mix in zip(subs, mixes):
        hx = x_ref[0, rs] + g1_ref[0] * _rms(mix, npost_ref[...])
        hx_ref[0, rs] = hx
        hm = _rms(hx, npre_ref[...]) * (1.0 + sc2_ref[0]) + sh2_ref[0]
        hm_ref[0, rs] = _to_row_tiles(hm)
        hms.append(hm)
    for rs, hm in zip(subs, hms):
        hm_hi, hm_lo = _split_bf16(hm)
        logits = dot(hm_hi, wr_hi) + (dot(hm_hi, wr_lo) + dot(hm_lo, wr_hi))
        real = lax.broadcasted_iota(jnp.int32, logits.shape, 1) < N_EXPERTS
        logits = jnp.where(real, logits, -1e30)
        e = jnp.exp(logits - jnp.max(logits, axis=-1, keepdims=True))
        aff = e / jnp.sum(e, axis=-1, keepdims=True)
        aff_ref[0, :, rs] = aff.T[:N_EXPERTS]


def _mix(yf, yb, xs, z, f, gates, x, g1, sh2, sc2, consts, tm):
    bn, length, d = x.shape
    tok = lambda width: pl.BlockSpec((1, tm, width), lambda b, i: (b, i, 0))
    vec = pl.BlockSpec((1, 1, d), lambda b, i: (b, 0, 0))
    return pl.pallas_call(
        _mix_kernel,
        out_shape=[jax.ShapeDtypeStruct((bn, length, d), F32),
                   jax.ShapeDtypeStruct((bn, length) + ROW_TILE, F32),
                   jax.ShapeDtypeStruct((bn, N_EXPERTS, length), F32)],
        grid=(bn, length // tm),
        in_specs=[tok(D_INNER), tok(D_INNER), tok(D_INNER), tok(D_INNER), tok(F_WIDTH), tok(2 * D_MODEL), tok(d),
                  vec, vec, vec] + [_const_spec(c.shape) for c in consts],
        out_specs=[tok(d), pl.BlockSpec((1, tm) + ROW_TILE, lambda b, i: (b, i, 0, 0)),
                   pl.BlockSpec((1, N_EXPERTS, tm), lambda b, i: (b, 0, i))],
        compiler_params=_cparams(2, VMEM_LIMIT),
        name="merge_out_router",
    )(yf, yb, xs, z, f, gates, x, g1, sh2, sc2, *consts)


def _to_tile_rows(v):
    return jnp.concatenate([v[:, k * LANES:(k + 1) * LANES] for k in range(v.shape[1] // LANES)], axis=0)


def _route_kernel(aff_ref, idx_ref, val_ref, *, cap):
    I32 = jnp.int32
    a = aff_ref[0]
    ne, length = a.shape
    nt = length // LANES
    rows = nt * ne

    def search(i, t):
        cand = t | lax.shift_left(jnp.int32(1), 30 - i)
        cnt = jnp.sum((a >= pltpu.bitcast(cand, F32)).astype(F32), axis=1, keepdims=True)
        return jnp.where(cnt >= cap, cand, t)

    thr = pltpu.bitcast(lax.fori_loop(0, 31, search, jnp.zeros((ne, 1), I32)), F32)
    n_gt = jnp.sum((a > thr).astype(F32), axis=1, keepdims=True)
    need = cap - n_gt

    val_r = _to_tile_rows(a)
    tile_rows = lambda v: jnp.concatenate([v] * nt, axis=0)
    thr_r = tile_rows(thr)
    need_r = tile_rows(need)
    li = lax.broadcasted_iota(I32, (LANES, LANES), 0)
    lj = lax.broadcasted_iota(I32, (LANES, LANES), 1)
    upper = (li < lj).astype(BF16)
    ri = lax.broadcasted_iota(I32, (rows, rows), 0)
    rj = lax.broadcasted_iota(I32, (rows, rows), 1)
    earlier = (((ri & (ne - 1)) == (rj & (ne - 1))) & (rj < ri)).astype(BF16)

    def prefix(mask):
        local = jnp.dot(mask.astype(BF16), upper, preferred_element_type=F32)
        total = jnp.sum(mask.astype(F32), axis=1, keepdims=True)
        tot_b = jnp.broadcast_to(total, (rows, LANES)).astype(BF16)
        offs = jnp.dot(earlier, tot_b, preferred_element_type=F32)[:, 0:1]
        return local, offs

    eq = val_r == thr_r
    eq_local, eq_off = prefix(eq)
    sel = (val_r > thr_r) | (eq & (eq_local + eq_off < need_r))
    sel_local, sel_off = prefix(sel)

    lane = lax.broadcasted_iota(I32, (rows, LANES), 1)
    k_row = lax.shift_right_logical(lax.broadcasted_iota(I32, (rows, LANES), 0), int(np.log2(ne)))
    tok = jnp.where(sel, k_row * LANES + lane, -1)
    val = val_r
    dist = jnp.where(sel, lane - sel_local.astype(I32), 0)
    for s in range(int(np.log2(LANES))):
        sh = 1 << s
        mv = (tok >= 0) & ((lax.shift_right_logical(dist, s) & 1) == 1)
        in_tok = pltpu.roll(jnp.where(mv, tok, -1), LANES - sh, axis=1)
        in_val = pltpu.roll(val, LANES - sh, axis=1)
        in_dist = pltpu.roll(dist, LANES - sh, axis=1)
        arrive = in_tok >= 0
        stay = (tok >= 0) & jnp.logical_not(mv)
        tok = jnp.where(arrive, in_tok, jnp.where(stay, tok, -1))
        val = jnp.where(arrive, in_val, val)
        dist = jnp.where(arrive, in_dist, dist)

    off = sel_off.astype(I32)
    lane_e = lax.broadcasted_iota(I32, (ne, LANES), 1)
    ncol = cap // LANES
    out_tok = [jnp.zeros((ne, LANES), I32) for _ in range(ncol)]
    out_val = [jnp.zeros((ne, LANES), F32) for _ in range(ncol)]
    for k in range(nt):
        rs = slice(k * ne, (k + 1) * ne)
        o = off[rs]
        sh, col = o & (LANES - 1), lax.shift_right_logical(o, int(np.log2(LANES)))
        pat = (lane_e - sh) & (LANES - 1)
        r_tok = jnp.take_along_axis(tok[rs], pat, axis=1)
        r_val = jnp.take_along_axis(val[rs], pat, axis=1)
        ok = r_tok >= 0
        for c in range(ncol):
            here = ok & (((col == c) & (lane_e >= sh)) | ((col + 1 == c) & (lane_e < sh)))
            out_tok[c] = jnp.where(here, r_tok, out_tok[c])
            out_val[c] = jnp.where(here, r_val, out_val[c])
    idx_ref[0] = jnp.concatenate(out_tok, axis=1)
    val_ref[0] = jnp.concatenate(out_val, axis=1)


def _route(aff_t, cap):
    bn, ne, length = aff_t.shape
    assert ne & (ne - 1) == 0 and cap % LANES == 0 and length % LANES == 0
    return pl.pallas_call(
        functools.partial(_route_kernel, cap=cap),
        out_shape=[jax.ShapeDtypeStruct((bn, ne, cap), jnp.int32), jax.ShapeDtypeStruct((bn, ne, cap), F32)],
        grid=(bn,),
        in_specs=[pl.BlockSpec((1, ne, length), lambda b: (b, 0, 0))],
        out_specs=[pl.BlockSpec((1, ne, cap), lambda b: (b, 0, 0))] * 2,
        compiler_params=_cparams(1),
        name="expert_choice_route",
    )(aff_t)


def _expert_kernel(idx_ref, idxn_ref, val_ref, hm_ref, wg_ref, wu_ref, wd_ref, ye_ref, buf0, buf1, sem,
                   *, cap, nb, nsteps):
    s = pl.program_id(0)
    bufs = (buf0, buf1)
    dot = functools.partial(jnp.dot, preferred_element_type=F32)

    def row_copy(k, j, b, r):
        return pltpu.make_async_copy(hm_ref.at[b, pl.ds(r, 1)], bufs[k].at[pl.ds(j, 1)], sem.at[k])

    def wait_rows(k):
        pltpu.make_async_copy(hm_ref.at[0, pl.ds(0, cap)], bufs[k], sem.at[k]).wait()

    b0 = (2 * s) % nb

    @pl.when(s == 0)
    def _():
        def issue(j, carry):
            row_copy(0, j, b0, idx_ref[0, 0, j]).start()
            return carry
        lax.fori_loop(0, cap, issue, 0)

    for k in range(2):
        wait_rows(k)
        x = _from_row_tiles(bufs[k][...]).astype(BF16)
        if k == 0:
            for j in range(cap):
                row_copy(1, j, b0 + 1, idx_ref[1, 0, j]).start()
            for j in range(cap):
                row_copy(0, j, (b0 + 2) % nb, idxn_ref[0, 0, j]).start()
        hid = (_silu(dot(x, wg_ref[0])) * dot(x, wu_ref[0])).astype(BF16)
        ye_ref[k] = _to_row_tiles(dot(hid, wd_ref[0]) * val_ref[k])

    @pl.when(s == nsteps - 1)
    def _():
        wait_rows(0)


def _experts(idx, vals, hm, wg, wu, wd):
    bn = hm.shape[0]
    ne, d, ff = wg.shape
    cap = idx.shape[-1]
    per = bn // 2
    nsteps = ne * per
    cur = lambda s: (s, 0, 0)
    nxt = lambda s: ((s + 1) % nsteps, 0, 0)
    wspec = pl.BlockSpec((1, d, ff), lambda s: (s // per, 0, 0))
    return pl.pallas_call(
        functools.partial(_expert_kernel, cap=cap, nb=bn, nsteps=nsteps),
        out_shape=jax.ShapeDtypeStruct((ne * bn, cap) + ROW_TILE, F32),
        grid=(nsteps,),
        in_specs=[pl.BlockSpec((2, 1, cap), cur, memory_space=pltpu.SMEM),
                  pl.BlockSpec((2, 1, cap), nxt, memory_space=pltpu.SMEM),
                  pl.BlockSpec((2, cap, 1), cur),
                  pl.BlockSpec(memory_space=pl.ANY),
                  wspec, wspec, pl.BlockSpec((1, ff, d), lambda s: (s // per, 0, 0))],
        out_specs=pl.BlockSpec((2, cap) + ROW_TILE, lambda s: (s, 0, 0, 0)),
        scratch_shapes=[pltpu.VMEM((cap,) + ROW_TILE, F32), pltpu.VMEM((cap,) + ROW_TILE, F32),
                        pltpu.SemaphoreType.DMA((2,))],
        compiler_params=_cparams(1, VMEM_LIMIT),
        name="expert_ffn",
    )(idx, idx, vals, hm, wg, wu, wd)


COMBINE_ROWS = 16


def _combine_kernel(idx_ref, ye_ref, hx_ref, g2_ref, w_ref, o_ref, acc, *, cap, nb, rows_out):
    b = pl.program_id(0)
    e = pl.program_id(1)
    slot = b % 2
    acc_cur = acc.at[slot]

    @pl.when(jnp.logical_and(e == 0, b < nb))
    def _():
        acc_cur[...] = jnp.zeros(acc.shape[1:], F32)

    @pl.when(b < nb)
    def _():
        def body(i, carry):
            j0 = i * COMBINE_ROWS
            rows = [idx_ref[0, 0, j0 + u] for u in range(COMBINE_ROWS)]
            new = [acc_cur[rows[u]] + ye_ref[0, j0 + u] for u in range(COMBINE_ROWS)]
            for u in range(COMBINE_ROWS):
                acc_cur[rows[u]] = new[u]
            return carry

        lax.fori_loop(0, cap // COMBINE_ROWS, body, 0)

    @pl.when(b > 0)
    def _():
        f3 = acc[1 - slot, pl.ds(pl.multiple_of(e * rows_out, rows_out), rows_out)]
        ms = jnp.mean(f3 * f3, axis=(1, 2), keepdims=True)
        y = _from_row_tiles(f3 * lax.rsqrt(ms + RMS_EPS) * w_ref[...])
        o_ref[0] = hx_ref[0] + g2_ref[0] * y


def _combine_residual(idx, ye, hx, g2, w):
    bn, length, d = hx.shape
    ne = N_EXPERTS
    cap = idx.shape[-1]
    rows_out = length // ne
    w3 = w.reshape((1,) + ROW_TILE)
    cur = lambda b: jnp.minimum(b, bn - 1)
    prev = lambda b: jnp.maximum(b - 1, 0)
    tok_prev = pl.BlockSpec((1, rows_out, d), lambda b, e: (prev(b), jnp.where(b > 0, e, 0), 0))
    return pl.pallas_call(
        functools.partial(_combine_kernel, cap=cap, nb=bn, rows_out=rows_out),
        out_shape=jax.ShapeDtypeStruct((bn, length, d), F32),
        grid=(bn + 1, ne),
        in_specs=[pl.BlockSpec((1, 1, cap), lambda b, e: (e * bn + cur(b), 0, 0), memory_space=pltpu.SMEM),
                  pl.BlockSpec((1, cap) + ROW_TILE, lambda b, e: (e * bn + cur(b), 0, 0, 0)),
                  tok_prev,
                  pl.BlockSpec((1, 1, d), lambda b, e: (prev(b), 0, 0)),
                  _const_spec(w3.shape)],
        out_specs=tok_prev,
        scratch_shapes=[pltpu.VMEM((2, length) + ROW_TILE, F32)],
        compiler_params=_cparams(2, VMEM_LIMIT),
        name="expert_combine_residual",
    )(idx, ye, hx, g2, w3)


def kernel(x, c, ctx, c_ctx, w_mod, b_mod, norm_mix_pre, norm_mix_post, norm_ffn_pre, norm_ffn_post,
           w_in, conv_w, conv_b, dt_bias, a_log, d_skip, ssd_norm, w_fourier, w_ssd_out, b_gate, w_out,
           w_router, w_e_gate, w_e_up, w_e_down):
    bn, length, d = x.shape
    ctx_len = ctx.shape[1]
    l = 0
    row = lambda v: v.reshape(1, -1)

    cs = jnp.concatenate([c, c_ctx[None], jnp.zeros((7, d), F32)], axis=0)
    mods = _modulation(cs, w_mod[l], b_mod[l])
    mod_x = mods[:bn].reshape(bn, N_MOD, 1, d)
    mod_c = jnp.broadcast_to(mods[bn].reshape(1, N_MOD, 1, d), (bn, N_MOD, 1, d))
    sh1, sc1, g1, sh2, sc2, g2 = [mod_x[:, i] for i in range(N_MOD)]

    i1 = F_WIDTH
    i2 = i1 + D_INNER
    i3 = i2 + D_INNER
    i4 = i3 + BC_WIDTH
    i5 = i4 + BC_WIDTH
    i6 = i5 + 2 * N_HEADS
    wi = w_in[l]
    wdt = jnp.zeros((d, 256), F32).at[:, :N_HEADS].set(wi[:, i5:i5 + N_HEADS])
    wdt = wdt.at[:, 128:128 + N_HEADS].set(wi[:, i5 + N_HEADS:i6])
    dtb = jnp.zeros((1, 256), F32).at[0, :N_HEADS].set(dt_bias[l, 0]).at[0, 128:128 + N_HEADS].set(dt_bias[l, 1])
    dft_c, w1, tw_c, tw_s, w2 = _dft_constants()
    cw, cb = conv_w[l], conv_b[l]
    j1, j2 = D_INNER, D_INNER + BC_WIDTH
    w = dict(
        wuf=wi[:, :i1].astype(BF16), wz=wi[:, i1:i2].astype(BF16), wxs=wi[:, i2:i3].astype(BF16),
        wb=wi[:, i3:i4].astype(BF16), wc=wi[:, i4:i5].astype(BF16), wdt=wdt.astype(BF16),
        wg=wi[:, i6:].astype(BF16), dft=dft_c,
        cwx=cw[:, :j1], cbx=row(cb[:j1]), cwb=cw[:, j1:j2], cbb=row(cb[j1:j2]),
        cwc=cw[:, j2:], cbc=row(cb[j2:]), dtb=dtb, bg=row(b_gate[l]))
    gain_pre = row(norm_mix_pre[l])

    alog_pad = jnp.zeros((2, 1, 128), F32).at[:, 0, :N_HEADS].set(a_log[l])

    cxs, cbm, ccm, cdt = _in_projection(ctx, mod_c[:, 0], mod_c[:, 1], gain_pre, w, ctx_len, ctx_len, False)
    h_zero = jnp.zeros((bn, N_GROUPS, D_STATE, GROUP_COLS), F32)
    hc_f, hc_b = _ssd_scan(cxs, cbm, ccm, cdt, alog_pad, h_zero, h_zero, emit_y=False)

    zr, zi, z, xs, bm, cm, dt, gates = _in_projection(x, sh1, sc1, gain_pre, w, GRID_W, 2 * INPROJ_SUB, True)
    f = _sequence_dft_real(zr, zi, w1, tw_c, tw_s, w2)
    yf, _, yb, _ = _ssd_scan(xs, bm, cm, dt, alog_pad, hc_f, hc_b)

    consts = [row(jnp.repeat(d_skip[l], HEAD_DIM)), row(ssd_norm[l]), row(norm_mix_post[l]),
              row(norm_ffn_pre[l]), w_ssd_out[l].astype(BF16), w_fourier[l].astype(BF16),
              w_out[l].astype(BF16), jnp.pad(w_router[l], ((0, 0), (0, LANES - N_EXPERTS)))]
    hx, hm, aff_t = _mix(yf, yb, xs, z, f, gates, x, g1, sh2, sc2, consts, MIX_TILE)

    cap = EC_FACTOR * length // N_EXPERTS
    top_idx, top_aff = _route(aff_t, cap)
    idx = jnp.swapaxes(top_idx, 0, 1).reshape(N_EXPERTS * bn, 1, cap)
    vals = jnp.swapaxes(top_aff, 0, 1).reshape(N_EXPERTS * bn, cap, 1)
    ye = _experts(idx, vals, hm, w_e_gate[l].astype(BF16), w_e_up[l].astype(BF16),
                  w_e_down[l].astype(BF16))
    return _combine_residual(idx, ye, hx, g2, row(norm_ffn_post[l]))
```

```python
import functools

import numpy as np
import jax
import jax.numpy as jnp
from jax import lax
from jax.experimental import pallas as pl
from jax.experimental.pallas import tpu as pltpu

F32 = jnp.float32
BF16 = jnp.bfloat16
HIGHEST = lax.Precision.HIGHEST

D_MODEL = 1024
GRID_W = 64
F_GROUP_W = 128
F_WIDTH = 512
D_INNER = 1536
HEAD_DIM = 64
N_HEADS = 24
N_GROUPS = 4
HEADS_PER_GROUP = 6
D_STATE = 128
CHUNK = 128
BC_WIDTH = N_GROUPS * D_STATE
N_EXPERTS = 16
EC_FACTOR = 2
N_MOD = 6
RMS_EPS = 1e-6
GROUP_COLS = HEADS_PER_GROUP * HEAD_DIM

FFT_N1 = 128
FFT_N2 = 32
FFT_KB = 8

VMEM_LIMIT = 56 * 1024 * 1024
LANES = 128


def _cparams(n_axes, vmem=None):
    return pltpu.CompilerParams(dimension_semantics=("arbitrary",) * n_axes,
                                vmem_limit_bytes=vmem)


def _const_spec(shape):
    nd = len(shape)
    return pl.BlockSpec(shape, lambda *_: (0,) * nd, pipeline_mode=pl.Buffered(1))


def _rms(x, w):
    ms = jnp.mean(x * x, axis=-1, keepdims=True)
    return x * lax.rsqrt(ms + RMS_EPS) * w


def _sigmoid(x):
    return 1.0 / (1.0 + jnp.exp(-x))


def _silu(x):
    return x * _sigmoid(x)


ROW_TILE = (8, 128)


def _to_row_tiles(v):
    parts = [v[:, c * 128:(c + 1) * 128] for c in range(ROW_TILE[0])]
    return jnp.swapaxes(jnp.stack(parts, axis=0), 0, 1)


def _from_row_tiles(v3):
    t = jnp.swapaxes(v3, 0, 1)
    return jnp.concatenate([t[c] for c in range(ROW_TILE[0])], axis=1)


def _mod_kernel(c_ref, w_ref, b_ref, o_ref):
    c = c_ref[...]
    o_ref[...] = jnp.dot(_silu(c), w_ref[...], precision=HIGHEST,
                         preferred_element_type=F32) + b_ref[...]


def _modulation(cs, w_mod, b_mod):
    rows = cs.shape[0]
    n = w_mod.shape[1]
    blk = D_MODEL
    return pl.pallas_call(
        _mod_kernel,
        out_shape=jax.ShapeDtypeStruct((rows, n), F32),
        grid=(n // blk,),
        in_specs=[pl.BlockSpec((rows, D_MODEL), lambda j: (0, 0)),
                  pl.BlockSpec((D_MODEL, blk), lambda j: (0, j)),
                  pl.BlockSpec((1, blk), lambda j: (0, j))],
        out_specs=pl.BlockSpec((rows, blk), lambda j: (0, j)),
        compiler_params=_cparams(1),
        name="modulation",
    )(cs, w_mod, b_mod.reshape(1, n))


def _conv_silu(xbc, cw_ref, cb_ref, row_len):
    tm = xbc.shape[0]
    pos = lax.broadcasted_iota(jnp.int32, (tm, 1), 0) % row_len
    prev = jnp.where(pos == 0, 0.0, pltpu.roll(xbc, 1, axis=0))
    nxt = jnp.where(pos == row_len - 1, 0.0, pltpu.roll(xbc, tm - 1, axis=0))
    out = prev * cw_ref[0:1, :] + xbc * cw_ref[1:2, :] + nxt * cw_ref[2:3, :] + cb_ref[...]
    return _silu(out)


INPROJ_SUB = 256


def _softplus(x):
    return jnp.maximum(x, 0.0) + jnp.log(1.0 + jnp.exp(-jnp.abs(x)))


def _inproj_kernel(x_ref, sh_ref, sc_ref, gain_ref, wuf_ref, wz_ref, wxs_ref, wb_ref, wc_ref,
                   wdt_ref, wg_ref, dft_ref, cwx_ref, cbx_ref, cwb_ref, cbb_ref, cwc_ref, cbc_ref,
                   dtb_ref, bg_ref,
                   zr_ref, zi_ref, z_ref, xs_ref, bm_ref, cm_ref, dt_ref, g_ref, *, row_len):
    dot = functools.partial(jnp.dot, preferred_element_type=F32)
    tm = x_ref.shape[1]
    sub = min(tm, INPROJ_SUB)
    for r0 in range(0, tm, sub):
        rs = slice(r0, r0 + sub)
        h = _rms(x_ref[0, rs], gain_ref[...]) * (1.0 + sc_ref[0]) + sh_ref[0]
        hb = h.astype(BF16)
        xs_ref[0, rs] = _conv_silu(dot(hb, wxs_ref[...]), cwx_ref, cbx_ref, row_len).astype(BF16)
        bm_ref[0, rs] = _conv_silu(dot(hb, wb_ref[...]), cwb_ref, cbb_ref, row_len).astype(BF16)
        cm_ref[0, rs] = _conv_silu(dot(hb, wc_ref[...]), cwc_ref, cbc_ref, row_len).astype(BF16)
        dt_ref[0, rs] = _softplus(dot(hb, wdt_ref[...]) + dtb_ref[...])
        if zr_ref is None:
            continue
        uf = dot(hb, wuf_ref[...]).astype(BF16)
        for g in range(F_WIDTH // F_GROUP_W):
            sl = slice(g * F_GROUP_W, (g + 1) * F_GROUP_W)
            zz = dot(uf[:, sl], dft_ref[...].astype(BF16))
            zr_ref[0, rs, sl] = zz[:, :F_GROUP_W].astype(BF16)
            zi_ref[0, rs, sl] = zz[:, F_GROUP_W:].astype(BF16)
        z_ref[0, rs] = dot(hb, wz_ref[...]).astype(BF16)
        g_ref[0, rs] = _sigmoid(dot(hb, wg_ref[...]) + bg_ref[...]).astype(BF16)


def _inproj_ctx_kernel(x_ref, sh_ref, sc_ref, gain_ref, wxs_ref, wb_ref, wc_ref, wdt_ref,
                       cwx_ref, cbx_ref, cwb_ref, cbb_ref, cwc_ref, cbc_ref, dtb_ref,
                       xs_ref, bm_ref, cm_ref, dt_ref, *, row_len):
    _inproj_kernel(x_ref, sh_ref, sc_ref, gain_ref, None, None, wxs_ref, wb_ref, wc_ref,
                   wdt_ref, None, None, cwx_ref, cbx_ref, cwb_ref, cbb_ref, cwc_ref, cbc_ref,
                   dtb_ref, None, None, None, None, xs_ref, bm_ref, cm_ref, dt_ref, None,
                   row_len=row_len)


def _in_projection(x, shift, scale, gain, w, row_len, tm, full):
    bn, length, d = x.shape
    grid = (bn, length // tm)
    tok = lambda width: pl.BlockSpec((1, tm, width), lambda b, i: (b, i, 0))
    vec = pl.BlockSpec((1, 1, d), lambda b, i: (b, 0, 0))
    out = lambda width, dt: jax.ShapeDtypeStruct((bn, length, width), dt)
    conv_ops = [w["cwx"], w["cbx"], w["cwb"], w["cbb"], w["cwc"], w["cbc"]]
    ssd_shapes = [out(D_INNER, BF16), out(BC_WIDTH, BF16), out(BC_WIDTH, BF16), out(256, F32)]
    ssd_specs = [tok(D_INNER), tok(BC_WIDTH), tok(BC_WIDTH), tok(256)]
    if full:
        ops = [x, shift, scale, gain, w["wuf"], w["wz"], w["wxs"], w["wb"], w["wc"], w["wdt"], w["wg"],
               w["dft"]] + conv_ops + [w["dtb"], w["bg"]]
        kern = functools.partial(_inproj_kernel, row_len=row_len)
        out_shape = [out(F_WIDTH, BF16), out(F_WIDTH, BF16), out(D_INNER, BF16)] + ssd_shapes + \
                    [out(2 * D_MODEL, BF16)]
        out_specs = [tok(F_WIDTH), tok(F_WIDTH), tok(D_INNER)] + ssd_specs + [tok(2 * D_MODEL)]
        name = "in_projection"
    else:
        ops = [x, shift, scale, gain, w["wxs"], w["wb"], w["wc"], w["wdt"]] + conv_ops + [w["dtb"]]
        kern = functools.partial(_inproj_ctx_kernel, row_len=row_len)
        out_shape = ssd_shapes
        out_specs = ssd_specs
        name = "in_projection_ctx"
    in_specs = [tok(d), vec, vec] + [_const_spec(o.shape) for o in ops[3:]]
    return pl.pallas_call(kern, out_shape=out_shape, grid=grid, in_specs=in_specs,
                          out_specs=out_specs, compiler_params=_cparams(2, VMEM_LIMIT),
                          name=name)(*ops)


def _dft_constants():
    j = np.arange(F_GROUP_W)
    ang = 2.0 * np.pi * np.outer(j, j) / F_GROUP_W
    dft_c = np.concatenate([np.cos(ang), -np.sin(ang)], axis=1)
    k1 = np.arange(FFT_N1)
    a1 = 2.0 * np.pi * np.outer(k1, k1) / FFT_N1
    c1, s1 = np.cos(a1), np.sin(a1)
    w1 = np.block([[c1, s1], [-s1, c1]])
    n2 = np.arange(FFT_N2)
    at = 2.0 * np.pi * np.outer(k1, n2) / (FFT_N1 * FFT_N2)
    tw_c = np.repeat(np.cos(at), 128, axis=1)
    tw_s = np.repeat(np.sin(at), 128, axis=1)
    a2 = 2.0 * np.pi * np.outer(n2, n2) / FFT_N2
    w2 = np.zeros((FFT_N2, FFT_KB, 2, FFT_KB, FFT_N2))
    for jj in range(FFT_KB):
        w2[:, jj, 0, jj, :] = np.cos(a2)
        w2[:, jj, 1, jj, :] = np.sin(a2)
    w2 = w2.reshape(FFT_N2 * FFT_KB, 2 * FFT_KB * FFT_N2)
    return (jnp.asarray(dft_c, F32), jnp.asarray(w1, F32), jnp.asarray(tw_c, F32),
            jnp.asarray(tw_s, F32), jnp.asarray(w2, F32))


FFT_QB = 16


def _seq_dft_kernel(zr_ref, zi_ref, w1_ref, twc_ref, tws_ref, w2_ref, o_ref, qr_s, qi_s, tr_s, ti_s, *, scale):
    w1 = w1_ref[...].astype(BF16)
    reps = F_WIDTH // 128
    for qb in range(FFT_N2 // FFT_QB):
        qs = slice(qb * FFT_QB, (qb + 1) * FFT_QB)
        zr_t = jnp.swapaxes(zr_ref[0, :, qs, :], 0, 1)
        zi_t = jnp.swapaxes(zi_ref[0, :, qs, :], 0, 1)
        for q in range(FFT_QB):
            n2 = qb * FFT_QB + q
            rhs = jnp.concatenate([zr_t[q], zi_t[q]], axis=0)
            t = jnp.dot(w1, rhs, preferred_element_type=F32)
            a, b = t[:FFT_N1], t[FFT_N1:]
            cw = jnp.concatenate([twc_ref[:, n2 * 128:(n2 + 1) * 128]] * reps, axis=1)
            sw = jnp.concatenate([tws_ref[:, n2 * 128:(n2 + 1) * 128]] * reps, axis=1)
            qr_s[q] = (a * cw + b * sw).astype(BF16)
            qi_s[q] = (b * cw - a * sw).astype(BF16)
        tr_s[:, qs, :] = jnp.swapaxes(qr_s[...], 0, 1)
        ti_s[:, qs, :] = jnp.swapaxes(qi_s[...], 0, 1)
    w2 = w2_ref[...].astype(BF16)
    rows = FFT_KB * FFT_N2
    for kb in range(FFT_N1 // FFT_KB):
        ks = slice(kb * FFT_KB, (kb + 1) * FFT_KB)
        rhs = jnp.concatenate([tr_s[ks].reshape(rows, F_WIDTH), ti_s[ks].reshape(rows, F_WIDTH)], axis=0)
        y = jnp.dot(w2, rhs, preferred_element_type=F32) * scale
        o_ref[0, :, ks, :] = y.reshape(FFT_N2, FFT_KB, F_WIDTH)


def _sequence_dft_real(zr, zi, w1, tw_c, tw_s, w2):
    bn, length, width = zr.shape
    blk = pl.BlockSpec((1, FFT_N1, FFT_N2, width), lambda b: (b, 0, 0, 0))
    scale = 1.0 / float(np.sqrt(length * F_GROUP_W))
    f = pl.pallas_call(
        functools.partial(_seq_dft_kernel, scale=scale),
        out_shape=jax.ShapeDtypeStruct((bn, FFT_N2, FFT_N1, width), F32),
        grid=(bn,),
        in_specs=[blk, blk, _const_spec(w1.shape), _const_spec(tw_c.shape), _const_spec(tw_s.shape),
                  _const_spec(w2.shape)],
        out_specs=pl.BlockSpec((1, FFT_N2, FFT_N1, width), lambda b: (b, 0, 0, 0)),
        scratch_shapes=[pltpu.VMEM((FFT_QB, FFT_N1, width), BF16)] * 2
                       + [pltpu.VMEM((FFT_N1, FFT_N2, width), BF16)] * 2,
        compiler_params=_cparams(1, VMEM_LIMIT),
        name="seq_dft",
    )(zr.reshape(bn, FFT_N1, FFT_N2, width), zi.reshape(bn, FFT_N1, FFT_N2, width), w1, tw_c, tw_s, w2)
    return f.reshape(bn, length, width)


SSD_SUB = 64
LOG2E = 1.4426950408889634


class _Dir:
    pass


def _ssd_prepare(dt_ref, alog_ref, reverse):
    q, sb = CHUNK, SSD_SUB
    d = _Dir()
    d.reverse = reverse
    r_i = lax.broadcasted_iota(jnp.int32, (q, q), 0)
    c_i = lax.broadcasted_iota(jnp.int32, (q, q), 1)
    tri = ((r_i <= c_i) if reverse else (r_i >= c_i)).astype(BF16)
    a = -jnp.exp(alog_ref[0])
    d.dt = dt_ref[0]
    da = d.dt * (a * LOG2E)
    p0 = da.astype(BF16)
    r1 = da - p0.astype(F32)
    p1 = r1.astype(BF16)
    p2 = (r1 - p1.astype(F32)).astype(BF16)
    parts = jnp.dot(tri, jnp.concatenate([p0, p1, p2], axis=1), preferred_element_type=F32)
    d.acum = parts[:, :LANES] + (parts[:, LANES:2 * LANES] + parts[:, 2 * LANES:])
    lane = lax.broadcasted_iota(jnp.int32, (sb, 128), 1)
    d.lo = lane < HEAD_DIM
    lo_h = lax.broadcasted_iota(jnp.int32, (q, 128), 1) < HEAD_DIM
    row_l = lax.broadcasted_iota(jnp.int32, (sb, 128), 0)
    s_l = jnp.where(d.lo, lane, lane - HEAD_DIM)
    d.mask = (s_l >= row_l) if reverse else (s_l <= row_l)
    d.order = (1, 0) if reverse else (0, 1)

    def halves(m):
        m_t = m.T
        m_sw = pltpu.roll(m_t, sb, axis=1)
        return (jnp.where(lo_h, m_t, m_sw), jnp.where(lo_h, m_sw, m_t))

    d.rt = halves(d.acum)
    d.rt_dt = halves(d.dt)
    return d


def _ssd_block_rows(d, step):
    sb = SSD_SUB
    blk = d.order[step]
    end_r = blk * sb if d.reverse else blk * sb + sb - 1
    end_row = d.acum[end_r:end_r + 1, :]
    if step == 0:
        base_row = jnp.zeros((1, 128), F32)
    else:
        pr = d.order[0] * sb if d.reverse else d.order[0] * sb + sb - 1
        base_row = d.acum[pr:pr + 1, :]
    d.rows8 = jnp.concatenate([base_row, jnp.exp2(end_row - base_row), jnp.zeros((6, 128), F32)], axis=0)
    d.rs = slice(blk * sb, (blk + 1) * sb)
    d.blk = blk
    d.acum_b = d.acum[d.rs]
    d.dw_b = d.dt[d.rs] * jnp.exp2(end_row - d.acum_b)


def _ssd_group_begin(d, refs, g):
    xs_ref, bm_ref, cm_ref, y_ref, st_ref = refs
    gs = slice(g * D_STATE, (g + 1) * D_STATE)
    bg = bm_ref[0, d.rs, gs]
    cg = cm_ref[0, d.rs, gs]
    bg_t = bg.astype(F32).T.astype(BF16)
    h_t = st_ref[g]
    d.grp = dict(bg_t=bg_t, h_t=h_t, xw=[], cd=[])
    if y_ref is not None:
        d.grp["cb2"] = jnp.dot(cg, jnp.concatenate([bg_t, bg_t], axis=1), preferred_element_type=F32)
        d.grp["yoff"] = jnp.dot(cg, h_t.astype(BF16), preferred_element_type=F32)


def _ssd_pair(d, refs, g, pr_i):
    xs_ref, bm_ref, cm_ref, y_ref, st_ref = refs
    lo = d.lo
    lo_row = lo[0:1]
    h0 = g * HEADS_PER_GROUP + 2 * pr_i
    h1 = h0 + 1
    ps = slice((g * 3 + pr_i) * 128, (g * 3 + pr_i + 1) * 128)
    pat = jnp.where(lo, h0, h1)
    dwp = jnp.take_along_axis(d.dw_b, pat, axis=1)
    r8 = jnp.take_along_axis(d.rows8, pat[0:8], axis=1)
    base_p, cd_p = r8[0:1], r8[1:2]
    x = xs_ref[0, d.rs, ps]
    gd = d.grp
    gd["xw"].append((x.astype(F32) * dwp).astype(BF16))
    gd["cd"].append(cd_p)
    if y_ref is None:
        return
    colp = jnp.take_along_axis(d.acum_b, pat, axis=1)
    rt, rt_dt = d.rt[d.blk], d.rt_dt[d.blk]
    rowp = jnp.where(lo_row, rt[h0:h0 + 1, :], rt[h1:h1 + 1, :])
    dt_row = jnp.where(lo_row, rt_dt[h0:h0 + 1, :], rt_dt[h1:h1 + 1, :])
    m = (jnp.exp2(jnp.where(d.mask, colp - rowp, -1e30)) * (gd["cb2"] * dt_row)).astype(BF16)
    zero = jnp.zeros_like(x)
    rhs = jnp.concatenate([jnp.where(lo, x, zero), jnp.where(lo, zero, x)], axis=0)
    ydiag = jnp.dot(m, rhs, preferred_element_type=F32)
    y = ydiag + gd["yoff"][:, pr_i * 128:(pr_i + 1) * 128] * jnp.exp2(colp - base_p)
    y_ref[0, d.rs, ps] = y.astype(BF16)


def _ssd_group_end(d, refs, g):
    st_ref = refs[4]
    gd = d.grp
    xw = jnp.concatenate(gd["xw"], axis=1)
    cd = jnp.concatenate(gd["cd"], axis=1)
    st_ref[g] = gd["h_t"] * cd + jnp.dot(gd["bg_t"], xw, preferred_element_type=F32)


def _ssd_kernel(xf, bf, cf, df, af, hf, xb, bb, cb_, db, ab, hb, *out_and_scratch, nchunks, emit_y):
    if emit_y:
        yf, hff, yb, hfb, stf, stb = out_and_scratch
    else:
        (hff, hfb, stf, stb), yf, yb = out_and_scratch, None, None
    c = pl.program_id(1)

    @pl.when(c == 0)
    def _():
        stf[...] = hf[0]
        stb[...] = hb[0]

    refs = ((xf, bf, cf, yf, stf), (xb, bb, cb_, yb, stb))
    dirs = (_ssd_prepare(df, af, False), _ssd_prepare(db, ab, True))
    for step in range(CHUNK // SSD_SUB):
        for d in dirs:
            _ssd_block_rows(d, step)
        for g in range(N_GROUPS):
            for d, r in zip(dirs, refs):
                _ssd_group_begin(d, r, g)
            for pr_i in range(HEADS_PER_GROUP // 2):
                for d, r in zip(dirs, refs):
                    _ssd_pair(d, r, g, pr_i)
            for d, r in zip(dirs, refs):
                _ssd_group_end(d, r, g)

    @pl.when(c == nchunks - 1)
    def _():
        hff[0] = stf[...]
        hfb[0] = stb[...]


def _ssd_scan(xs, bm, cm, dt, alog_pad, h0f, h0b, emit_y=True):
    bn, length, _ = xs.shape
    nchunks = length // CHUNK
    st_shape = (N_GROUPS, D_STATE, GROUP_COLS)
    st_spec = pl.BlockSpec((1,) + st_shape, lambda b, c: (b, 0, 0, 0))

    def specs(reverse):
        cc = (lambda c: nchunks - 1 - c) if reverse else (lambda c: c)
        di = 1 if reverse else 0
        tok = lambda width: pl.BlockSpec((1, CHUNK, width), lambda b, c: (b, cc(c), 0))
        ins = [tok(D_INNER), tok(BC_WIDTH), tok(BC_WIDTH),
               pl.BlockSpec((1, CHUNK, 128), lambda b, c: (b, cc(c), di)),
               pl.BlockSpec((1, 1, 128), lambda b, c: (di, 0, 0)), st_spec]
        return ins, ([tok(D_INNER)] if emit_y else []) + [st_spec]

    in_f, out_f = specs(False)
    in_b, out_b = specs(True)
    y_shape = [jax.ShapeDtypeStruct((bn, length, D_INNER), BF16)] if emit_y else []
    h_shape = [jax.ShapeDtypeStruct((bn,) + st_shape, F32)]
    return pl.pallas_call(
        functools.partial(_ssd_kernel, nchunks=nchunks, emit_y=emit_y),
        out_shape=y_shape + h_shape + y_shape + h_shape,
        grid=(bn, nchunks),
        in_specs=in_f + in_b,
        out_specs=out_f + out_b,
        scratch_shapes=[pltpu.VMEM(st_shape, F32), pltpu.VMEM(st_shape, F32)],
        compiler_params=_cparams(2),
        name="ssd_scan",
    )(xs, bm, cm, dt, alog_pad, h0f, xs, bm, cm, dt, alog_pad, h0b)


MIX_SUB = 128
MIX_TILE = 512


def _split_bf16(v):
    hi = v.astype(BF16)
    return hi, (v - hi.astype(F32)).astype(BF16)


def _mix_kernel(yf_ref, yb_ref, xs_ref, z_ref, f_ref, g_ref, x_ref, g1_ref, sh2_ref, sc2_ref,
                dsk_ref, nssd_ref, npost_ref, npre_ref, wso_ref, wf_ref, wo_ref, wr_ref,
                hx_ref, hm_ref, aff_ref):
    dot = functools.partial(jnp.dot, preferred_element_type=F32)
    wr_hi, wr_lo = _split_bf16(wr_ref[...])
    subs = [slice(r0, r0 + MIX_SUB) for r0 in range(0, x_ref.shape[1], MIX_SUB)]
    ys = []
    for rs in subs:
        y = (yf_ref[0, rs].astype(F32) + yb_ref[0, rs].astype(F32)
             + dsk_ref[...] * xs_ref[0, rs].astype(F32))
        ys.append(_rms(y * _silu(z_ref[0, rs].astype(F32)), nssd_ref[...]).astype(BF16))
    s_branches = [dot(y, wso_ref[...]) for y in ys]
    f_branches = [dot(f_ref[0, rs].astype(BF16), wf_ref[...]) for rs in subs]
    merged = []
    for rs, s_branch, f_branch in zip(subs, s_branches, f_branches):
        gates = g_ref[0, rs].astype(F32)
        merged.append((gates[:, :D_MODEL] * f_branch + gates[:, D_MODEL:] * s_branch).astype(BF16))
    mixes = [dot(m, wo_ref[...]) for m in merged]
    hms = []
    for rs, mix in zip(subs, mixes):
        hx = x_ref[0, rs] + g1_ref[0] * _rms(mix, npost_ref[...])
        hx_ref[0, rs] = hx
        hm = _rms(hx, npre_ref[...]) * (1.0 + sc2_ref[0]) + sh2_ref[0]
        hm_ref[0, rs] = _to_row_tiles(hm)
        hms.append(hm)
    for rs, hm in zip(subs, hms):
        hm_hi, hm_lo = _split_bf16(hm)
        logits = dot(hm_hi, wr_hi) + (dot(hm_hi, wr_lo) + dot(hm_lo, wr_hi))
        real = lax.broadcasted_iota(jnp.int32, logits.shape, 1) < N_EXPERTS
        logits = jnp.where(real, logits, -1e30)
        e = jnp.exp(logits - jnp.max(logits, axis=-1, keepdims=True))
        aff = e / jnp.sum(e, axis=-1, keepdims=True)
        aff_ref[0, :, rs] = aff.T[:N_EXPERTS]


def _mix(yf, yb, xs, z, f, gates, x, g1, sh2, sc2, consts, tm):
    bn, length, d = x.shape
    tok = lambda width: pl.BlockSpec((1, tm, width), lambda b, i: (b, i, 0))
    vec = pl.BlockSpec((1, 1, d), lambda b, i: (b, 0, 0))
    return pl.pallas_call(
        _mix_kernel,
        out_shape=[jax.ShapeDtypeStruct((bn, length, d), F32),
                   jax.ShapeDtypeStruct((bn, length) + ROW_TILE, F32),
                   jax.ShapeDtypeStruct((bn, N_EXPERTS, length), F32)],
        grid=(bn, length // tm),
        in_specs=[tok(D_INNER), tok(D_INNER), tok(D_INNER), tok(D_INNER), tok(F_WIDTH), tok(2 * D_MODEL), tok(d),
                  vec, vec, vec] + [_const_spec(c.shape) for c in consts],
        out_specs=[tok(d), pl.BlockSpec((1, tm) + ROW_TILE, lambda b, i: (b, i, 0, 0)),
                   pl.BlockSpec((1, N_EXPERTS, tm), lambda b, i: (b, 0, i))],
        compiler_params=_cparams(2, VMEM_LIMIT),
        name="merge_out_router",
    )(yf, yb, xs, z, f, gates, x, g1, sh2, sc2, *consts)


def _to_tile_rows(v):
    return jnp.concatenate([v[:, k * LANES:(k + 1) * LANES] for k in range(v.shape[1] // LANES)], axis=0)


def _route_kernel(aff_ref, idx_ref, val_ref, *, cap):
    I32 = jnp.int32
    a = aff_ref[0]
    ne, length = a.shape
    nt = length // LANES
    rows = nt * ne

    def search(i, t):
        cand = t | lax.shift_left(jnp.int32(1), 30 - i)
        cnt = jnp.sum((a >= pltpu.bitcast(cand, F32)).astype(F32), axis=1, keepdims=True)
        return jnp.where(cnt >= cap, cand, t)

    thr = pltpu.bitcast(lax.fori_loop(0, 31, search, jnp.zeros((ne, 1), I32)), F32)
    n_gt = jnp.sum((a > thr).astype(F32), axis=1, keepdims=True)
    need = cap - n_gt

    val_r = _to_tile_rows(a)
    tile_rows = lambda v: jnp.concatenate([v] * nt, axis=0)
    thr_r = tile_rows(thr)
    need_r = tile_rows(need)
    li = lax.broadcasted_iota(I32, (LANES, LANES), 0)
    lj = lax.broadcasted_iota(I32, (LANES, LANES), 1)
    upper = (li < lj).astype(BF16)
    ri = lax.broadcasted_iota(I32, (rows, rows), 0)
    rj = lax.broadcasted_iota(I32, (rows, rows), 1)
    earlier = (((ri & (ne - 1)) == (rj & (ne - 1))) & (rj < ri)).astype(BF16)

    def prefix(mask):
        local = jnp.dot(mask.astype(BF16), upper, preferred_element_type=F32)
        total = jnp.sum(mask.astype(F32), axis=1, keepdims=True)
        tot_b = jnp.broadcast_to(total, (rows, LANES)).astype(BF16)
        offs = jnp.dot(earlier, tot_b, preferred_element_type=F32)[:, 0:1]
        return local, offs

    eq = val_r == thr_r
    eq_local, eq_off = prefix(eq)
    sel = (val_r > thr_r) | (eq & (eq_local + eq_off < need_r))
    sel_local, sel_off = prefix(sel)

    lane = lax.broadcasted_iota(I32, (rows, LANES), 1)
    k_row = lax.shift_right_logical(lax.broadcasted_iota(I32, (rows, LANES), 0), int(np.log2(ne)))
    tok = jnp.where(sel, k_row * LANES + lane, -1)
    val = val_r
    dist = jnp.where(sel, lane - sel_local.astype(I32), 0)
    for s in range(int(np.log2(LANES))):
        sh = 1 << s
        mv = (tok >= 0) & ((lax.shift_right_logical(dist, s) & 1) == 1)
        in_tok = pltpu.roll(jnp.where(mv, tok, -1), LANES - sh, axis=1)
        in_val = pltpu.roll(val, LANES - sh, axis=1)
        in_dist = pltpu.roll(dist, LANES - sh, axis=1)
        arrive = in_tok >= 0
        stay = (tok >= 0) & jnp.logical_not(mv)
        tok = jnp.where(arrive, in_tok, jnp.where(stay, tok, -1))
        val = jnp.where(arrive, in_val, val)
        dist = jnp.where(arrive, in_dist, dist)

    off = sel_off.astype(I32)
    lane_e = lax.broadcasted_iota(I32, (ne, LANES), 1)
    ncol = cap // LANES
    out_tok = [jnp.zeros((ne, LANES), I32) for _ in range(ncol)]
    out_val = [jnp.zeros((ne, LANES), F32) for _ in range(ncol)]
    for k in range(nt):
        rs = slice(k * ne, (k + 1) * ne)
        o = off[rs]
        sh, col = o & (LANES - 1), lax.shift_right_logical(o, int(np.log2(LANES)))
        pat = (lane_e - sh) & (LANES - 1)
        r_tok = jnp.take_along_axis(tok[rs], pat, axis=1)
        r_val = jnp.take_along_axis(val[rs], pat, axis=1)
        ok = r_tok >= 0
        for c in range(ncol):
            here = ok & (((col == c) & (lane_e >= sh)) | ((col + 1 == c) & (lane_e < sh)))
            out_tok[c] = jnp.where(here, r_tok, out_tok[c])
            out_val[c] = jnp.where(here, r_val, out_val[c])
    idx_ref[0] = jnp.concatenate(out_tok, axis=1)
    val_ref[0] = jnp.concatenate(out_val, axis=1)


def _route(aff_t, cap):
    bn, ne, length = aff_t.shape
    assert ne & (ne - 1) == 0 and cap % LANES == 0 and length % LANES == 0
    return pl.pallas_call(
        functools.partial(_route_kernel, cap=cap),
        out_shape=[jax.ShapeDtypeStruct((bn, ne, cap), jnp.int32), jax.ShapeDtypeStruct((bn, ne, cap), F32)],
        grid=(bn,),
        in_specs=[pl.BlockSpec((1, ne, length), lambda b: (b, 0, 0))],
        out_specs=[pl.BlockSpec((1, ne, cap), lambda b: (b, 0, 0))] * 2,
        compiler_params=_cparams(1),
        name="expert_choice_route",
    )(aff_t)


def _expert_kernel(idx_ref, idxn_ref, hm_ref, wg_ref, wu_ref, wd_ref, ye_ref, buf0, buf1, wgb, wub, wdb, sem,
                   *, cap, nb, nsteps):
    s = pl.program_id(0)
    bufs = (buf0, buf1)
    dot = functools.partial(jnp.dot, preferred_element_type=F32)

    def row_copy(k, j, b, r):
        return pltpu.make_async_copy(hm_ref.at[b, pl.ds(r, 1)], bufs[k].at[pl.ds(j, 1)], sem.at[k])

    def wait_rows(k):
        pltpu.make_async_copy(hm_ref.at[0, pl.ds(0, cap)], bufs[k], sem.at[k]).wait()

    b0 = (2 * s) % nb

    @pl.when(b0 == 0)
    def _():
        wgb[...] = wg_ref[0].astype(BF16)
        wub[...] = wu_ref[0].astype(BF16)
        wdb[...] = wd_ref[0].astype(BF16)

    @pl.when(s == 0)
    def _():
        def issue(j, carry):
            row_copy(0, j, b0, idx_ref[0, 0, j]).start()
            return carry
        lax.fori_loop(0, cap, issue, 0)

    for k in range(2):
        wait_rows(k)
        x = _from_row_tiles(bufs[k][...]).astype(BF16)
        if k == 0:
            for j in range(cap):
                row_copy(1, j, b0 + 1, idx_ref[1, 0, j]).start()
            for j in range(cap):
                row_copy(0, j, (b0 + 2) % nb, idxn_ref[0, 0, j]).start()
        hid = (_silu(dot(x, wgb[...])) * dot(x, wub[...])).astype(BF16)
        ye_ref[k] = _to_row_tiles(dot(hid, wdb[...]))

    @pl.when(s == nsteps - 1)
    def _():
        wait_rows(0)


def _experts(idx, hm, wg, wu, wd):
    bn = hm.shape[0]
    ne, d, ff = wg.shape
    cap = idx.shape[-1]
    per = bn // 2
    nsteps = ne * per
    cur = lambda s: (s, 0, 0)
    nxt = lambda s: ((s + 1) % nsteps, 0, 0)
    wspec = pl.BlockSpec((1, d, ff), lambda s: (s // per, 0, 0))
    return pl.pallas_call(
        functools.partial(_expert_kernel, cap=cap, nb=bn, nsteps=nsteps),
        out_shape=jax.ShapeDtypeStruct((ne * bn, cap) + ROW_TILE, F32),
        grid=(nsteps,),
        in_specs=[pl.BlockSpec((2, 1, cap), cur, memory_space=pltpu.SMEM),
                  pl.BlockSpec((2, 1, cap), nxt, memory_space=pltpu.SMEM),
                  pl.BlockSpec(memory_space=pl.ANY),
                  wspec, wspec, pl.BlockSpec((1, ff, d), lambda s: (s // per, 0, 0))],
        out_specs=pl.BlockSpec((2, cap) + ROW_TILE, lambda s: (s, 0, 0, 0)),
        scratch_shapes=[pltpu.VMEM((cap,) + ROW_TILE, F32), pltpu.VMEM((cap,) + ROW_TILE, F32),
                        pltpu.VMEM((d, ff), BF16), pltpu.VMEM((d, ff), BF16), pltpu.VMEM((ff, d), BF16),
                        pltpu.SemaphoreType.DMA((2,))],
        compiler_params=_cparams(1, VMEM_LIMIT),
        name="expert_ffn",
    )(idx, idx, hm, wg, wu, wd)


COMBINE_ROWS = 16


def _combine_kernel(idx_ref, val_ref, ye_ref, hx_ref, g2_ref, w_ref, o_ref, acc, *, cap, nb, rows_out):
    b = pl.program_id(0)
    e = pl.program_id(1)
    slot = b % 2
    acc_cur = acc.at[slot]

    @pl.when(jnp.logical_and(e == 0, b < nb))
    def _():
        acc_cur[...] = jnp.zeros(acc.shape[1:], F32)

    @pl.when(b < nb)
    def _():
        def body(i, carry):
            j0 = i * COMBINE_ROWS
            rows = [idx_ref[0, 0, j0 + u] for u in range(COMBINE_ROWS)]
            new = [acc_cur[rows[u]] + ye_ref[0, j0 + u] * val_ref[0, 0, j0 + u] for u in range(COMBINE_ROWS)]
            for u in range(COMBINE_ROWS):
                acc_cur[rows[u]] = new[u]
            return carry

        lax.fori_loop(0, cap // COMBINE_ROWS, body, 0)

    @pl.when(b > 0)
    def _():
        f3 = acc[1 - slot, pl.ds(pl.multiple_of(e * rows_out, rows_out), rows_out)]
        ms = jnp.mean(f3 * f3, axis=(1, 2), keepdims=True)
        y = _from_row_tiles(f3 * lax.rsqrt(ms + RMS_EPS) * w_ref[...])
        o_ref[0] = hx_ref[0] + g2_ref[0] * y


def _combine_residual(idx, vals, ye, hx, g2, w):
    bn, length, d = hx.shape
    ne = N_EXPERTS
    cap = idx.shape[-1]
    rows_out = length // ne
    w3 = w.reshape((1,) + ROW_TILE)
    cur = lambda b: jnp.minimum(b, bn - 1)
    prev = lambda b: jnp.maximum(b - 1, 0)
    tok_prev = pl.BlockSpec((1, rows_out, d), lambda b, e: (prev(b), jnp.where(b > 0, e, 0), 0))
    return pl.pallas_call(
        functools.partial(_combine_kernel, cap=cap, nb=bn, rows_out=rows_out),
        out_shape=jax.ShapeDtypeStruct((bn, length, d), F32),
        grid=(bn + 1, ne),
        in_specs=[pl.BlockSpec((1, 1, cap), lambda b, e: (e * bn + cur(b), 0, 0), memory_space=pltpu.SMEM),
                  pl.BlockSpec((1, 1, cap), lambda b, e: (e * bn + cur(b), 0, 0), memory_space=pltpu.SMEM),
                  pl.BlockSpec((1, cap) + ROW_TILE, lambda b, e: (e * bn + cur(b), 0, 0, 0)),
                  tok_prev,
                  pl.BlockSpec((1, 1, d), lambda b, e: (prev(b), 0, 0)),
                  _const_spec(w3.shape)],
        out_specs=tok_prev,
        scratch_shapes=[pltpu.VMEM((2, length) + ROW_TILE, F32)],
        compiler_params=_cparams(2, VMEM_LIMIT),
        name="expert_combine_residual",
    )(idx, vals, ye, hx, g2, w3)


def kernel(x, c, ctx, c_ctx, w_mod, b_mod, norm_mix_pre, norm_mix_post, norm_ffn_pre, norm_ffn_post,
           w_in, conv_w, conv_b, dt_bias, a_log, d_skip, ssd_norm, w_fourier, w_ssd_out, b_gate, w_out,
           w_router, w_e_gate, w_e_up, w_e_down):
    bn, length, d = x.shape
    ctx_len = ctx.shape[1]
    l = 0
    row = lambda v: v.reshape(1, -1)

    cs = jnp.concatenate([c, c_ctx[None], jnp.zeros((7, d), F32)], axis=0)
    mods = _modulation(cs, w_mod[l], b_mod[l])
    mod_x = mods[:bn].reshape(bn, N_MOD, 1, d)
    mod_c = jnp.broadcast_to(mods[bn].reshape(1, N_MOD, 1, d), (bn, N_MOD, 1, d))
    sh1, sc1, g1, sh2, sc2, g2 = [mod_x[:, i] for i in range(N_MOD)]

    i1 = F_WIDTH
    i2 = i1 + D_INNER
    i3 = i2 + D_INNER
    i4 = i3 + BC_WIDTH
    i5 = i4 + BC_WIDTH
    i6 = i5 + 2 * N_HEADS
    wi = w_in[l]
    wdt = jnp.zeros((d, 256), F32).at[:, :N_HEADS].set(wi[:, i5:i5 + N_HEADS])
    wdt = wdt.at[:, 128:128 + N_HEADS].set(wi[:, i5 + N_HEADS:i6])
    dtb = jnp.zeros((1, 256), F32).at[0, :N_HEADS].set(dt_bias[l, 0]).at[0, 128:128 + N_HEADS].set(dt_bias[l, 1])
    dft_c, w1, tw_c, tw_s, w2 = _dft_constants()
    cw, cb = conv_w[l], conv_b[l]
    j1, j2 = D_INNER, D_INNER + BC_WIDTH
    w = dict(
        wuf=wi[:, :i1].astype(BF16), wz=wi[:, i1:i2].astype(BF16), wxs=wi[:, i2:i3].astype(BF16),
        wb=wi[:, i3:i4].astype(BF16), wc=wi[:, i4:i5].astype(BF16), wdt=wdt.astype(BF16),
        wg=wi[:, i6:].astype(BF16), dft=dft_c,
        cwx=cw[:, :j1], cbx=row(cb[:j1]), cwb=cw[:, j1:j2], cbb=row(cb[j1:j2]),
        cwc=cw[:, j2:], cbc=row(cb[j2:]), dtb=dtb, bg=row(b_gate[l]))
    gain_pre = row(norm_mix_pre[l])

    alog_pad = jnp.zeros((2, 1, 128), F32).at[:, 0, :N_HEADS].set(a_log[l])

    cxs, cbm, ccm, cdt = _in_projection(ctx, mod_c[:, 0], mod_c[:, 1], gain_pre, w, ctx_len, ctx_len, False)
    h_zero = jnp.zeros((bn, N_GROUPS, D_STATE, GROUP_COLS), F32)
    hc_f, hc_b = _ssd_scan(cxs, cbm, ccm, cdt, alog_pad, h_zero, h_zero, emit_y=False)

    zr, zi, z, xs, bm, cm, dt, gates = _in_projection(x, sh1, sc1, gain_pre, w, GRID_W, 2 * INPROJ_SUB, True)
    f = _sequence_dft_real(zr, zi, w1, tw_c, tw_s, w2)
    yf, _, yb, _ = _ssd_scan(xs, bm, cm, dt, alog_pad, hc_f, hc_b)

    consts = [row(jnp.repeat(d_skip[l], HEAD_DIM)), row(ssd_norm[l]), row(norm_mix_post[l]),
              row(norm_ffn_pre[l]), w_ssd_out[l].astype(BF16), w_fourier[l].astype(BF16),
              w_out[l].astype(BF16), jnp.pad(w_router[l], ((0, 0), (0, LANES - N_EXPERTS)))]
    hx, hm, aff_t = _mix(yf, yb, xs, z, f, gates, x, g1, sh2, sc2, consts, MIX_TILE)

    cap = EC_FACTOR * length // N_EXPERTS
    top_idx, top_aff = _route(aff_t, cap)
    idx = jnp.swapaxes(top_idx, 0, 1).reshape(N_EXPERTS * bn, 1, cap)
    vals = jnp.swapaxes(top_aff, 0, 1).reshape(N_EXPERTS * bn, 1, cap)
    ye = _experts(idx, hm, w_e_gate[l], w_e_up[l], w_e_down[l])
    return _combine_residual(idx, vals, ye, hx, g2, row(norm_ffn_post[l]))
```

```python
import functools

import numpy as np
import jax
import jax.numpy as jnp
from jax import lax
from jax.experimental import pallas as pl
from jax.experimental.pallas import tpu as pltpu

F32 = jnp.float32
BF16 = jnp.bfloat16
HIGHEST = lax.Precision.HIGHEST

D_MODEL = 1024
GRID_W = 64
F_GROUP_W = 128
F_WIDTH = 512
D_INNER = 1536
HEAD_DIM = 64
N_HEADS = 24
N_GROUPS = 4
HEADS_PER_GROUP = 6
D_STATE = 128
CHUNK = 128
BC_WIDTH = N_GROUPS * D_STATE
N_EXPERTS = 16
EC_FACTOR = 2
N_MOD = 6
RMS_EPS = 1e-6
GROUP_COLS = HEADS_PER_GROUP * HEAD_DIM

FFT_N1 = 128
FFT_N2 = 32
FFT_KB = 8

VMEM_LIMIT = 56 * 1024 * 1024
LANES = 128


def _cparams(n_axes, vmem=None):
    return pltpu.CompilerParams(dimension_semantics=("arbitrary",) * n_axes,
                                vmem_limit_bytes=vmem)


def _const_spec(shape):
    nd = len(shape)
    return pl.BlockSpec(shape, lambda *_: (0,) * nd, pipeline_mode=pl.Buffered(1))


def _rms(x, w):
    ms = jnp.mean(x * x, axis=-1, keepdims=True)
    return x * lax.rsqrt(ms + RMS_EPS) * w


def _sigmoid(x):
    return 1.0 / (1.0 + jnp.exp(-x))


def _silu(x):
    return x * _sigmoid(x)


ROW_TILE = (8, 128)


def _to_row_tiles(v):
    parts = [v[:, c * 128:(c + 1) * 128] for c in range(ROW_TILE[0])]
    return jnp.swapaxes(jnp.stack(parts, axis=0), 0, 1)


def _from_row_tiles(v3):
    t = jnp.swapaxes(v3, 0, 1)
    return jnp.concatenate([t[c] for c in range(ROW_TILE[0])], axis=1)


def _mod_kernel(c_ref, w_ref, b_ref, o_ref):
    c = c_ref[...]
    o_ref[...] = jnp.dot(_silu(c), w_ref[...], precision=HIGHEST,
                         preferred_element_type=F32) + b_ref[...]


def _modulation(cs, w_mod, b_mod):
    rows = cs.shape[0]
    n = w_mod.shape[1]
    blk = D_MODEL
    return pl.pallas_call(
        _mod_kernel,
        out_shape=jax.ShapeDtypeStruct((rows, n), F32),
        grid=(n // blk,),
        in_specs=[pl.BlockSpec((rows, D_MODEL), lambda j: (0, 0)),
                  pl.BlockSpec((D_MODEL, blk), lambda j: (0, j)),
                  pl.BlockSpec((1, blk), lambda j: (0, j))],
        out_specs=pl.BlockSpec((rows, blk), lambda j: (0, j)),
        compiler_params=_cparams(1),
        name="modulation",
    )(cs, w_mod, b_mod.reshape(1, n))


def _conv_silu(xbc, cw_ref, cb_ref, row_len):
    tm = xbc.shape[0]
    pos = lax.broadcasted_iota(jnp.int32, (tm, 1), 0) % row_len
    prev = jnp.where(pos == 0, 0.0, pltpu.roll(xbc, 1, axis=0))
    nxt = jnp.where(pos == row_len - 1, 0.0, pltpu.roll(xbc, tm - 1, axis=0))
    out = prev * cw_ref[0:1, :] + xbc * cw_ref[1:2, :] + nxt * cw_ref[2:3, :] + cb_ref[...]
    return _silu(out)


INPROJ_SUB = 256


def _softplus(x):
    return jnp.maximum(x, 0.0) + jnp.log(1.0 + jnp.exp(-jnp.abs(x)))


def _inproj_kernel(x_ref, sh_ref, sc_ref, gain_ref, wuf_ref, wz_ref, wxs_ref, wb_ref, wc_ref,
                   wdt_ref, wg_ref, dft_ref, cwx_ref, cbx_ref, cwb_ref, cbb_ref, cwc_ref, cbc_ref,
                   dtb_ref, bg_ref,
                   zr_ref, zi_ref, z_ref, xs_ref, bm_ref, cm_ref, dt_ref, g_ref, *, row_len):
    dot = functools.partial(jnp.dot, preferred_element_type=F32)
    tm = x_ref.shape[1]
    sub = min(tm, INPROJ_SUB)
    for r0 in range(0, tm, sub):
        rs = slice(r0, r0 + sub)
        h = _rms(x_ref[0, rs], gain_ref[...]) * (1.0 + sc_ref[0]) + sh_ref[0]
        hb = h.astype(BF16)
        xs_ref[0, rs] = _conv_silu(dot(hb, wxs_ref[...]), cwx_ref, cbx_ref, row_len).astype(BF16)
        bm_ref[0, rs] = _conv_silu(dot(hb, wb_ref[...]), cwb_ref, cbb_ref, row_len).astype(BF16)
        cm_ref[0, rs] = _conv_silu(dot(hb, wc_ref[...]), cwc_ref, cbc_ref, row_len).astype(BF16)
        dt_ref[0, rs] = _softplus(dot(hb, wdt_ref[...]) + dtb_ref[...])
        if zr_ref is None:
            continue
        uf = dot(hb, wuf_ref[...]).astype(BF16)
        for g in range(F_WIDTH // F_GROUP_W):
            sl = slice(g * F_GROUP_W, (g + 1) * F_GROUP_W)
            zz = dot(uf[:, sl], dft_ref[...].astype(BF16))
            zr_ref[0, rs, sl] = zz[:, :F_GROUP_W].astype(BF16)
            zi_ref[0, rs, sl] = zz[:, F_GROUP_W:].astype(BF16)
        z_ref[0, rs] = dot(hb, wz_ref[...]).astype(BF16)
        g_ref[0, rs] = _sigmoid(dot(hb, wg_ref[...]) + bg_ref[...]).astype(BF16)


def _inproj_ctx_kernel(x_ref, sh_ref, sc_ref, gain_ref, wxs_ref, wb_ref, wc_ref, wdt_ref,
                       cwx_ref, cbx_ref, cwb_ref, cbb_ref, cwc_ref, cbc_ref, dtb_ref,
                       xs_ref, bm_ref, cm_ref, dt_ref, *, row_len):
    _inproj_kernel(x_ref, sh_ref, sc_ref, gain_ref, None, None, wxs_ref, wb_ref, wc_ref,
                   wdt_ref, None, None, cwx_ref, cbx_ref, cwb_ref, cbb_ref, cwc_ref, cbc_ref,
                   dtb_ref, None, None, None, None, xs_ref, bm_ref, cm_ref, dt_ref, None,
                   row_len=row_len)


def _in_projection(x, shift, scale, gain, w, row_len, tm, full):
    bn, length, d = x.shape
    grid = (bn, length // tm)
    tok = lambda width: pl.BlockSpec((1, tm, width), lambda b, i: (b, i, 0))
    vec = pl.BlockSpec((1, 1, d), lambda b, i: (b, 0, 0))
    out = lambda width, dt: jax.ShapeDtypeStruct((bn, length, width), dt)
    conv_ops = [w["cwx"], w["cbx"], w["cwb"], w["cbb"], w["cwc"], w["cbc"]]
    ssd_shapes = [out(D_INNER, BF16), out(BC_WIDTH, BF16), out(BC_WIDTH, BF16), out(256, F32)]
    ssd_specs = [tok(D_INNER), tok(BC_WIDTH), tok(BC_WIDTH), tok(256)]
    if full:
        ops = [x, shift, scale, gain, w["wuf"], w["wz"], w["wxs"], w["wb"], w["wc"], w["wdt"], w["wg"],
               w["dft"]] + conv_ops + [w["dtb"], w["bg"]]
        kern = functools.partial(_inproj_kernel, row_len=row_len)
        out_shape = [out(F_WIDTH, BF16), out(F_WIDTH, BF16), out(D_INNER, BF16)] + ssd_shapes + \
                    [out(2 * D_MODEL, BF16)]
        out_specs = [tok(F_WIDTH), tok(F_WIDTH), tok(D_INNER)] + ssd_specs + [tok(2 * D_MODEL)]
        name = "in_projection"
    else:
        ops = [x, shift, scale, gain, w["wxs"], w["wb"], w["wc"], w["wdt"]] + conv_ops + [w["dtb"]]
        kern = functools.partial(_inproj_ctx_kernel, row_len=row_len)
        out_shape = ssd_shapes
        out_specs = ssd_specs
        name = "in_projection_ctx"
    in_specs = [tok(d), vec, vec] + [_const_spec(o.shape) for o in ops[3:]]
    return pl.pallas_call(kern, out_shape=out_shape, grid=grid, in_specs=in_specs,
                          out_specs=out_specs, compiler_params=_cparams(2, VMEM_LIMIT),
                          name=name)(*ops)


def _dft_constants():
    j = np.arange(F_GROUP_W)
    ang = 2.0 * np.pi * np.outer(j, j) / F_GROUP_W
    dft_c = np.concatenate([np.cos(ang), -np.sin(ang)], axis=1)
    k1 = np.arange(FFT_N1)
    a1 = 2.0 * np.pi * np.outer(k1, k1) / FFT_N1
    c1, s1 = np.cos(a1), np.sin(a1)
    w1 = np.block([[c1, s1], [-s1, c1]])
    n2 = np.arange(FFT_N2)
    at = 2.0 * np.pi * np.outer(k1, n2) / (FFT_N1 * FFT_N2)
    tw_c = np.repeat(np.cos(at), 128, axis=1)
    tw_s = np.repeat(np.sin(at), 128, axis=1)
    a2 = 2.0 * np.pi * np.outer(n2, n2) / FFT_N2
    w2 = np.zeros((FFT_N2, FFT_KB, 2, FFT_KB, FFT_N2))
    for jj in range(FFT_KB):
        w2[:, jj, 0, jj, :] = np.cos(a2)
        w2[:, jj, 1, jj, :] = np.sin(a2)
    w2 = w2.reshape(FFT_N2 * FFT_KB, 2 * FFT_KB * FFT_N2)
    return (jnp.asarray(dft_c, F32), jnp.asarray(w1, F32), jnp.asarray(tw_c, F32),
            jnp.asarray(tw_s, F32), jnp.asarray(w2, F32))


FFT_QB = 16


def _seq_dft_kernel(zr_ref, zi_ref, w1_ref, twc_ref, tws_ref, w2_ref, o_ref, qr_s, qi_s, tr_s, ti_s, *, scale):
    w1 = w1_ref[...].astype(BF16)
    reps = F_WIDTH // 128
    for qb in range(FFT_N2 // FFT_QB):
        qs = slice(qb * FFT_QB, (qb + 1) * FFT_QB)
        zr_t = jnp.swapaxes(zr_ref[0, :, qs, :], 0, 1)
        zi_t = jnp.swapaxes(zi_ref[0, :, qs, :], 0, 1)
        for q in range(FFT_QB):
            n2 = qb * FFT_QB + q
            rhs = jnp.concatenate([zr_t[q], zi_t[q]], axis=0)
            t = jnp.dot(w1, rhs, preferred_element_type=F32)
            a, b = t[:FFT_N1], t[FFT_N1:]
            cw = jnp.concatenate([twc_ref[:, n2 * 128:(n2 + 1) * 128]] * reps, axis=1)
            sw = jnp.concatenate([tws_ref[:, n2 * 128:(n2 + 1) * 128]] * reps, axis=1)
            qr_s[q] = (a * cw + b * sw).astype(BF16)
            qi_s[q] = (b * cw - a * sw).astype(BF16)
        tr_s[:, qs, :] = jnp.swapaxes(qr_s[...], 0, 1)
        ti_s[:, qs, :] = jnp.swapaxes(qi_s[...], 0, 1)
    w2 = w2_ref[...].astype(BF16)
    rows = FFT_KB * FFT_N2
    for kb in range(FFT_N1 // FFT_KB):
        ks = slice(kb * FFT_KB, (kb + 1) * FFT_KB)
        rhs = jnp.concatenate([tr_s[ks].reshape(rows, F_WIDTH), ti_s[ks].reshape(rows, F_WIDTH)], axis=0)
        y = jnp.dot(w2, rhs, preferred_element_type=F32) * scale
        o_ref[0, :, ks, :] = y.reshape(FFT_N2, FFT_KB, F_WIDTH)


def _sequence_dft_real(zr, zi, w1, tw_c, tw_s, w2):
    bn, length, width = zr.shape
    blk = pl.BlockSpec((1, FFT_N1, FFT_N2, width), lambda b: (b, 0, 0, 0))
    scale = 1.0 / float(np.sqrt(length * F_GROUP_W))
    f = pl.pallas_call(
        functools.partial(_seq_dft_kernel, scale=scale),
        out_shape=jax.ShapeDtypeStruct((bn, FFT_N2, FFT_N1, width), F32),
        grid=(bn,),
        in_specs=[blk, blk, _const_spec(w1.shape), _const_spec(tw_c.shape), _const_spec(tw_s.shape),
                  _const_spec(w2.shape)],
        out_specs=pl.BlockSpec((1, FFT_N2, FFT_N1, width), lambda b: (b, 0, 0, 0)),
        scratch_shapes=[pltpu.VMEM((FFT_QB, FFT_N1, width), BF16)] * 2
                       + [pltpu.VMEM((FFT_N1, FFT_N2, width), BF16)] * 2,
        compiler_params=_cparams(1, VMEM_LIMIT),
        name="seq_dft",
    )(zr.reshape(bn, FFT_N1, FFT_N2, width), zi.reshape(bn, FFT_N1, FFT_N2, width), w1, tw_c, tw_s, w2)
    return f.reshape(bn, length, width)


SSD_SUB = 64
SSD_CHUNKS_PER_STEP = 4
LOG2E = 1.4426950408889634


class _Dir:
    pass


def _ssd_prepare(dt_ref, alog_ref, reverse, row0):
    q, sb = CHUNK, SSD_SUB
    d = _Dir()
    d.reverse = reverse
    r_i = lax.broadcasted_iota(jnp.int32, (q, q), 0)
    c_i = lax.broadcasted_iota(jnp.int32, (q, q), 1)
    tri = ((r_i <= c_i) if reverse else (r_i >= c_i)).astype(BF16)
    a = -jnp.exp(alog_ref[0])
    d.row0 = row0
    d.dt = dt_ref[0, row0:row0 + q]
    da = d.dt * (a * LOG2E)
    p0 = da.astype(BF16)
    r1 = da - p0.astype(F32)
    p1 = r1.astype(BF16)
    p2 = (r1 - p1.astype(F32)).astype(BF16)
    parts = jnp.dot(tri, jnp.concatenate([p0, p1, p2], axis=1), preferred_element_type=F32)
    d.acum = parts[:, :LANES] + (parts[:, LANES:2 * LANES] + parts[:, 2 * LANES:])
    lane = lax.broadcasted_iota(jnp.int32, (sb, 128), 1)
    d.lo = lane < HEAD_DIM
    lo_h = lax.broadcasted_iota(jnp.int32, (q, 128), 1) < HEAD_DIM
    row_l = lax.broadcasted_iota(jnp.int32, (sb, 128), 0)
    s_l = jnp.where(d.lo, lane, lane - HEAD_DIM)
    d.mask = (s_l >= row_l) if reverse else (s_l <= row_l)
    d.order = (1, 0) if reverse else (0, 1)

    def halves(m):
        m_t = m.T
        m_sw = pltpu.roll(m_t, sb, axis=1)
        return (jnp.where(lo_h, m_t, m_sw), jnp.where(lo_h, m_sw, m_t))

    d.rt = halves(d.acum)
    d.rt_dt = halves(d.dt)
    return d


def _ssd_block_rows(d, step):
    sb = SSD_SUB
    blk = d.order[step]
    end_r = blk * sb if d.reverse else blk * sb + sb - 1
    end_row = d.acum[end_r:end_r + 1, :]
    if step == 0:
        base_row = jnp.zeros((1, 128), F32)
    else:
        pr = d.order[0] * sb if d.reverse else d.order[0] * sb + sb - 1
        base_row = d.acum[pr:pr + 1, :]
    d.rows8 = jnp.concatenate([base_row, jnp.exp2(end_row - base_row), jnp.zeros((6, 128), F32)], axis=0)
    local = slice(blk * sb, (blk + 1) * sb)
    d.rs = slice(d.row0 + blk * sb, d.row0 + (blk + 1) * sb)
    d.blk = blk
    d.acum_b = d.acum[local]
    d.dw_b = d.dt[local] * jnp.exp2(end_row - d.acum_b)


def _ssd_group_begin(d, refs, g):
    xs_ref, bm_ref, cm_ref, y_ref, st_ref = refs
    gs = slice(g * D_STATE, (g + 1) * D_STATE)
    bg = bm_ref[0, d.rs, gs]
    cg = cm_ref[0, d.rs, gs]
    bg_t = bg.astype(F32).T.astype(BF16)
    h_t = st_ref[g]
    d.grp = dict(bg_t=bg_t, h_t=h_t, xw=[], cd=[])
    if y_ref is not None:
        d.grp["cb2"] = jnp.dot(cg, jnp.concatenate([bg_t, bg_t], axis=1), preferred_element_type=F32)
        d.grp["yoff"] = jnp.dot(cg, h_t.astype(BF16), preferred_element_type=F32)


def _ssd_pair(d, refs, g, pr_i):
    xs_ref, bm_ref, cm_ref, y_ref, st_ref = refs
    lo = d.lo
    lo_row = lo[0:1]
    h0 = g * HEADS_PER_GROUP + 2 * pr_i
    h1 = h0 + 1
    ps = slice((g * 3 + pr_i) * 128, (g * 3 + pr_i + 1) * 128)
    pat = jnp.where(lo, h0, h1)
    dwp = jnp.take_along_axis(d.dw_b, pat, axis=1)
    r8 = jnp.take_along_axis(d.rows8, pat[0:8], axis=1)
    base_p, cd_p = r8[0:1], r8[1:2]
    x = xs_ref[0, d.rs, ps]
    gd = d.grp
    gd["xw"].append((x.astype(F32) * dwp).astype(BF16))
    gd["cd"].append(cd_p)
    if y_ref is None:
        return
    colp = jnp.take_along_axis(d.acum_b, pat, axis=1)
    rt, rt_dt = d.rt[d.blk], d.rt_dt[d.blk]
    rowp = jnp.where(lo_row, rt[h0:h0 + 1, :], rt[h1:h1 + 1, :])
    dt_row = jnp.where(lo_row, rt_dt[h0:h0 + 1, :], rt_dt[h1:h1 + 1, :])
    m = (jnp.exp2(jnp.where(d.mask, colp - rowp, -1e30)) * (gd["cb2"] * dt_row)).astype(BF16)
    zero = jnp.zeros_like(x)
    rhs = jnp.concatenate([jnp.where(lo, x, zero), jnp.where(lo, zero, x)], axis=0)
    ydiag = jnp.dot(m, rhs, preferred_element_type=F32)
    y = ydiag + gd["yoff"][:, pr_i * 128:(pr_i + 1) * 128] * jnp.exp2(colp - base_p)
    y_ref[0, d.rs, ps] = y.astype(BF16)


def _ssd_group_end(d, refs, g):
    st_ref = refs[4]
    gd = d.grp
    xw = jnp.concatenate(gd["xw"], axis=1)
    cd = jnp.concatenate(gd["cd"], axis=1)
    st_ref[g] = gd["h_t"] * cd + jnp.dot(gd["bg_t"], xw, preferred_element_type=F32)


def _ssd_kernel(xf, bf, cf, df, af, hf, xb, bb, cb_, db, ab, hb, *out_and_scratch, nchunks, emit_y):
    if emit_y:
        yf, hff, yb, hfb, stf, stb = out_and_scratch
    else:
        (hff, hfb, stf, stb), yf, yb = out_and_scratch, None, None
    c = pl.program_id(1)

    @pl.when(c == 0)
    def _():
        stf[...] = hf[0]
        stb[...] = hb[0]

    refs = ((xf, bf, cf, yf, stf), (xb, bb, cb_, yb, stb))
    per_step = xf.shape[1] // CHUNK
    for ci in range(per_step):
        dirs = (_ssd_prepare(df, af, False, ci * CHUNK), _ssd_prepare(db, ab, True, (per_step - 1 - ci) * CHUNK))
        for step in range(CHUNK // SSD_SUB):
            for d in dirs:
                _ssd_block_rows(d, step)
            for g in range(N_GROUPS):
                for d, r in zip(dirs, refs):
                    _ssd_group_begin(d, r, g)
                for pr_i in range(HEADS_PER_GROUP // 2):
                    for d, r in zip(dirs, refs):
                        _ssd_pair(d, r, g, pr_i)
                for d, r in zip(dirs, refs):
                    _ssd_group_end(d, r, g)

    @pl.when(c == nchunks - 1)
    def _():
        hff[0] = stf[...]
        hfb[0] = stb[...]


def _ssd_scan(xs, bm, cm, dt, alog_pad, h0f, h0b, emit_y=True):
    bn, length, _ = xs.shape
    rows = CHUNK * min(SSD_CHUNKS_PER_STEP, length // CHUNK)
    nchunks = length // rows
    st_shape = (N_GROUPS, D_STATE, GROUP_COLS)
    st_spec = pl.BlockSpec((1,) + st_shape, lambda b, c: (b, 0, 0, 0))

    def specs(reverse):
        cc = (lambda c: nchunks - 1 - c) if reverse else (lambda c: c)
        di = 1 if reverse else 0
        tok = lambda width: pl.BlockSpec((1, rows, width), lambda b, c: (b, cc(c), 0))
        ins = [tok(D_INNER), tok(BC_WIDTH), tok(BC_WIDTH),
               pl.BlockSpec((1, rows, 128), lambda b, c: (b, cc(c), di)),
               pl.BlockSpec((1, 1, 128), lambda b, c: (di, 0, 0)), st_spec]
        return ins, ([tok(D_INNER)] if emit_y else []) + [st_spec]

    in_f, out_f = specs(False)
    in_b, out_b = specs(True)
    y_shape = [jax.ShapeDtypeStruct((bn, length, D_INNER), BF16)] if emit_y else []
    h_shape = [jax.ShapeDtypeStruct((bn,) + st_shape, F32)]
    return pl.pallas_call(
        functools.partial(_ssd_kernel, nchunks=nchunks, emit_y=emit_y),
        out_shape=y_shape + h_shape + y_shape + h_shape,
        grid=(bn, nchunks),
        in_specs=in_f + in_b,
        out_specs=out_f + out_b,
        scratch_shapes=[pltpu.VMEM(st_shape, F32), pltpu.VMEM(st_shape, F32)],
        compiler_params=_cparams(2),
        name="ssd_scan",
    )(xs, bm, cm, dt, alog_pad, h0f, xs, bm, cm, dt, alog_pad, h0b)


MIX_SUB = 128
MIX_TILE = 512


def _split_bf16(v):
    hi = v.astype(BF16)
    return hi, (v - hi.astype(F32)).astype(BF16)


def _mix_kernel(yf_ref, yb_ref, xs_ref, z_ref, f_ref, g_ref, x_ref, g1_ref, sh2_ref, sc2_ref,
                dsk_ref, nssd_ref, npost_ref, npre_ref, wso_ref, wf_ref, wo_ref, wr_ref,
                hx_ref, hm_ref, aff_ref):
    dot = functools.partial(jnp.dot, preferred_element_type=F32)
    wr_hi, wr_lo = _split_bf16(wr_ref[...])
    subs = [slice(r0, r0 + MIX_SUB) for r0 in range(0, x_ref.shape[1], MIX_SUB)]
    ys = []
    for rs in subs:
        y = (yf_ref[0, rs].astype(F32) + yb_ref[0, rs].astype(F32)
             + dsk_ref[...] * xs_ref[0, rs].astype(F32))
        ys.append(_rms(y * _silu(z_ref[0, rs].astype(F32)), nssd_ref[...]).astype(BF16))
    s_branches = [dot(y, wso_ref[...]) for y in ys]
    f_branches = [dot(f_ref[0, rs].astype(BF16), wf_ref[...]) for rs in subs]
    merged = []
    for rs, s_branch, f_branch in zip(subs, s_branches, f_branches):
        gates = g_ref[0, rs].astype(F32)
        merged.append((gates[:, :D_MODEL] * f_branch + gates[:, D_MODEL:] * s_branch).astype(BF16))
    mixes = [dot(m, wo_ref[...]) for m in merged]
    hms = []
    for rs, mix in zip(subs, mixes):
        hx = x_ref[0, rs] + g1_ref[0] * _rms(mix, npost_ref[...])
        hx_ref[0, rs] = hx
        hm = _rms(hx, npre_ref[...]) * (1.0 + sc2_ref[0]) + sh2_ref[0]
        hm_ref[0, rs] = _to_row_tiles(hm)
        hms.append(hm)
    for rs, hm in zip(subs, hms):
        hm_hi, hm_lo = _split_bf16(hm)
        logits = dot(hm_hi, wr_hi) + (dot(hm_hi, wr_lo) + dot(hm_lo, wr_hi))
        real = lax.broadcasted_iota(jnp.int32, logits.shape, 1) < N_EXPERTS
        logits = jnp.where(real, logits, -1e30)
        e = jnp.exp(logits - jnp.max(logits, axis=-1, keepdims=True))
        aff = e / jnp.sum(e, axis=-1, keepdims=True)
        aff_ref[0, :, rs] = aff.T[:N_EXPERTS]


def _mix(yf, yb, xs, z, f, gates, x, g1, sh2, sc2, consts, tm):
    bn, length, d = x.shape
    tok = lambda width: pl.BlockSpec((1, tm, width), lambda b, i: (b, i, 0))
    vec = pl.BlockSpec((1, 1, d), lambda b, i: (b, 0, 0))
    return pl.pallas_call(
        _mix_kernel,
        out_shape=[jax.ShapeDtypeStruct((bn, length, d), F32),
                   jax.ShapeDtypeStruct((bn, length) + ROW_TILE, F32),
                   jax.ShapeDtypeStruct((bn, N_EXPERTS, length), F32)],
        grid=(bn, length // tm),
        in_specs=[tok(D_INNER), tok(D_INNER), tok(D_INNER), tok(D_INNER), tok(F_WIDTH), tok(2 * D_MODEL), tok(d),
                  vec, vec, vec] + [_const_spec(c.shape) for c in consts],
        out_specs=[tok(d), pl.BlockSpec((1, tm) + ROW_TILE, lambda b, i: (b, i, 0, 0)),
                   pl.BlockSpec((1, N_EXPERTS, tm), lambda b, i: (b, 0, i))],
        compiler_params=_cparams(2, VMEM_LIMIT),
        name="merge_out_router",
    )(yf, yb, xs, z, f, gates, x, g1, sh2, sc2, *consts)


def _to_tile_rows(v):
    return jnp.concatenate([v[:, k * LANES:(k + 1) * LANES] for k in range(v.shape[1] // LANES)], axis=0)


def _route_kernel(aff_ref, idx_ref, val_ref, *, cap):
    I32 = jnp.int32
    a = aff_ref[0]
    ne, length = a.shape
    nt = length // LANES
    rows = nt * ne

    def search(i, t):
        cand = t | lax.shift_left(jnp.int32(1), 30 - i)
        cnt = jnp.sum((a >= pltpu.bitcast(cand, F32)).astype(F32), axis=1, keepdims=True)
        return jnp.where(cnt >= cap, cand, t)

    thr = pltpu.bitcast(lax.fori_loop(0, 31, search, jnp.zeros((ne, 1), I32)), F32)
    n_gt = jnp.sum((a > thr).astype(F32), axis=1, keepdims=True)
    need = cap - n_gt

    val_r = _to_tile_rows(a)
    tile_rows = lambda v: jnp.concatenate([v] * nt, axis=0)
    thr_r = tile_rows(thr)
    need_r = tile_rows(need)
    li = lax.broadcasted_iota(I32, (LANES, LANES), 0)
    lj = lax.broadcasted_iota(I32, (LANES, LANES), 1)
    upper = (li < lj).astype(BF16)
    ri = lax.broadcasted_iota(I32, (rows, rows), 0)
    rj = lax.broadcasted_iota(I32, (rows, rows), 1)
    earlier = (((ri & (ne - 1)) == (rj & (ne - 1))) & (rj < ri)).astype(BF16)

    def prefix(mask):
        local = jnp.dot(mask.astype(BF16), upper, preferred_element_type=F32)
        total = jnp.sum(mask.astype(F32), axis=1, keepdims=True)
        tot_b = jnp.broadcast_to(total, (rows, LANES)).astype(BF16)
        offs = jnp.dot(earlier, tot_b, preferred_element_type=F32)[:, 0:1]
        return local, offs

    eq = val_r == thr_r
    eq_local, eq_off = prefix(eq)
    sel = (val_r > thr_r) | (eq & (eq_local + eq_off < need_r))
    sel_local, sel_off = prefix(sel)

    lane = lax.broadcasted_iota(I32, (rows, LANES), 1)
    k_row = lax.shift_right_logical(lax.broadcasted_iota(I32, (rows, LANES), 0), int(np.log2(ne)))
    tok = jnp.where(sel, k_row * LANES + lane, -1)
    val = val_r
    dist = jnp.where(sel, lane - sel_local.astype(I32), 0)
    for s in range(int(np.log2(LANES))):
        sh = 1 << s
        mv = (tok >= 0) & ((lax.shift_right_logical(dist, s) & 1) == 1)
        in_tok = pltpu.roll(jnp.where(mv, tok, -1), LANES - sh, axis=1)
        in_val = pltpu.roll(val, LANES - sh, axis=1)
        in_dist = pltpu.roll(dist, LANES - sh, axis=1)
        arrive = in_tok >= 0
        stay = (tok >= 0) & jnp.logical_not(mv)
        tok = jnp.where(arrive, in_tok, jnp.where(stay, tok, -1))
        val = jnp.where(arrive, in_val, val)
        dist = jnp.where(arrive, in_dist, dist)

    off = sel_off.astype(I32)
    lane_e = lax.broadcasted_iota(I32, (ne, LANES), 1)
    ncol = cap // LANES
    out_tok = [jnp.zeros((ne, LANES), I32) for _ in range(ncol)]
    out_val = [jnp.zeros((ne, LANES), F32) for _ in range(ncol)]
    for k in range(nt):
        rs = slice(k * ne, (k + 1) * ne)
        o = off[rs]
        sh, col = o & (LANES - 1), lax.shift_right_logical(o, int(np.log2(LANES)))
        pat = (lane_e - sh) & (LANES - 1)
        r_tok = jnp.take_along_axis(tok[rs], pat, axis=1)
        r_val = jnp.take_along_axis(val[rs], pat, axis=1)
        ok = r_tok >= 0
        for c in range(ncol):
            here = ok & (((col == c) & (lane_e >= sh)) | ((col + 1 == c) & (lane_e < sh)))
            out_tok[c] = jnp.where(here, r_tok, out_tok[c])
            out_val[c] = jnp.where(here, r_val, out_val[c])
    idx_ref[0] = jnp.concatenate(out_tok, axis=1)
    val_ref[0] = jnp.concatenate(out_val, axis=1)


def _route(aff_t, cap):
    bn, ne, length = aff_t.shape
    assert ne & (ne - 1) == 0 and cap % LANES == 0 and length % LANES == 0
    return pl.pallas_call(
        functools.partial(_route_kernel, cap=cap),
        out_shape=[jax.ShapeDtypeStruct((bn, ne, cap), jnp.int32), jax.ShapeDtypeStruct((bn, ne, cap), F32)],
        grid=(bn,),
        in_specs=[pl.BlockSpec((1, ne, length), lambda b: (b, 0, 0))],
        out_specs=[pl.BlockSpec((1, ne, cap), lambda b: (b, 0, 0))] * 2,
        compiler_params=_cparams(1),
        name="expert_choice_route",
    )(aff_t)


def _expert_kernel(idx_ref, idxn_ref, hm_ref, wg_ref, wu_ref, wd_ref, ye_ref, buf0, buf1, wgb, wub, wdb, sem,
                   *, cap, nb, nsteps):
    s = pl.program_id(0)
    bufs = (buf0, buf1)
    dot = functools.partial(jnp.dot, preferred_element_type=F32)

    def row_copy(k, j, b, r):
        return pltpu.make_async_copy(hm_ref.at[b, pl.ds(r, 1)], bufs[k].at[pl.ds(j, 1)], sem.at[k])

    def wait_rows(k):
        pltpu.make_async_copy(hm_ref.at[0, pl.ds(0, cap)], bufs[k], sem.at[k]).wait()

    b0 = (2 * s) % nb

    @pl.when(b0 == 0)
    def _():
        wgb[...] = wg_ref[0].astype(BF16)
        wub[...] = wu_ref[0].astype(BF16)
        wdb[...] = wd_ref[0].astype(BF16)

    @pl.when(s == 0)
    def _():
        def issue(j, carry):
            row_copy(0, j, b0, idx_ref[0, 0, j]).start()
            return carry
        lax.fori_loop(0, cap, issue, 0)

    for k in range(2):
        wait_rows(k)
        x = _from_row_tiles(bufs[k][...]).astype(BF16)
        if k == 0:
            for j in range(cap):
                row_copy(1, j, b0 + 1, idx_ref[1, 0, j]).start()
            for j in range(cap):
                row_copy(0, j, (b0 + 2) % nb, idxn_ref[0, 0, j]).start()
        hid = (_silu(dot(x, wgb[...])) * dot(x, wub[...])).astype(BF16)
        ye_ref[k] = _to_row_tiles(dot(hid, wdb[...]))

    @pl.when(s == nsteps - 1)
    def _():
        wait_rows(0)


def _experts(idx, hm, wg, wu, wd):
    bn = hm.shape[0]
    ne, d, ff = wg.shape
    cap = idx.shape[-1]
    per = bn // 2
    nsteps = ne * per
    cur = lambda s: (s, 0, 0)
    nxt = lambda s: ((s + 1) % nsteps, 0, 0)
    wspec = pl.BlockSpec((1, d, ff), lambda s: (s // per, 0, 0))
    return pl.pallas_call(
        functools.partial(_expert_kernel, cap=cap, nb=bn, nsteps=nsteps),
        out_shape=jax.ShapeDtypeStruct((ne * bn, cap) + ROW_TILE, F32),
        grid=(nsteps,),
        in_specs=[pl.BlockSpec((2, 1, cap), cur, memory_space=pltpu.SMEM),
                  pl.BlockSpec((2, 1, cap), nxt, memory_space=pltpu.SMEM),
                  pl.BlockSpec(memory_space=pl.ANY),
                  wspec, wspec, pl.BlockSpec((1, ff, d), lambda s: (s // per, 0, 0))],
        out_specs=pl.BlockSpec((2, cap) + ROW_TILE, lambda s: (s, 0, 0, 0)),
        scratch_shapes=[pltpu.VMEM((cap,) + ROW_TILE, F32), pltpu.VMEM((cap,) + ROW_TILE, F32),
                        pltpu.VMEM((d, ff), BF16), pltpu.VMEM((d, ff), BF16), pltpu.VMEM((ff, d), BF16),
                        pltpu.SemaphoreType.DMA((2,))],
        compiler_params=_cparams(1, VMEM_LIMIT),
        name="expert_ffn",
    )(idx, idx, hm, wg, wu, wd)


COMBINE_ROWS = 16


COMBINE_EXPERTS = 2


def _combine_kernel(*refs, cap, nb, rows_out):
    n = COMBINE_EXPERTS
    idx_refs, val_refs, ye_refs = refs[:n], refs[n:2 * n], refs[2 * n:3 * n]
    hx_ref, g2_ref, w_ref, o_ref, acc = refs[3 * n:]
    b = pl.program_id(0)
    e2 = pl.program_id(1)
    slot = b % 2
    acc_cur = acc.at[slot]

    @pl.when(jnp.logical_and(e2 == 0, b < nb))
    def _():
        acc_cur[...] = jnp.zeros(acc.shape[1:], F32)

    @pl.when(b < nb)
    def _():
        for idx_ref, val_ref, ye_ref in zip(idx_refs, val_refs, ye_refs):
            def body(i, carry, idx_ref=idx_ref, val_ref=val_ref, ye_ref=ye_ref):
                j0 = i * COMBINE_ROWS
                rows = [idx_ref[0, 0, j0 + u] for u in range(COMBINE_ROWS)]
                new = [acc_cur[rows[u]] + ye_ref[0, j0 + u] * val_ref[0, 0, j0 + u]
                       for u in range(COMBINE_ROWS)]
                for u in range(COMBINE_ROWS):
                    acc_cur[rows[u]] = new[u]
                return carry

            lax.fori_loop(0, cap // COMBINE_ROWS, body, 0)

    @pl.when(b > 0)
    def _():
        f3 = acc[1 - slot, pl.ds(pl.multiple_of(e2 * rows_out, rows_out), rows_out)]
        ms = jnp.mean(f3 * f3, axis=(1, 2), keepdims=True)
        y = _from_row_tiles(f3 * lax.rsqrt(ms + RMS_EPS) * w_ref[...])
        o_ref[0] = hx_ref[0] + g2_ref[0] * y


def _combine_residual(idx, vals, ye, hx, g2, w):
    bn, length, d = hx.shape
    n = COMBINE_EXPERTS
    steps = N_EXPERTS // n
    cap = idx.shape[-1]
    rows_out = length // steps
    w3 = w.reshape((1,) + ROW_TILE)
    cur = lambda b: jnp.minimum(b, bn - 1)
    prev = lambda b: jnp.maximum(b - 1, 0)
    item = lambda k: (lambda b, e2: ((e2 * n + k) * bn + cur(b), 0, 0))
    item4 = lambda k: (lambda b, e2: ((e2 * n + k) * bn + cur(b), 0, 0, 0))
    smem = [pl.BlockSpec((1, 1, cap), item(k), memory_space=pltpu.SMEM) for k in range(n)]
    tok_prev = pl.BlockSpec((1, rows_out, d), lambda b, e2: (prev(b), jnp.where(b > 0, e2, 0), 0))
    return pl.pallas_call(
        functools.partial(_combine_kernel, cap=cap, nb=bn, rows_out=rows_out),
        out_shape=jax.ShapeDtypeStruct((bn, length, d), F32),
        grid=(bn + 1, steps),
        in_specs=smem + smem + [pl.BlockSpec((1, cap) + ROW_TILE, item4(k)) for k in range(n)]
                 + [tok_prev, pl.BlockSpec((1, 1, d), lambda b, e2: (prev(b), 0, 0)), _const_spec(w3.shape)],
        out_specs=tok_prev,
        scratch_shapes=[pltpu.VMEM((2, length) + ROW_TILE, F32)],
        compiler_params=_cparams(2, VMEM_LIMIT),
        name="expert_combine_residual",
    )(*([idx] * n + [vals] * n + [ye] * n), hx, g2, w3)


def kernel(x, c, ctx, c_ctx, w_mod, b_mod, norm_mix_pre, norm_mix_post, norm_ffn_pre, norm_ffn_post,
           w_in, conv_w, conv_b, dt_bias, a_log, d_skip, ssd_norm, w_fourier, w_ssd_out, b_gate, w_out,
           w_router, w_e_gate, w_e_up, w_e_down):
    bn, length, d = x.shape
    ctx_len = ctx.shape[1]
    l = 0
    row = lambda v: v.reshape(1, -1)

    cs = jnp.concatenate([c, c_ctx[None], jnp.zeros((7, d), F32)], axis=0)
    mods = _modulation(cs, w_mod[l], b_mod[l])
    mod_x = mods[:bn].reshape(bn, N_MOD, 1, d)
    mod_c = jnp.broadcast_to(mods[bn].reshape(1, N_MOD, 1, d), (bn, N_MOD, 1, d))
    sh1, sc1, g1, sh2, sc2, g2 = [mod_x[:, i] for i in range(N_MOD)]

    i1 = F_WIDTH
    i2 = i1 + D_INNER
    i3 = i2 + D_INNER
    i4 = i3 + BC_WIDTH
    i5 = i4 + BC_WIDTH
    i6 = i5 + 2 * N_HEADS
    wi = w_in[l]
    wdt = jnp.zeros((d, 256), F32).at[:, :N_HEADS].set(wi[:, i5:i5 + N_HEADS])
    wdt = wdt.at[:, 128:128 + N_HEADS].set(wi[:, i5 + N_HEADS:i6])
    dtb = jnp.zeros((1, 256), F32).at[0, :N_HEADS].set(dt_bias[l, 0]).at[0, 128:128 + N_HEADS].set(dt_bias[l, 1])
    dft_c, w1, tw_c, tw_s, w2 = _dft_constants()
    cw, cb = conv_w[l], conv_b[l]
    j1, j2 = D_INNER, D_INNER + BC_WIDTH
    w = dict(
        wuf=wi[:, :i1].astype(BF16), wz=wi[:, i1:i2].astype(BF16), wxs=wi[:, i2:i3].astype(BF16),
        wb=wi[:, i3:i4].astype(BF16), wc=wi[:, i4:i5].astype(BF16), wdt=wdt.astype(BF16),
        wg=wi[:, i6:].astype(BF16), dft=dft_c,
        cwx=cw[:, :j1], cbx=row(cb[:j1]), cwb=cw[:, j1:j2], cbb=row(cb[j1:j2]),
        cwc=cw[:, j2:], cbc=row(cb[j2:]), dtb=dtb, bg=row(b_gate[l]))
    gain_pre = row(norm_mix_pre[l])

    alog_pad = jnp.zeros((2, 1, 128), F32).at[:, 0, :N_HEADS].set(a_log[l])

    cxs, cbm, ccm, cdt = _in_projection(ctx, mod_c[:, 0], mod_c[:, 1], gain_pre, w, ctx_len, ctx_len, False)
    h_zero = jnp.zeros((bn, N_GROUPS, D_STATE, GROUP_COLS), F32)
    hc_f, hc_b = _ssd_scan(cxs, cbm, ccm, cdt, alog_pad, h_zero, h_zero, emit_y=False)

    zr, zi, z, xs, bm, cm, dt, gates = _in_projection(x, sh1, sc1, gain_pre, w, GRID_W, 2 * INPROJ_SUB, True)
    f = _sequence_dft_real(zr, zi, w1, tw_c, tw_s, w2)
    yf, _, yb, _ = _ssd_scan(xs, bm, cm, dt, alog_pad, hc_f, hc_b)

    consts = [row(jnp.repeat(d_skip[l], HEAD_DIM)), row(ssd_norm[l]), row(norm_mix_post[l]),
              row(norm_ffn_pre[l]), w_ssd_out[l].astype(BF16), w_fourier[l].astype(BF16),
              w_out[l].astype(BF16), jnp.pad(w_router[l], ((0, 0), (0, LANES - N_EXPERTS)))]
    hx, hm, aff_t = _mix(yf, yb, xs, z, f, gates, x, g1, sh2, sc2, consts, MIX_TILE)

    cap = EC_FACTOR * length // N_EXPERTS
    top_idx, top_aff = _route(aff_t, cap)
    idx = jnp.swapaxes(top_idx, 0, 1).reshape(N_EXPERTS * bn, 1, cap)
    vals = jnp.swapaxes(top_aff, 0, 1).reshape(N_EXPERTS * bn, 1, cap)
    ye = _experts(idx, hm, w_e_gate[l], w_e_up[l], w_e_down[l])
    return _combine_residual(idx, vals, ye, hx, g2, row(norm_ffn_post[l]))
```

```python
import functools

import numpy as np
import jax
import jax.numpy as jnp
from jax import lax
from jax.experimental import pallas as pl
from jax.experimental.pallas import tpu as pltpu

F32 = jnp.float32
BF16 = jnp.bfloat16
HIGHEST = lax.Precision.HIGHEST

D_MODEL = 1024
GRID_W = 64
F_GROUP_W = 128
F_WIDTH = 512
D_INNER = 1536
HEAD_DIM = 64
N_HEADS = 24
N_GROUPS = 4
HEADS_PER_GROUP = 6
D_STATE = 128
CHUNK = 128
BC_WIDTH = N_GROUPS * D_STATE
N_EXPERTS = 16
EC_FACTOR = 2
N_MOD = 6
RMS_EPS = 1e-6
GROUP_COLS = HEADS_PER_GROUP * HEAD_DIM

FFT_N1 = 128
FFT_N2 = 32
FFT_KB = 8

VMEM_LIMIT = 56 * 1024 * 1024
LANES = 128


def _cparams(n_axes, vmem=None):
    return pltpu.CompilerParams(dimension_semantics=("arbitrary",) * n_axes,
                                vmem_limit_bytes=vmem)


def _const_spec(shape):
    nd = len(shape)
    return pl.BlockSpec(shape, lambda *_: (0,) * nd, pipeline_mode=pl.Buffered(1))


def _rms(x, w):
    ms = jnp.mean(x * x, axis=-1, keepdims=True)
    return x * lax.rsqrt(ms + RMS_EPS) * w


def _sigmoid(x):
    return 1.0 / (1.0 + jnp.exp(-x))


def _silu(x):
    return x * _sigmoid(x)


ROW_TILE = (8, 128)


def _to_row_tiles(v):
    parts = [v[:, c * 128:(c + 1) * 128] for c in range(ROW_TILE[0])]
    return jnp.swapaxes(jnp.stack(parts, axis=0), 0, 1)


def _from_row_tiles(v3):
    t = jnp.swapaxes(v3, 0, 1)
    return jnp.concatenate([t[c] for c in range(ROW_TILE[0])], axis=1)


def _mod_kernel(c_ref, w_ref, b_ref, o_ref):
    c = c_ref[...]
    o_ref[...] = jnp.dot(_silu(c), w_ref[...], precision=HIGHEST,
                         preferred_element_type=F32) + b_ref[...]


def _modulation(cs, w_mod, b_mod):
    rows = cs.shape[0]
    n = w_mod.shape[1]
    blk = D_MODEL
    return pl.pallas_call(
        _mod_kernel,
        out_shape=jax.ShapeDtypeStruct((rows, n), F32),
        grid=(n // blk,),
        in_specs=[pl.BlockSpec((rows, D_MODEL), lambda j: (0, 0)),
                  pl.BlockSpec((D_MODEL, blk), lambda j: (0, j)),
                  pl.BlockSpec((1, blk), lambda j: (0, j))],
        out_specs=pl.BlockSpec((rows, blk), lambda j: (0, j)),
        compiler_params=_cparams(1),
        name="modulation",
    )(cs, w_mod, b_mod.reshape(1, n))


def _conv_silu(xbc, cw_ref, cb_ref, row_len):
    tm = xbc.shape[0]
    pos = lax.broadcasted_iota(jnp.int32, (tm, 1), 0) % row_len
    prev = jnp.where(pos == 0, 0.0, pltpu.roll(xbc, 1, axis=0))
    nxt = jnp.where(pos == row_len - 1, 0.0, pltpu.roll(xbc, tm - 1, axis=0))
    out = prev * cw_ref[0:1, :] + xbc * cw_ref[1:2, :] + nxt * cw_ref[2:3, :] + cb_ref[...]
    return _silu(out)


INPROJ_SUB = 128
INPROJ_TILE = 512


def _softplus(x):
    return jnp.maximum(x, 0.0) + jnp.log(1.0 + jnp.exp(-jnp.abs(x)))


def _inproj_kernel(x_ref, sh_ref, sc_ref, gain_ref, wuf_ref, wz_ref, wxs_ref, wb_ref, wc_ref,
                   wdt_ref, wg_ref, dft_ref, cwx_ref, cbx_ref, cwb_ref, cbb_ref, cwc_ref, cbc_ref,
                   dtb_ref, bg_ref,
                   zr_ref, zi_ref, z_ref, xs_ref, bm_ref, cm_ref, dt_ref, g_ref, *, row_len):
    dot = functools.partial(jnp.dot, preferred_element_type=F32)
    tm = x_ref.shape[1]
    sub = min(tm, max(INPROJ_SUB, row_len))
    assert sub % row_len == 0 and tm % sub == 0
    for r0 in range(0, tm, sub):
        rs = slice(r0, r0 + sub)
        h = _rms(x_ref[0, rs], gain_ref[...]) * (1.0 + sc_ref[0]) + sh_ref[0]
        hb = h.astype(BF16)
        xs_ref[0, rs] = _conv_silu(dot(hb, wxs_ref[...]), cwx_ref, cbx_ref, row_len).astype(BF16)
        bm_ref[0, rs] = _conv_silu(dot(hb, wb_ref[...]), cwb_ref, cbb_ref, row_len).astype(BF16)
        cm_ref[0, rs] = _conv_silu(dot(hb, wc_ref[...]), cwc_ref, cbc_ref, row_len).astype(BF16)
        dt_ref[0, rs] = _softplus(dot(hb, wdt_ref[...]) + dtb_ref[...])
        if zr_ref is None:
            continue
        uf = dot(hb, wuf_ref[...]).astype(BF16)
        for g in range(F_WIDTH // F_GROUP_W):
            sl = slice(g * F_GROUP_W, (g + 1) * F_GROUP_W)
            zz = dot(uf[:, sl], dft_ref[...].astype(BF16))
            zr_ref[0, rs, sl] = zz[:, :F_GROUP_W].astype(BF16)
            zi_ref[0, rs, sl] = zz[:, F_GROUP_W:].astype(BF16)
        z_ref[0, rs] = dot(hb, wz_ref[...]).astype(BF16)
        g_ref[0, rs] = _sigmoid(dot(hb, wg_ref[...]) + bg_ref[...]).astype(BF16)


def _inproj_ctx_kernel(x_ref, sh_ref, sc_ref, gain_ref, wxs_ref, wb_ref, wc_ref, wdt_ref,
                       cwx_ref, cbx_ref, cwb_ref, cbb_ref, cwc_ref, cbc_ref, dtb_ref,
                       xs_ref, bm_ref, cm_ref, dt_ref, *, row_len):
    _inproj_kernel(x_ref, sh_ref, sc_ref, gain_ref, None, None, wxs_ref, wb_ref, wc_ref,
                   wdt_ref, None, None, cwx_ref, cbx_ref, cwb_ref, cbb_ref, cwc_ref, cbc_ref,
                   dtb_ref, None, None, None, None, xs_ref, bm_ref, cm_ref, dt_ref, None,
                   row_len=row_len)


def _in_projection(x, shift, scale, gain, w, row_len, tm, full):
    bn, length, d = x.shape
    grid = (bn, length // tm)
    tok = lambda width: pl.BlockSpec((1, tm, width), lambda b, i: (b, i, 0))
    vec = pl.BlockSpec((1, 1, d), lambda b, i: (b, 0, 0))
    out = lambda width, dt: jax.ShapeDtypeStruct((bn, length, width), dt)
    conv_ops = [w["cwx"], w["cbx"], w["cwb"], w["cbb"], w["cwc"], w["cbc"]]
    ssd_shapes = [out(D_INNER, BF16), out(BC_WIDTH, BF16), out(BC_WIDTH, BF16), out(256, F32)]
    ssd_specs = [tok(D_INNER), tok(BC_WIDTH), tok(BC_WIDTH), tok(256)]
    if full:
        ops = [x, shift, scale, gain, w["wuf"], w["wz"], w["wxs"], w["wb"], w["wc"], w["wdt"], w["wg"],
               w["dft"]] + conv_ops + [w["dtb"], w["bg"]]
        kern = functools.partial(_inproj_kernel, row_len=row_len)
        out_shape = [out(F_WIDTH, BF16), out(F_WIDTH, BF16), out(D_INNER, BF16)] + ssd_shapes + \
                    [out(2 * D_MODEL, BF16)]
        out_specs = [tok(F_WIDTH), tok(F_WIDTH), tok(D_INNER)] + ssd_specs + [tok(2 * D_MODEL)]
        name = "in_projection"
    else:
        ops = [x, shift, scale, gain, w["wxs"], w["wb"], w["wc"], w["wdt"]] + conv_ops + [w["dtb"]]
        kern = functools.partial(_inproj_ctx_kernel, row_len=row_len)
        out_shape = ssd_shapes
        out_specs = ssd_specs
        name = "in_projection_ctx"
    in_specs = [tok(d), vec, vec] + [_const_spec(o.shape) for o in ops[3:]]
    return pl.pallas_call(kern, out_shape=out_shape, grid=grid, in_specs=in_specs,
                          out_specs=out_specs, compiler_params=_cparams(2, VMEM_LIMIT),
                          name=name)(*ops)


def _dft_constants():
    j = np.arange(F_GROUP_W)
    ang = 2.0 * np.pi * np.outer(j, j) / F_GROUP_W
    dft_c = np.concatenate([np.cos(ang), -np.sin(ang)], axis=1)
    k1 = np.arange(FFT_N1)
    a1 = 2.0 * np.pi * np.outer(k1, k1) / FFT_N1
    c1, s1 = np.cos(a1), np.sin(a1)
    w1 = np.block([[c1, s1], [-s1, c1]])
    n2 = np.arange(FFT_N2)
    at = 2.0 * np.pi * np.outer(k1, n2) / (FFT_N1 * FFT_N2)
    tw_c = np.repeat(np.cos(at), 128, axis=1)
    tw_s = np.repeat(np.sin(at), 128, axis=1)
    a2 = 2.0 * np.pi * np.outer(n2, n2) / FFT_N2
    w2 = np.zeros((FFT_N2, FFT_KB, 2, FFT_KB, FFT_N2))
    for jj in range(FFT_KB):
        w2[:, jj, 0, jj, :] = np.cos(a2)
        w2[:, jj, 1, jj, :] = np.sin(a2)
    w2 = w2.reshape(FFT_N2 * FFT_KB, 2 * FFT_KB * FFT_N2)
    return (jnp.asarray(dft_c, F32), jnp.asarray(w1, F32), jnp.asarray(tw_c, F32),
            jnp.asarray(tw_s, F32), jnp.asarray(w2, F32))


FFT_QB = 16


def _seq_dft_kernel(zr_ref, zi_ref, w1_ref, twc_ref, tws_ref, w2_ref, o_ref, qr_s, qi_s, tr_s, ti_s, *, scale):
    w1 = w1_ref[...].astype(BF16)
    reps = F_WIDTH // 128
    for qb in range(FFT_N2 // FFT_QB):
        qs = slice(qb * FFT_QB, (qb + 1) * FFT_QB)
        zr_t = jnp.swapaxes(zr_ref[0, :, qs, :], 0, 1)
        zi_t = jnp.swapaxes(zi_ref[0, :, qs, :], 0, 1)
        for q in range(FFT_QB):
            n2 = qb * FFT_QB + q
            rhs = jnp.concatenate([zr_t[q], zi_t[q]], axis=0)
            t = jnp.dot(w1, rhs, preferred_element_type=F32)
            a, b = t[:FFT_N1], t[FFT_N1:]
            cw = jnp.concatenate([twc_ref[:, n2 * 128:(n2 + 1) * 128]] * reps, axis=1)
            sw = jnp.concatenate([tws_ref[:, n2 * 128:(n2 + 1) * 128]] * reps, axis=1)
            qr_s[q] = (a * cw + b * sw).astype(BF16)
            qi_s[q] = (b * cw - a * sw).astype(BF16)
        tr_s[:, qs, :] = jnp.swapaxes(qr_s[...], 0, 1)
        ti_s[:, qs, :] = jnp.swapaxes(qi_s[...], 0, 1)
    w2 = w2_ref[...].astype(BF16)
    rows = FFT_KB * FFT_N2
    for kb in range(FFT_N1 // FFT_KB):
        ks = slice(kb * FFT_KB, (kb + 1) * FFT_KB)
        rhs = jnp.concatenate([tr_s[ks].reshape(rows, F_WIDTH), ti_s[ks].reshape(rows, F_WIDTH)], axis=0)
        y = jnp.dot(w2, rhs, preferred_element_type=F32) * scale
        o_ref[0, :, ks, :] = y.reshape(FFT_N2, FFT_KB, F_WIDTH)


def _sequence_dft_real(zr, zi, w1, tw_c, tw_s, w2):
    bn, length, width = zr.shape
    blk = pl.BlockSpec((1, FFT_N1, FFT_N2, width), lambda b: (b, 0, 0, 0))
    scale = 1.0 / float(np.sqrt(length * F_GROUP_W))
    f = pl.pallas_call(
        functools.partial(_seq_dft_kernel, scale=scale),
        out_shape=jax.ShapeDtypeStruct((bn, FFT_N2, FFT_N1, width), F32),
        grid=(bn,),
        in_specs=[blk, blk, _const_spec(w1.shape), _const_spec(tw_c.shape), _const_spec(tw_s.shape),
                  _const_spec(w2.shape)],
        out_specs=pl.BlockSpec((1, FFT_N2, FFT_N1, width), lambda b: (b, 0, 0, 0)),
        scratch_shapes=[pltpu.VMEM((FFT_QB, FFT_N1, width), BF16)] * 2
                       + [pltpu.VMEM((FFT_N1, FFT_N2, width), BF16)] * 2,
        compiler_params=_cparams(1, VMEM_LIMIT),
        name="seq_dft",
    )(zr.reshape(bn, FFT_N1, FFT_N2, width), zi.reshape(bn, FFT_N1, FFT_N2, width), w1, tw_c, tw_s, w2)
    return f.reshape(bn, length, width)


SSD_SUB = 64
SSD_CHUNKS_PER_STEP = 4
LOG2E = 1.4426950408889634


class _Dir:
    pass


def _ssd_prepare(dt_ref, alog_ref, reverse, row0):
    q, sb = CHUNK, SSD_SUB
    d = _Dir()
    d.reverse = reverse
    r_i = lax.broadcasted_iota(jnp.int32, (q, q), 0)
    c_i = lax.broadcasted_iota(jnp.int32, (q, q), 1)
    tri = ((r_i <= c_i) if reverse else (r_i >= c_i)).astype(BF16)
    a = -jnp.exp(alog_ref[0])
    d.row0 = row0
    d.dt = dt_ref[0, row0:row0 + q]
    da = d.dt * (a * LOG2E)
    p0 = da.astype(BF16)
    r1 = da - p0.astype(F32)
    p1 = r1.astype(BF16)
    p2 = (r1 - p1.astype(F32)).astype(BF16)
    parts = jnp.dot(tri, jnp.concatenate([p0, p1, p2], axis=1), preferred_element_type=F32)
    d.acum = parts[:, :LANES] + (parts[:, LANES:2 * LANES] + parts[:, 2 * LANES:])
    lane = lax.broadcasted_iota(jnp.int32, (sb, 128), 1)
    d.lo = lane < HEAD_DIM
    lo_h = lax.broadcasted_iota(jnp.int32, (q, 128), 1) < HEAD_DIM
    row_l = lax.broadcasted_iota(jnp.int32, (sb, 128), 0)
    s_l = jnp.where(d.lo, lane, lane - HEAD_DIM)
    d.mask = (s_l >= row_l) if reverse else (s_l <= row_l)
    d.order = (1, 0) if reverse else (0, 1)

    def halves(m):
        m_t = m.T
        m_sw = pltpu.roll(m_t, sb, axis=1)
        return (jnp.where(lo_h, m_t, m_sw), jnp.where(lo_h, m_sw, m_t))

    d.rt = halves(d.acum)
    d.rt_dt = halves(d.dt)
    return d


def _ssd_block_rows(d, step):
    sb = SSD_SUB
    blk = d.order[step]
    end_r = blk * sb if d.reverse else blk * sb + sb - 1
    end_row = d.acum[end_r:end_r + 1, :]
    if step == 0:
        base_row = jnp.zeros((1, 128), F32)
    else:
        pr = d.order[0] * sb if d.reverse else d.order[0] * sb + sb - 1
        base_row = d.acum[pr:pr + 1, :]
    d.rows8 = jnp.concatenate([base_row, jnp.exp2(end_row - base_row), jnp.zeros((6, 128), F32)], axis=0)
    local = slice(blk * sb, (blk + 1) * sb)
    d.rs = slice(d.row0 + blk * sb, d.row0 + (blk + 1) * sb)
    d.blk = blk
    d.acum_b = d.acum[local]
    d.dw_b = d.dt[local] * jnp.exp2(end_row - d.acum_b)


def _ssd_group_begin(d, refs, g):
    xs_ref, bm_ref, cm_ref, y_ref, st_ref = refs
    gs = slice(g * D_STATE, (g + 1) * D_STATE)
    bg = bm_ref[0, d.rs, gs]
    cg = cm_ref[0, d.rs, gs]
    bg_t = bg.astype(F32).T.astype(BF16)
    h_t = st_ref[g]
    d.grp = dict(bg_t=bg_t, h_t=h_t, xw=[], cd=[])
    if y_ref is not None:
        d.grp["cb2"] = jnp.dot(cg, jnp.concatenate([bg_t, bg_t], axis=1), preferred_element_type=F32)
        d.grp["yoff"] = jnp.dot(cg, h_t.astype(BF16), preferred_element_type=F32)


def _ssd_pair(d, refs, g, pr_i):
    xs_ref, bm_ref, cm_ref, y_ref, st_ref = refs
    lo = d.lo
    lo_row = lo[0:1]
    h0 = g * HEADS_PER_GROUP + 2 * pr_i
    h1 = h0 + 1
    ps = slice((g * 3 + pr_i) * 128, (g * 3 + pr_i + 1) * 128)
    pat = jnp.where(lo, h0, h1)
    dwp = jnp.take_along_axis(d.dw_b, pat, axis=1)
    r8 = jnp.take_along_axis(d.rows8, pat[0:8], axis=1)
    base_p, cd_p = r8[0:1], r8[1:2]
    x = xs_ref[0, d.rs, ps]
    gd = d.grp
    gd["xw"].append((x.astype(F32) * dwp).astype(BF16))
    gd["cd"].append(cd_p)
    if y_ref is None:
        return
    colp = jnp.take_along_axis(d.acum_b, pat, axis=1)
    rt, rt_dt = d.rt[d.blk], d.rt_dt[d.blk]
    rowp = jnp.where(lo_row, rt[h0:h0 + 1, :], rt[h1:h1 + 1, :])
    dt_row = jnp.where(lo_row, rt_dt[h0:h0 + 1, :], rt_dt[h1:h1 + 1, :])
    m = (jnp.exp2(jnp.where(d.mask, colp - rowp, -1e30)) * (gd["cb2"] * dt_row)).astype(BF16)
    zero = jnp.zeros_like(x)
    rhs = jnp.concatenate([jnp.where(lo, x, zero), jnp.where(lo, zero, x)], axis=0)
    ydiag = jnp.dot(m, rhs, preferred_element_type=F32)
    y = ydiag + gd["yoff"][:, pr_i * 128:(pr_i + 1) * 128] * jnp.exp2(colp - base_p)
    y_ref[0, d.rs, ps] = y.astype(BF16)


def _ssd_group_end(d, refs, g):
    st_ref = refs[4]
    gd = d.grp
    xw = jnp.concatenate(gd["xw"], axis=1)
    cd = jnp.concatenate(gd["cd"], axis=1)
    st_ref[g] = gd["h_t"] * cd + jnp.dot(gd["bg_t"], xw, preferred_element_type=F32)


def _ssd_kernel(xf, bf, cf, df, af, hf, xb, bb, cb_, db, ab, hb, *out_and_scratch, nchunks, emit_y):
    if emit_y:
        yf, hff, yb, hfb, stf, stb = out_and_scratch
    else:
        (hff, hfb, stf, stb), yf, yb = out_and_scratch, None, None
    c = pl.program_id(1)

    @pl.when(c == 0)
    def _():
        stf[...] = hf[0]
        stb[...] = hb[0]

    refs = ((xf, bf, cf, yf, stf), (xb, bb, cb_, yb, stb))
    per_step = xf.shape[1] // CHUNK
    for ci in range(per_step):
        dirs = (_ssd_prepare(df, af, False, ci * CHUNK), _ssd_prepare(db, ab, True, (per_step - 1 - ci) * CHUNK))
        for step in range(CHUNK // SSD_SUB):
            for d in dirs:
                _ssd_block_rows(d, step)
            for g in range(N_GROUPS):
                for d, r in zip(dirs, refs):
                    _ssd_group_begin(d, r, g)
                for pr_i in range(HEADS_PER_GROUP // 2):
                    for d, r in zip(dirs, refs):
                        _ssd_pair(d, r, g, pr_i)
                for d, r in zip(dirs, refs):
                    _ssd_group_end(d, r, g)

    @pl.when(c == nchunks - 1)
    def _():
        hff[0] = stf[...]
        hfb[0] = stb[...]


def _ssd_scan(xs, bm, cm, dt, alog_pad, h0f, h0b, emit_y=True):
    bn, length, _ = xs.shape
    rows = CHUNK * min(SSD_CHUNKS_PER_STEP, length // CHUNK)
    nchunks = length // rows
    st_shape = (N_GROUPS, D_STATE, GROUP_COLS)
    st_spec = pl.BlockSpec((1,) + st_shape, lambda b, c: (b, 0, 0, 0))

    def specs(reverse):
        cc = (lambda c: nchunks - 1 - c) if reverse else (lambda c: c)
        di = 1 if reverse else 0
        tok = lambda width: pl.BlockSpec((1, rows, width), lambda b, c: (b, cc(c), 0))
        ins = [tok(D_INNER), tok(BC_WIDTH), tok(BC_WIDTH),
               pl.BlockSpec((1, rows, 128), lambda b, c: (b, cc(c), di)),
               pl.BlockSpec((1, 1, 128), lambda b, c: (di, 0, 0)), st_spec]
        return ins, ([tok(D_INNER)] if emit_y else []) + [st_spec]

    in_f, out_f = specs(False)
    in_b, out_b = specs(True)
    y_shape = [jax.ShapeDtypeStruct((bn, length, D_INNER), BF16)] if emit_y else []
    h_shape = [jax.ShapeDtypeStruct((bn,) + st_shape, F32)]
    return pl.pallas_call(
        functools.partial(_ssd_kernel, nchunks=nchunks, emit_y=emit_y),
        out_shape=y_shape + h_shape + y_shape + h_shape,
        grid=(bn, nchunks),
        in_specs=in_f + in_b,
        out_specs=out_f + out_b,
        scratch_shapes=[pltpu.VMEM(st_shape, F32), pltpu.VMEM(st_shape, F32)],
        compiler_params=_cparams(2),
        name="ssd_scan",
    )(xs, bm, cm, dt, alog_pad, h0f, xs, bm, cm, dt, alog_pad, h0b)


MIX_SUB = 128
MIX_TILE = 512


def _split_bf16(v):
    hi = v.astype(BF16)
    return hi, (v - hi.astype(F32)).astype(BF16)


def _mix_kernel(yf_ref, yb_ref, xs_ref, z_ref, f_ref, g_ref, x_ref, g1_ref, sh2_ref, sc2_ref,
                dsk_ref, nssd_ref, npost_ref, npre_ref, wso_ref, wf_ref, wo_ref, wr_ref,
                hx_ref, hm_ref, aff_ref):
    dot = functools.partial(jnp.dot, preferred_element_type=F32)
    wr_hi, wr_lo = _split_bf16(wr_ref[...])
    subs = [slice(r0, r0 + MIX_SUB) for r0 in range(0, x_ref.shape[1], MIX_SUB)]
    ys = []
    for rs in subs:
        y = (yf_ref[0, rs].astype(F32) + yb_ref[0, rs].astype(F32)
             + dsk_ref[...] * xs_ref[0, rs].astype(F32))
        ys.append(_rms(y * _silu(z_ref[0, rs].astype(F32)), nssd_ref[...]).astype(BF16))
    s_branches = [dot(y, wso_ref[...]) for y in ys]
    f_branches = [dot(f_ref[0, rs].astype(BF16), wf_ref[...]) for rs in subs]
    merged = []
    for rs, s_branch, f_branch in zip(subs, s_branches, f_branches):
        gates = g_ref[0, rs].astype(F32)
        merged.append((gates[:, :D_MODEL] * f_branch + gates[:, D_MODEL:] * s_branch).astype(BF16))
    mixes = [dot(m, wo_ref[...]) for m in merged]
    hms = []
    for rs, mix in zip(subs, mixes):
        hx = x_ref[0, rs] + g1_ref[0] * _rms(mix, npost_ref[...])
        hx_ref[0, rs] = hx
        hm = _rms(hx, npre_ref[...]) * (1.0 + sc2_ref[0]) + sh2_ref[0]
        hm_ref[0, rs] = _to_row_tiles(hm)
        hms.append(hm)
    for rs, hm in zip(subs, hms):
        hm_hi, hm_lo = _split_bf16(hm)
        logits = dot(hm_hi, wr_hi) + (dot(hm_hi, wr_lo) + dot(hm_lo, wr_hi))
        real = lax.broadcasted_iota(jnp.int32, logits.shape, 1) < N_EXPERTS
        logits = jnp.where(real, logits, -1e30)
        e = jnp.exp(logits - jnp.max(logits, axis=-1, keepdims=True))
        aff = e / jnp.sum(e, axis=-1, keepdims=True)
        aff_ref[0, :, rs] = aff.T[:N_EXPERTS]


def _mix(yf, yb, xs, z, f, gates, x, g1, sh2, sc2, consts, tm):
    bn, length, d = x.shape
    tok = lambda width: pl.BlockSpec((1, tm, width), lambda b, i: (b, i, 0))
    vec = pl.BlockSpec((1, 1, d), lambda b, i: (b, 0, 0))
    return pl.pallas_call(
        _mix_kernel,
        out_shape=[jax.ShapeDtypeStruct((bn, length, d), F32),
                   jax.ShapeDtypeStruct((bn, length) + ROW_TILE, F32),
                   jax.ShapeDtypeStruct((bn, N_EXPERTS, length), F32)],
        grid=(bn, length // tm),
        in_specs=[tok(D_INNER), tok(D_INNER), tok(D_INNER), tok(D_INNER), tok(F_WIDTH), tok(2 * D_MODEL), tok(d),
                  vec, vec, vec] + [_const_spec(c.shape) for c in consts],
        out_specs=[tok(d), pl.BlockSpec((1, tm) + ROW_TILE, lambda b, i: (b, i, 0, 0)),
                   pl.BlockSpec((1, N_EXPERTS, tm), lambda b, i: (b, 0, i))],
        compiler_params=_cparams(2, VMEM_LIMIT),
        name="merge_out_router",
    )(yf, yb, xs, z, f, gates, x, g1, sh2, sc2, *consts)


def _to_tile_rows(v):
    return jnp.concatenate([v[:, k * LANES:(k + 1) * LANES] for k in range(v.shape[1] // LANES)], axis=0)


def _route_kernel(aff_ref, idx_ref, val_ref, *, cap):
    I32 = jnp.int32
    a = aff_ref[0]
    ne, length = a.shape
    nt = length // LANES
    rows = nt * ne

    def search(i, t):
        cand = t | lax.shift_left(jnp.int32(1), 30 - i)
        cnt = jnp.sum((a >= pltpu.bitcast(cand, F32)).astype(F32), axis=1, keepdims=True)
        return jnp.where(cnt >= cap, cand, t)

    thr = pltpu.bitcast(lax.fori_loop(0, 31, search, jnp.zeros((ne, 1), I32)), F32)
    n_gt = jnp.sum((a > thr).astype(F32), axis=1, keepdims=True)
    need = cap - n_gt

    val_r = _to_tile_rows(a)
    tile_rows = lambda v: jnp.concatenate([v] * nt, axis=0)
    thr_r = tile_rows(thr)
    need_r = tile_rows(need)
    li = lax.broadcasted_iota(I32, (LANES, LANES), 0)
    lj = lax.broadcasted_iota(I32, (LANES, LANES), 1)
    upper = (li < lj).astype(BF16)
    ri = lax.broadcasted_iota(I32, (rows, rows), 0)
    rj = lax.broadcasted_iota(I32, (rows, rows), 1)
    earlier = (((ri & (ne - 1)) == (rj & (ne - 1))) & (rj < ri)).astype(BF16)

    def prefix(mask):
        local = jnp.dot(mask.astype(BF16), upper, preferred_element_type=F32)
        total = jnp.sum(mask.astype(F32), axis=1, keepdims=True)
        tot_b = jnp.broadcast_to(total, (rows, LANES)).astype(BF16)
        offs = jnp.dot(earlier, tot_b, preferred_element_type=F32)[:, 0:1]
        return local, offs

    eq = val_r == thr_r
    eq_local, eq_off = prefix(eq)
    sel = (val_r > thr_r) | (eq & (eq_local + eq_off < need_r))
    sel_local, sel_off = prefix(sel)

    lane = lax.broadcasted_iota(I32, (rows, LANES), 1)
    k_row = lax.shift_right_logical(lax.broadcasted_iota(I32, (rows, LANES), 0), int(np.log2(ne)))
    tok = jnp.where(sel, k_row * LANES + lane, -1)
    val = val_r
    dist = jnp.where(sel, lane - sel_local.astype(I32), 0)
    for s in range(int(np.log2(LANES))):
        sh = 1 << s
        mv = (tok >= 0) & ((lax.shift_right_logical(dist, s) & 1) == 1)
        in_tok = pltpu.roll(jnp.where(mv, tok, -1), LANES - sh, axis=1)
        in_val = pltpu.roll(val, LANES - sh, axis=1)
        in_dist = pltpu.roll(dist, LANES - sh, axis=1)
        arrive = in_tok >= 0
        stay = (tok >= 0) & jnp.logical_not(mv)
        tok = jnp.where(arrive, in_tok, jnp.where(stay, tok, -1))
        val = jnp.where(arrive, in_val, val)
        dist = jnp.where(arrive, in_dist, dist)

    off = sel_off.astype(I32)
    lane_e = lax.broadcasted_iota(I32, (ne, LANES), 1)
    ncol = cap // LANES
    out_tok = [jnp.zeros((ne, LANES), I32) for _ in range(ncol)]
    out_val = [jnp.zeros((ne, LANES), F32) for _ in range(ncol)]
    for k in range(nt):
        rs = slice(k * ne, (k + 1) * ne)
        o = off[rs]
        sh, col = o & (LANES - 1), lax.shift_right_logical(o, int(np.log2(LANES)))
        pat = (lane_e - sh) & (LANES - 1)
        r_tok = jnp.take_along_axis(tok[rs], pat, axis=1)
        r_val = jnp.take_along_axis(val[rs], pat, axis=1)
        ok = r_tok >= 0
        for c in range(ncol):
            here = ok & (((col == c) & (lane_e >= sh)) | ((col + 1 == c) & (lane_e < sh)))
            out_tok[c] = jnp.where(here, r_tok, out_tok[c])
            out_val[c] = jnp.where(here, r_val, out_val[c])
    idx_ref[0] = jnp.concatenate(out_tok, axis=1)
    val_ref[0] = jnp.concatenate(out_val, axis=1)


def _route(aff_t, cap):
    bn, ne, length = aff_t.shape
    assert ne & (ne - 1) == 0 and cap % LANES == 0 and length % LANES == 0
    return pl.pallas_call(
        functools.partial(_route_kernel, cap=cap),
        out_shape=[jax.ShapeDtypeStruct((bn, ne, cap), jnp.int32), jax.ShapeDtypeStruct((bn, ne, cap), F32)],
        grid=(bn,),
        in_specs=[pl.BlockSpec((1, ne, length), lambda b: (b, 0, 0))],
        out_specs=[pl.BlockSpec((1, ne, cap), lambda b: (b, 0, 0))] * 2,
        compiler_params=_cparams(1),
        name="expert_choice_route",
    )(aff_t)


def _expert_kernel(idx_ref, idxn_ref, hm_ref, wg_ref, wu_ref, wd_ref, ye_ref, buf0, buf1, wgb, wub, wdb, sem,
                   *, cap, nb, nsteps):
    s = pl.program_id(0)
    bufs = (buf0, buf1)
    dot = functools.partial(jnp.dot, preferred_element_type=F32)

    def row_copy(k, j, b, r):
        return pltpu.make_async_copy(hm_ref.at[b, pl.ds(r, 1)], bufs[k].at[pl.ds(j, 1)], sem.at[k])

    def wait_rows(k):
        pltpu.make_async_copy(hm_ref.at[0, pl.ds(0, cap)], bufs[k], sem.at[k]).wait()

    b0 = (2 * s) % nb

    @pl.when(b0 == 0)
    def _():
        wgb[...] = wg_ref[0].astype(BF16)
        wub[...] = wu_ref[0].astype(BF16)
        wdb[...] = wd_ref[0].astype(BF16)

    @pl.when(s == 0)
    def _():
        def issue(j, carry):
            row_copy(0, j, b0, idx_ref[0, 0, j]).start()
            return carry
        lax.fori_loop(0, cap, issue, 0)

    for k in range(2):
        wait_rows(k)
        x = _from_row_tiles(bufs[k][...]).astype(BF16)
        if k == 0:
            for j in range(cap):
                row_copy(1, j, b0 + 1, idx_ref[1, 0, j]).start(priority=j % 2)
            for j in range(cap):
                row_copy(0, j, (b0 + 2) % nb, idxn_ref[0, 0, j]).start(priority=j % 2)
        hid = (_silu(dot(x, wgb[...])) * dot(x, wub[...])).astype(BF16)
        ye_ref[k] = _to_row_tiles(dot(hid, wdb[...]))

    @pl.when(s == nsteps - 1)
    def _():
        wait_rows(0)


def _experts(idx, hm, wg, wu, wd):
    bn = hm.shape[0]
    ne, d, ff = wg.shape
    cap = idx.shape[-1]
    per = bn // 2
    nsteps = ne * per
    cur = lambda s: (s, 0, 0)
    nxt = lambda s: ((s + 1) % nsteps, 0, 0)
    wspec = pl.BlockSpec((1, d, ff), lambda s: (s // per, 0, 0))
    return pl.pallas_call(
        functools.partial(_expert_kernel, cap=cap, nb=bn, nsteps=nsteps),
        out_shape=jax.ShapeDtypeStruct((ne * bn, cap) + ROW_TILE, F32),
        grid=(nsteps,),
        in_specs=[pl.BlockSpec((2, 1, cap), cur, memory_space=pltpu.SMEM),
                  pl.BlockSpec((2, 1, cap), nxt, memory_space=pltpu.SMEM),
                  pl.BlockSpec(memory_space=pl.ANY),
                  wspec, wspec, pl.BlockSpec((1, ff, d), lambda s: (s // per, 0, 0))],
        out_specs=pl.BlockSpec((2, cap) + ROW_TILE, lambda s: (s, 0, 0, 0)),
        scratch_shapes=[pltpu.VMEM((cap,) + ROW_TILE, F32), pltpu.VMEM((cap,) + ROW_TILE, F32),
                        pltpu.VMEM((d, ff), BF16), pltpu.VMEM((d, ff), BF16), pltpu.VMEM((ff, d), BF16),
                        pltpu.SemaphoreType.DMA((2,))],
        compiler_params=_cparams(1, VMEM_LIMIT),
        name="expert_ffn",
    )(idx, idx, hm, wg, wu, wd)


COMBINE_ROWS = 16


COMBINE_EXPERTS = 2


def _combine_kernel(*refs, cap, nb, rows_out):
    n = COMBINE_EXPERTS
    idx_refs, val_refs, ye_refs = refs[:n], refs[n:2 * n], refs[2 * n:3 * n]
    hx_ref, g2_ref, w_ref, o_ref, acc = refs[3 * n:]
    b = pl.program_id(0)
    e2 = pl.program_id(1)
    slot = b % 2
    acc_cur = acc.at[slot]

    @pl.when(jnp.logical_and(e2 == 0, b < nb))
    def _():
        acc_cur[...] = jnp.zeros(acc.shape[1:], F32)

    @pl.when(b < nb)
    def _():
        for idx_ref, val_ref, ye_ref in zip(idx_refs, val_refs, ye_refs):
            def body(i, carry, idx_ref=idx_ref, val_ref=val_ref, ye_ref=ye_ref):
                j0 = i * COMBINE_ROWS
                rows = [idx_ref[0, 0, j0 + u] for u in range(COMBINE_ROWS)]
                new = [acc_cur[rows[u]] + ye_ref[0, j0 + u] * val_ref[0, 0, j0 + u]
                       for u in range(COMBINE_ROWS)]
                for u in range(COMBINE_ROWS):
                    acc_cur[rows[u]] = new[u]
                return carry

            lax.fori_loop(0, cap // COMBINE_ROWS, body, 0)

    @pl.when(b > 0)
    def _():
        f3 = acc[1 - slot, pl.ds(pl.multiple_of(e2 * rows_out, rows_out), rows_out)]
        ms = jnp.mean(f3 * f3, axis=(1, 2), keepdims=True)
        y = _from_row_tiles(f3 * lax.rsqrt(ms + RMS_EPS) * w_ref[...])
        o_ref[0] = hx_ref[0] + g2_ref[0] * y


def _combine_residual(idx, vals, ye, hx, g2, w):
    bn, length, d = hx.shape
    n = COMBINE_EXPERTS
    steps = N_EXPERTS // n
    cap = idx.shape[-1]
    rows_out = length // steps
    w3 = w.reshape((1,) + ROW_TILE)
    cur = lambda b: jnp.minimum(b, bn - 1)
    prev = lambda b: jnp.maximum(b - 1, 0)
    item = lambda k: (lambda b, e2: ((e2 * n + k) * bn + cur(b), 0, 0))
    item4 = lambda k: (lambda b, e2: ((e2 * n + k) * bn + cur(b), 0, 0, 0))
    smem = [pl.BlockSpec((1, 1, cap), item(k), memory_space=pltpu.SMEM) for k in range(n)]
    tok_prev = pl.BlockSpec((1, rows_out, d), lambda b, e2: (prev(b), jnp.where(b > 0, e2, 0), 0))
    return pl.pallas_call(
        functools.partial(_combine_kernel, cap=cap, nb=bn, rows_out=rows_out),
        out_shape=jax.ShapeDtypeStruct((bn, length, d), F32),
        grid=(bn + 1, steps),
        in_specs=smem + smem + [pl.BlockSpec((1, cap) + ROW_TILE, item4(k)) for k in range(n)]
                 + [tok_prev, pl.BlockSpec((1, 1, d), lambda b, e2: (prev(b), 0, 0)), _const_spec(w3.shape)],
        out_specs=tok_prev,
        scratch_shapes=[pltpu.VMEM((2, length) + ROW_TILE, F32)],
        compiler_params=_cparams(2, VMEM_LIMIT),
        name="expert_combine_residual",
    )(*([idx] * n + [vals] * n + [ye] * n), hx, g2, w3)


def kernel(x, c, ctx, c_ctx, w_mod, b_mod, norm_mix_pre, norm_mix_post, norm_ffn_pre, norm_ffn_post,
           w_in, conv_w, conv_b, dt_bias, a_log, d_skip, ssd_norm, w_fourier, w_ssd_out, b_gate, w_out,
           w_router, w_e_gate, w_e_up, w_e_down):
    bn, length, d = x.shape
    ctx_len = ctx.shape[1]
    l = 0
    row = lambda v: v.reshape(1, -1)

    cs = jnp.concatenate([c, c_ctx[None], jnp.zeros((7, d), F32)], axis=0)
    mods = _modulation(cs, w_mod[l], b_mod[l])
    mod_x = mods[:bn].reshape(bn, N_MOD, 1, d)
    mod_c = jnp.broadcast_to(mods[bn].reshape(1, N_MOD, 1, d), (bn, N_MOD, 1, d))
    sh1, sc1, g1, sh2, sc2, g2 = [mod_x[:, i] for i in range(N_MOD)]

    i1 = F_WIDTH
    i2 = i1 + D_INNER
    i3 = i2 + D_INNER
    i4 = i3 + BC_WIDTH
    i5 = i4 + BC_WIDTH
    i6 = i5 + 2 * N_HEADS
    wi = w_in[l]
    wdt = jnp.zeros((d, 256), F32).at[:, :N_HEADS].set(wi[:, i5:i5 + N_HEADS])
    wdt = wdt.at[:, 128:128 + N_HEADS].set(wi[:, i5 + N_HEADS:i6])
    dtb = jnp.zeros((1, 256), F32).at[0, :N_HEADS].set(dt_bias[l, 0]).at[0, 128:128 + N_HEADS].set(dt_bias[l, 1])
    dft_c, w1, tw_c, tw_s, w2 = _dft_constants()
    cw, cb = conv_w[l], conv_b[l]
    j1, j2 = D_INNER, D_INNER + BC_WIDTH
    w = dict(
        wuf=wi[:, :i1].astype(BF16), wz=wi[:, i1:i2].astype(BF16), wxs=wi[:, i2:i3].astype(BF16),
        wb=wi[:, i3:i4].astype(BF16), wc=wi[:, i4:i5].astype(BF16), wdt=wdt.astype(BF16),
        wg=wi[:, i6:].astype(BF16), dft=dft_c,
        cwx=cw[:, :j1], cbx=row(cb[:j1]), cwb=cw[:, j1:j2], cbb=row(cb[j1:j2]),
        cwc=cw[:, j2:], cbc=row(cb[j2:]), dtb=dtb, bg=row(b_gate[l]))
    gain_pre = row(norm_mix_pre[l])

    alog_pad = jnp.zeros((2, 1, 128), F32).at[:, 0, :N_HEADS].set(a_log[l])

    cxs, cbm, ccm, cdt = _in_projection(ctx, mod_c[:, 0], mod_c[:, 1], gain_pre, w, ctx_len, ctx_len, False)
    h_zero = jnp.zeros((bn, N_GROUPS, D_STATE, GROUP_COLS), F32)
    hc_f, hc_b = _ssd_scan(cxs, cbm, ccm, cdt, alog_pad, h_zero, h_zero, emit_y=False)

    zr, zi, z, xs, bm, cm, dt, gates = _in_projection(x, sh1, sc1, gain_pre, w, GRID_W, INPROJ_TILE, True)
    f = _sequence_dft_real(zr, zi, w1, tw_c, tw_s, w2)
    yf, _, yb, _ = _ssd_scan(xs, bm, cm, dt, alog_pad, hc_f, hc_b)

    consts = [row(jnp.repeat(d_skip[l], HEAD_DIM)), row(ssd_norm[l]), row(norm_mix_post[l]),
              row(norm_ffn_pre[l]), w_ssd_out[l].astype(BF16), w_fourier[l].astype(BF16),
              w_out[l].astype(BF16), jnp.pad(w_router[l], ((0, 0), (0, LANES - N_EXPERTS)))]
    hx, hm, aff_t = _mix(yf, yb, xs, z, f, gates, x, g1, sh2, sc2, consts, MIX_TILE)

    cap = EC_FACTOR * length // N_EXPERTS
    top_idx, top_aff = _route(aff_t, cap)
    idx = jnp.swapaxes(top_idx, 0, 1).reshape(N_EXPERTS * bn, 1, cap)
    vals = jnp.swapaxes(top_aff, 0, 1).reshape(N_EXPERTS * bn, 1, cap)
    ye = _experts(idx, hm, w_e_gate[l], w_e_up[l], w_e_down[l])
    return _combine_residual(idx, vals, ye, hx, g2, row(norm_ffn_post[l]))
```

```python
import functools

import numpy as np
import jax
import jax.numpy as jnp
from jax import lax
from jax.experimental import pallas as pl
from jax.experimental.pallas import tpu as pltpu

F32 = jnp.float32
BF16 = jnp.bfloat16
HIGHEST = lax.Precision.HIGHEST

D_MODEL = 1024
GRID_W = 64
F_GROUP_W = 128
F_WIDTH = 512
D_INNER = 1536
HEAD_DIM = 64
N_HEADS = 24
N_GROUPS = 4
HEADS_PER_GROUP = 6
D_STATE = 128
CHUNK = 128
BC_WIDTH = N_GROUPS * D_STATE
N_EXPERTS = 16
EC_FACTOR = 2
N_MOD = 6
RMS_EPS = 1e-6
GROUP_COLS = HEADS_PER_GROUP * HEAD_DIM

FFT_N1 = 128
FFT_N2 = 32
FFT_KB = 8

VMEM_LIMIT = 56 * 1024 * 1024
LANES = 128


def _cparams(n_axes, vmem=None):
    return pltpu.CompilerParams(dimension_semantics=("arbitrary",) * n_axes,
                                vmem_limit_bytes=vmem)


def _const_spec(shape):
    nd = len(shape)
    return pl.BlockSpec(shape, lambda *_: (0,) * nd, pipeline_mode=pl.Buffered(1))


def _rms(x, w):
    ms = jnp.mean(x * x, axis=-1, keepdims=True)
    return x * lax.rsqrt(ms + RMS_EPS) * w


def _sigmoid(x):
    return 1.0 / (1.0 + jnp.exp(-x))


def _silu(x):
    return x * _sigmoid(x)


ROW_TILE = (8, 128)
PAIR_TILE = (16, 128)


def _to_row_tiles(v):
    parts = [v[:, c * 128:(c + 1) * 128] for c in range(ROW_TILE[0])]
    return jnp.swapaxes(jnp.stack(parts, axis=0), 0, 1)


def _from_row_tiles(v3):
    t = jnp.swapaxes(v3, 0, 1)
    return jnp.concatenate([t[c] for c in range(ROW_TILE[0])], axis=1)


def _mod_kernel(c_ref, w_ref, b_ref, o_ref):
    c = c_ref[...]
    o_ref[...] = jnp.dot(_silu(c), w_ref[...], precision=HIGHEST,
                         preferred_element_type=F32) + b_ref[...]


def _modulation(cs, w_mod, b_mod):
    rows = cs.shape[0]
    n = w_mod.shape[1]
    blk = D_MODEL
    return pl.pallas_call(
        _mod_kernel,
        out_shape=jax.ShapeDtypeStruct((rows, n), F32),
        grid=(n // blk,),
        in_specs=[pl.BlockSpec((rows, D_MODEL), lambda j: (0, 0)),
                  pl.BlockSpec((D_MODEL, blk), lambda j: (0, j)),
                  pl.BlockSpec((1, blk), lambda j: (0, j))],
        out_specs=pl.BlockSpec((rows, blk), lambda j: (0, j)),
        compiler_params=_cparams(1),
        name="modulation",
    )(cs, w_mod, b_mod.reshape(1, n))


def _conv_silu(xbc, cw_ref, cb_ref, row_len):
    tm = xbc.shape[0]
    pos = lax.broadcasted_iota(jnp.int32, (tm, 1), 0) % row_len
    prev = jnp.where(pos == 0, 0.0, pltpu.roll(xbc, 1, axis=0))
    nxt = jnp.where(pos == row_len - 1, 0.0, pltpu.roll(xbc, tm - 1, axis=0))
    out = prev * cw_ref[0:1, :] + xbc * cw_ref[1:2, :] + nxt * cw_ref[2:3, :] + cb_ref[...]
    return _silu(out)


INPROJ_SUB = 128
INPROJ_TILE = 512


def _softplus(x):
    return jnp.maximum(x, 0.0) + jnp.log(1.0 + jnp.exp(-jnp.abs(x)))


def _inproj_kernel(x_ref, sh_ref, sc_ref, gain_ref, wuf_ref, wz_ref, wxs_ref, wb_ref, wc_ref,
                   wdt_ref, wg_ref, dft_ref, cwx_ref, cbx_ref, cwb_ref, cbb_ref, cwc_ref, cbc_ref,
                   dtb_ref, bg_ref,
                   zr_ref, zi_ref, z_ref, xs_ref, bm_ref, cm_ref, dt_ref, g_ref, *, row_len):
    dot = functools.partial(jnp.dot, preferred_element_type=F32)
    tm = x_ref.shape[1]
    sub = min(tm, max(INPROJ_SUB, row_len))
    assert sub % row_len == 0 and tm % sub == 0
    for r0 in range(0, tm, sub):
        rs = slice(r0, r0 + sub)
        h = _rms(x_ref[0, rs], gain_ref[...]) * (1.0 + sc_ref[0]) + sh_ref[0]
        hb = h.astype(BF16)
        xs_ref[0, rs] = _conv_silu(dot(hb, wxs_ref[...]), cwx_ref, cbx_ref, row_len).astype(BF16)
        bm_ref[0, rs] = _conv_silu(dot(hb, wb_ref[...]), cwb_ref, cbb_ref, row_len).astype(BF16)
        cm_ref[0, rs] = _conv_silu(dot(hb, wc_ref[...]), cwc_ref, cbc_ref, row_len).astype(BF16)
        dt_ref[0, rs] = _softplus(dot(hb, wdt_ref[...]) + dtb_ref[...])
        if zr_ref is None:
            continue
        uf = dot(hb, wuf_ref[...]).astype(BF16)
        for g in range(F_WIDTH // F_GROUP_W):
            sl = slice(g * F_GROUP_W, (g + 1) * F_GROUP_W)
            zz = dot(uf[:, sl], dft_ref[...].astype(BF16))
            zr_ref[0, rs, sl] = zz[:, :F_GROUP_W].astype(BF16)
            zi_ref[0, rs, sl] = zz[:, F_GROUP_W:].astype(BF16)
        z_ref[0, rs] = dot(hb, wz_ref[...]).astype(BF16)
        g_ref[0, rs] = _sigmoid(dot(hb, wg_ref[...]) + bg_ref[...]).astype(BF16)


def _inproj_ctx_kernel(x_ref, sh_ref, sc_ref, gain_ref, wxs_ref, wb_ref, wc_ref, wdt_ref,
                       cwx_ref, cbx_ref, cwb_ref, cbb_ref, cwc_ref, cbc_ref, dtb_ref,
                       xs_ref, bm_ref, cm_ref, dt_ref, *, row_len):
    _inproj_kernel(x_ref, sh_ref, sc_ref, gain_ref, None, None, wxs_ref, wb_ref, wc_ref,
                   wdt_ref, None, None, cwx_ref, cbx_ref, cwb_ref, cbb_ref, cwc_ref, cbc_ref,
                   dtb_ref, None, None, None, None, xs_ref, bm_ref, cm_ref, dt_ref, None,
                   row_len=row_len)


def _in_projection(x, shift, scale, gain, w, row_len, tm, full):
    bn, length, d = x.shape
    grid = (bn, length // tm)
    tok = lambda width: pl.BlockSpec((1, tm, width), lambda b, i: (b, i, 0))
    vec = pl.BlockSpec((1, 1, d), lambda b, i: (b, 0, 0))
    out = lambda width, dt: jax.ShapeDtypeStruct((bn, length, width), dt)
    conv_ops = [w["cwx"], w["cbx"], w["cwb"], w["cbb"], w["cwc"], w["cbc"]]
    ssd_shapes = [out(D_INNER, BF16), out(BC_WIDTH, BF16), out(BC_WIDTH, BF16), out(256, F32)]
    ssd_specs = [tok(D_INNER), tok(BC_WIDTH), tok(BC_WIDTH), tok(256)]
    if full:
        ops = [x, shift, scale, gain, w["wuf"], w["wz"], w["wxs"], w["wb"], w["wc"], w["wdt"], w["wg"],
               w["dft"]] + conv_ops + [w["dtb"], w["bg"]]
        kern = functools.partial(_inproj_kernel, row_len=row_len)
        out_shape = [out(F_WIDTH, BF16), out(F_WIDTH, BF16), out(D_INNER, BF16)] + ssd_shapes + \
                    [out(2 * D_MODEL, BF16)]
        out_specs = [tok(F_WIDTH), tok(F_WIDTH), tok(D_INNER)] + ssd_specs + [tok(2 * D_MODEL)]
        name = "in_projection"
    else:
        ops = [x, shift, scale, gain, w["wxs"], w["wb"], w["wc"], w["wdt"]] + conv_ops + [w["dtb"]]
        kern = functools.partial(_inproj_ctx_kernel, row_len=row_len)
        out_shape = ssd_shapes
        out_specs = ssd_specs
        name = "in_projection_ctx"
    in_specs = [tok(d), vec, vec] + [_const_spec(o.shape) for o in ops[3:]]
    return pl.pallas_call(kern, out_shape=out_shape, grid=grid, in_specs=in_specs,
                          out_specs=out_specs, compiler_params=_cparams(2, VMEM_LIMIT),
                          name=name)(*ops)


def _dft_constants():
    j = np.arange(F_GROUP_W)
    ang = 2.0 * np.pi * np.outer(j, j) / F_GROUP_W
    dft_c = np.concatenate([np.cos(ang), -np.sin(ang)], axis=1)
    k1 = np.arange(FFT_N1)
    a1 = 2.0 * np.pi * np.outer(k1, k1) / FFT_N1
    c1, s1 = np.cos(a1), np.sin(a1)
    w1 = np.block([[c1, s1], [-s1, c1]])
    n2 = np.arange(FFT_N2)
    at = 2.0 * np.pi * np.outer(k1, n2) / (FFT_N1 * FFT_N2)
    tw_c = np.repeat(np.cos(at), 128, axis=1)
    tw_s = np.repeat(np.sin(at), 128, axis=1)
    a2 = 2.0 * np.pi * np.outer(n2, n2) / FFT_N2
    w2 = np.zeros((FFT_N2, FFT_KB, 2, FFT_KB, FFT_N2))
    for jj in range(FFT_KB):
        w2[:, jj, 0, jj, :] = np.cos(a2)
        w2[:, jj, 1, jj, :] = np.sin(a2)
    w2 = w2.reshape(FFT_N2 * FFT_KB, 2 * FFT_KB * FFT_N2)
    return (jnp.asarray(dft_c, F32), jnp.asarray(w1, F32), jnp.asarray(tw_c, F32),
            jnp.asarray(tw_s, F32), jnp.asarray(w2, F32))


FFT_QB = 16


def _seq_dft_kernel(zr_ref, zi_ref, w1_ref, twc_ref, tws_ref, w2_ref, o_ref, qr_s, qi_s, tr_s, ti_s, *, scale):
    w1 = w1_ref[...].astype(BF16)
    reps = F_WIDTH // 128
    for qb in range(FFT_N2 // FFT_QB):
        qs = slice(qb * FFT_QB, (qb + 1) * FFT_QB)
        zr_t = jnp.swapaxes(zr_ref[0, :, qs, :], 0, 1)
        zi_t = jnp.swapaxes(zi_ref[0, :, qs, :], 0, 1)
        for q in range(FFT_QB):
            n2 = qb * FFT_QB + q
            rhs = jnp.concatenate([zr_t[q], zi_t[q]], axis=0)
            t = jnp.dot(w1, rhs, preferred_element_type=F32)
            a, b = t[:FFT_N1], t[FFT_N1:]
            cw = jnp.concatenate([twc_ref[:, n2 * 128:(n2 + 1) * 128]] * reps, axis=1)
            sw = jnp.concatenate([tws_ref[:, n2 * 128:(n2 + 1) * 128]] * reps, axis=1)
            qr_s[q] = (a * cw + b * sw).astype(BF16)
            qi_s[q] = (b * cw - a * sw).astype(BF16)
        tr_s[:, qs, :] = jnp.swapaxes(qr_s[...], 0, 1)
        ti_s[:, qs, :] = jnp.swapaxes(qi_s[...], 0, 1)
    w2 = w2_ref[...].astype(BF16)
    rows = FFT_KB * FFT_N2
    for kb in range(FFT_N1 // FFT_KB):
        ks = slice(kb * FFT_KB, (kb + 1) * FFT_KB)
        rhs = jnp.concatenate([tr_s[ks].reshape(rows, F_WIDTH), ti_s[ks].reshape(rows, F_WIDTH)], axis=0)
        y = jnp.dot(w2, rhs, preferred_element_type=F32) * scale
        o_ref[0, :, ks, :] = y.reshape(FFT_N2, FFT_KB, F_WIDTH)


def _sequence_dft_real(zr, zi, w1, tw_c, tw_s, w2):
    bn, length, width = zr.shape
    blk = pl.BlockSpec((1, FFT_N1, FFT_N2, width), lambda b: (b, 0, 0, 0))
    scale = 1.0 / float(np.sqrt(length * F_GROUP_W))
    f = pl.pallas_call(
        functools.partial(_seq_dft_kernel, scale=scale),
        out_shape=jax.ShapeDtypeStruct((bn, FFT_N2, FFT_N1, width), F32),
        grid=(bn,),
        in_specs=[blk, blk, _const_spec(w1.shape), _const_spec(tw_c.shape), _const_spec(tw_s.shape),
                  _const_spec(w2.shape)],
        out_specs=pl.BlockSpec((1, FFT_N2, FFT_N1, width), lambda b: (b, 0, 0, 0)),
        scratch_shapes=[pltpu.VMEM((FFT_QB, FFT_N1, width), BF16)] * 2
                       + [pltpu.VMEM((FFT_N1, FFT_N2, width), BF16)] * 2,
        compiler_params=_cparams(1, VMEM_LIMIT),
        name="seq_dft",
    )(zr.reshape(bn, FFT_N1, FFT_N2, width), zi.reshape(bn, FFT_N1, FFT_N2, width), w1, tw_c, tw_s, w2)
    return f.reshape(bn, length, width)


SSD_SUB = 64
SSD_CHUNKS_PER_STEP = 4
LOG2E = 1.4426950408889634


class _Dir:
    pass


def _ssd_prepare(dt_ref, alog_ref, reverse, row0):
    q, sb = CHUNK, SSD_SUB
    d = _Dir()
    d.reverse = reverse
    r_i = lax.broadcasted_iota(jnp.int32, (q, q), 0)
    c_i = lax.broadcasted_iota(jnp.int32, (q, q), 1)
    tri = ((r_i <= c_i) if reverse else (r_i >= c_i)).astype(BF16)
    a = -jnp.exp(alog_ref[0])
    d.row0 = row0
    d.dt = dt_ref[0, row0:row0 + q]
    da = d.dt * (a * LOG2E)
    p0 = da.astype(BF16)
    r1 = da - p0.astype(F32)
    p1 = r1.astype(BF16)
    p2 = (r1 - p1.astype(F32)).astype(BF16)
    parts = jnp.dot(tri, jnp.concatenate([p0, p1, p2], axis=1), preferred_element_type=F32)
    d.acum = parts[:, :LANES] + (parts[:, LANES:2 * LANES] + parts[:, 2 * LANES:])
    lane = lax.broadcasted_iota(jnp.int32, (sb, 128), 1)
    d.lo = lane < HEAD_DIM
    lo_h = lax.broadcasted_iota(jnp.int32, (q, 128), 1) < HEAD_DIM
    row_l = lax.broadcasted_iota(jnp.int32, (sb, 128), 0)
    s_l = jnp.where(d.lo, lane, lane - HEAD_DIM)
    d.mask = (s_l >= row_l) if reverse else (s_l <= row_l)
    d.order = (1, 0) if reverse else (0, 1)

    def halves(m):
        m_t = m.T
        m_sw = pltpu.roll(m_t, sb, axis=1)
        return (jnp.where(lo_h, m_t, m_sw), jnp.where(lo_h, m_sw, m_t))

    d.rt = halves(d.acum)
    d.rt_dt = halves(d.dt)
    return d


def _ssd_block_rows(d, step):
    sb = SSD_SUB
    blk = d.order[step]
    end_r = blk * sb if d.reverse else blk * sb + sb - 1
    end_row = d.acum[end_r:end_r + 1, :]
    if step == 0:
        base_row = jnp.zeros((1, 128), F32)
    else:
        pr = d.order[0] * sb if d.reverse else d.order[0] * sb + sb - 1
        base_row = d.acum[pr:pr + 1, :]
    d.rows8 = jnp.concatenate([base_row, jnp.exp2(end_row - base_row), jnp.zeros((6, 128), F32)], axis=0)
    local = slice(blk * sb, (blk + 1) * sb)
    d.rs = slice(d.row0 + blk * sb, d.row0 + (blk + 1) * sb)
    d.blk = blk
    d.acum_b = d.acum[local]
    d.dw_b = d.dt[local] * jnp.exp2(end_row - d.acum_b)


def _ssd_group_begin(d, refs, g):
    xs_ref, bm_ref, cm_ref, y_ref, st_ref = refs
    gs = slice(g * D_STATE, (g + 1) * D_STATE)
    bg = bm_ref[0, d.rs, gs]
    cg = cm_ref[0, d.rs, gs]
    bg_t = bg.astype(F32).T.astype(BF16)
    h_t = st_ref[g]
    d.grp = dict(bg_t=bg_t, h_t=h_t, xw=[], cd=[])
    if y_ref is not None:
        d.grp["cb2"] = jnp.dot(cg, jnp.concatenate([bg_t, bg_t], axis=1), preferred_element_type=F32)
        d.grp["yoff"] = jnp.dot(cg, h_t.astype(BF16), preferred_element_type=F32)


def _ssd_pair(d, refs, g, pr_i):
    xs_ref, bm_ref, cm_ref, y_ref, st_ref = refs
    lo = d.lo
    lo_row = lo[0:1]
    h0 = g * HEADS_PER_GROUP + 2 * pr_i
    h1 = h0 + 1
    ps = slice((g * 3 + pr_i) * 128, (g * 3 + pr_i + 1) * 128)
    pat = jnp.where(lo, h0, h1)
    dwp = jnp.take_along_axis(d.dw_b, pat, axis=1)
    r8 = jnp.take_along_axis(d.rows8, pat[0:8], axis=1)
    base_p, cd_p = r8[0:1], r8[1:2]
    x = xs_ref[0, d.rs, ps]
    gd = d.grp
    gd["xw"].append((x.astype(F32) * dwp).astype(BF16))
    gd["cd"].append(cd_p)
    if y_ref is None:
        return
    colp = jnp.take_along_axis(d.acum_b, pat, axis=1)
    rt, rt_dt = d.rt[d.blk], d.rt_dt[d.blk]
    rowp = jnp.where(lo_row, rt[h0:h0 + 1, :], rt[h1:h1 + 1, :])
    dt_row = jnp.where(lo_row, rt_dt[h0:h0 + 1, :], rt_dt[h1:h1 + 1, :])
    m = (jnp.exp2(jnp.where(d.mask, colp - rowp, -1e30)) * (gd["cb2"] * dt_row)).astype(BF16)
    zero = jnp.zeros_like(x)
    rhs = jnp.concatenate([jnp.where(lo, x, zero), jnp.where(lo, zero, x)], axis=0)
    ydiag = jnp.dot(m, rhs, preferred_element_type=F32)
    y = ydiag + gd["yoff"][:, pr_i * 128:(pr_i + 1) * 128] * jnp.exp2(colp - base_p)
    y_ref[0, d.rs, ps] = y.astype(BF16)


def _ssd_group_end(d, refs, g):
    st_ref = refs[4]
    gd = d.grp
    xw = jnp.concatenate(gd["xw"], axis=1)
    cd = jnp.concatenate(gd["cd"], axis=1)
    st_ref[g] = gd["h_t"] * cd + jnp.dot(gd["bg_t"], xw, preferred_element_type=F32)


def _ssd_kernel(xf, bf, cf, df, af, hf, xb, bb, cb_, db, ab, hb, *out_and_scratch, nchunks, emit_y):
    if emit_y:
        yf, hff, yb, hfb, stf, stb = out_and_scratch
    else:
        (hff, hfb, stf, stb), yf, yb = out_and_scratch, None, None
    c = pl.program_id(1)

    @pl.when(c == 0)
    def _():
        stf[...] = hf[0]
        stb[...] = hb[0]

    refs = ((xf, bf, cf, yf, stf), (xb, bb, cb_, yb, stb))
    per_step = xf.shape[1] // CHUNK
    for ci in range(per_step):
        dirs = (_ssd_prepare(df, af, False, ci * CHUNK), _ssd_prepare(db, ab, True, (per_step - 1 - ci) * CHUNK))
        for step in range(CHUNK // SSD_SUB):
            for d in dirs:
                _ssd_block_rows(d, step)
            for g in range(N_GROUPS):
                for d, r in zip(dirs, refs):
                    _ssd_group_begin(d, r, g)
                for pr_i in range(HEADS_PER_GROUP // 2):
                    for d, r in zip(dirs, refs):
                        _ssd_pair(d, r, g, pr_i)
                for d, r in zip(dirs, refs):
                    _ssd_group_end(d, r, g)

    @pl.when(c == nchunks - 1)
    def _():
        hff[0] = stf[...]
        hfb[0] = stb[...]


def _ssd_scan(xs, bm, cm, dt, alog_pad, h0f, h0b, emit_y=True):
    bn, length, _ = xs.shape
    rows = CHUNK * min(SSD_CHUNKS_PER_STEP, length // CHUNK)
    nchunks = length // rows
    st_shape = (N_GROUPS, D_STATE, GROUP_COLS)
    st_spec = pl.BlockSpec((1,) + st_shape, lambda b, c: (b, 0, 0, 0))

    def specs(reverse):
        cc = (lambda c: nchunks - 1 - c) if reverse else (lambda c: c)
        di = 1 if reverse else 0
        tok = lambda width: pl.BlockSpec((1, rows, width), lambda b, c: (b, cc(c), 0))
        ins = [tok(D_INNER), tok(BC_WIDTH), tok(BC_WIDTH),
               pl.BlockSpec((1, rows, 128), lambda b, c: (b, cc(c), di)),
               pl.BlockSpec((1, 1, 128), lambda b, c: (di, 0, 0)), st_spec]
        return ins, ([tok(D_INNER)] if emit_y else []) + [st_spec]

    in_f, out_f = specs(False)
    in_b, out_b = specs(True)
    y_shape = [jax.ShapeDtypeStruct((bn, length, D_INNER), BF16)] if emit_y else []
    h_shape = [jax.ShapeDtypeStruct((bn,) + st_shape, F32)]
    return pl.pallas_call(
        functools.partial(_ssd_kernel, nchunks=nchunks, emit_y=emit_y),
        out_shape=y_shape + h_shape + y_shape + h_shape,
        grid=(bn, nchunks),
        in_specs=in_f + in_b,
        out_specs=out_f + out_b,
        scratch_shapes=[pltpu.VMEM(st_shape, F32), pltpu.VMEM(st_shape, F32)],
        compiler_params=_cparams(2),
        name="ssd_scan",
    )(xs, bm, cm, dt, alog_pad, h0f, xs, bm, cm, dt, alog_pad, h0b)


MIX_SUB = 128
MIX_TILE = 512


def _split_bf16(v):
    hi = v.astype(BF16)
    return hi, (v - hi.astype(F32)).astype(BF16)


def _mix_kernel(yf_ref, yb_ref, xs_ref, z_ref, f_ref, g_ref, x_ref, g1_ref, sh2_ref, sc2_ref,
                dsk_ref, nssd_ref, npost_ref, npre_ref, wso_ref, wf_ref, wo_ref, wr_ref,
                hx_ref, hm_ref, aff_ref):
    dot = functools.partial(jnp.dot, preferred_element_type=F32)
    wr_hi, wr_lo = _split_bf16(wr_ref[...])
    subs = [slice(r0, r0 + MIX_SUB) for r0 in range(0, x_ref.shape[1], MIX_SUB)]
    ys = []
    for rs in subs:
        y = (yf_ref[0, rs].astype(F32) + yb_ref[0, rs].astype(F32)
             + dsk_ref[...] * xs_ref[0, rs].astype(F32))
        ys.append(_rms(y * _silu(z_ref[0, rs].astype(F32)), nssd_ref[...]).astype(BF16))
    s_branches = [dot(y, wso_ref[...]) for y in ys]
    f_branches = [dot(f_ref[0, rs].astype(BF16), wf_ref[...]) for rs in subs]
    merged = []
    for rs, s_branch, f_branch in zip(subs, s_branches, f_branches):
        gates = g_ref[0, rs].astype(F32)
        merged.append((gates[:, :D_MODEL] * f_branch + gates[:, D_MODEL:] * s_branch).astype(BF16))
    mixes = [dot(m, wo_ref[...]) for m in merged]
    hms = []
    for rs, mix in zip(subs, mixes):
        hx = x_ref[0, rs] + g1_ref[0] * _rms(mix, npost_ref[...])
        hx_ref[0, rs] = hx
        hm = _rms(hx, npre_ref[...]) * (1.0 + sc2_ref[0]) + sh2_ref[0]
        hm_ref[0, rs] = _to_row_tiles(hm)
        hms.append(hm)
    for rs, hm in zip(subs, hms):
        hm_hi, hm_lo = _split_bf16(hm)
        logits = dot(hm_hi, wr_hi) + (dot(hm_hi, wr_lo) + dot(hm_lo, wr_hi))
        real = lax.broadcasted_iota(jnp.int32, logits.shape, 1) < N_EXPERTS
        logits = jnp.where(real, logits, -1e30)
        e = jnp.exp(logits - jnp.max(logits, axis=-1, keepdims=True))
        aff = e / jnp.sum(e, axis=-1, keepdims=True)
        aff_ref[0, :, rs] = aff.T[:N_EXPERTS]


def _mix(yf, yb, xs, z, f, gates, x, g1, sh2, sc2, consts, tm):
    bn, length, d = x.shape
    tok = lambda width: pl.BlockSpec((1, tm, width), lambda b, i: (b, i, 0))
    vec = pl.BlockSpec((1, 1, d), lambda b, i: (b, 0, 0))
    return pl.pallas_call(
        _mix_kernel,
        out_shape=[jax.ShapeDtypeStruct((bn, length, d), F32),
                   jax.ShapeDtypeStruct((bn, length) + ROW_TILE, F32),
                   jax.ShapeDtypeStruct((bn, N_EXPERTS, length), F32)],
        grid=(bn, length // tm),
        in_specs=[tok(D_INNER), tok(D_INNER), tok(D_INNER), tok(D_INNER), tok(F_WIDTH), tok(2 * D_MODEL), tok(d),
                  vec, vec, vec] + [_const_spec(c.shape) for c in consts],
        out_specs=[tok(d), pl.BlockSpec((1, tm) + ROW_TILE, lambda b, i: (b, i, 0, 0)),
                   pl.BlockSpec((1, N_EXPERTS, tm), lambda b, i: (b, 0, i))],
        compiler_params=_cparams(2, VMEM_LIMIT),
        name="merge_out_router",
    )(yf, yb, xs, z, f, gates, x, g1, sh2, sc2, *consts)


def _to_tile_rows(v):
    return jnp.concatenate([v[:, k * LANES:(k + 1) * LANES] for k in range(v.shape[1] // LANES)], axis=0)


def _route_kernel(aff_ref, idx_ref, val_ref, *, cap):
    I32 = jnp.int32
    a = aff_ref[0]
    ne, length = a.shape
    nt = length // LANES
    rows = nt * ne

    def search(i, t):
        cand = t | lax.shift_left(jnp.int32(1), 30 - i)
        cnt = jnp.sum((a >= pltpu.bitcast(cand, F32)).astype(F32), axis=1, keepdims=True)
        return jnp.where(cnt >= cap, cand, t)

    thr = pltpu.bitcast(lax.fori_loop(0, 31, search, jnp.zeros((ne, 1), I32)), F32)
    n_gt = jnp.sum((a > thr).astype(F32), axis=1, keepdims=True)
    need = cap - n_gt

    val_r = _to_tile_rows(a)
    tile_rows = lambda v: jnp.concatenate([v] * nt, axis=0)
    thr_r = tile_rows(thr)
    need_r = tile_rows(need)
    li = lax.broadcasted_iota(I32, (LANES, LANES), 0)
    lj = lax.broadcasted_iota(I32, (LANES, LANES), 1)
    upper = (li < lj).astype(BF16)
    ri = lax.broadcasted_iota(I32, (rows, rows), 0)
    rj = lax.broadcasted_iota(I32, (rows, rows), 1)
    earlier = (((ri & (ne - 1)) == (rj & (ne - 1))) & (rj < ri)).astype(BF16)

    def prefix(mask):
        local = jnp.dot(mask.astype(BF16), upper, preferred_element_type=F32)
        total = jnp.sum(mask.astype(F32), axis=1, keepdims=True)
        tot_b = jnp.broadcast_to(total, (rows, LANES)).astype(BF16)
        offs = jnp.dot(earlier, tot_b, preferred_element_type=F32)[:, 0:1]
        return local, offs

    eq = val_r == thr_r
    eq_local, eq_off = prefix(eq)
    sel = (val_r > thr_r) | (eq & (eq_local + eq_off < need_r))
    sel_local, sel_off = prefix(sel)

    lane = lax.broadcasted_iota(I32, (rows, LANES), 1)
    k_row = lax.shift_right_logical(lax.broadcasted_iota(I32, (rows, LANES), 0), int(np.log2(ne)))
    tok = jnp.where(sel, k_row * LANES + lane, -1)
    val = val_r
    dist = jnp.where(sel, lane - sel_local.astype(I32), 0)
    for s in range(int(np.log2(LANES))):
        sh = 1 << s
        mv = (tok >= 0) & ((lax.shift_right_logical(dist, s) & 1) == 1)
        in_tok = pltpu.roll(jnp.where(mv, tok, -1), LANES - sh, axis=1)
        in_val = pltpu.roll(val, LANES - sh, axis=1)
        in_dist = pltpu.roll(dist, LANES - sh, axis=1)
        arrive = in_tok >= 0
        stay = (tok >= 0) & jnp.logical_not(mv)
        tok = jnp.where(arrive, in_tok, jnp.where(stay, tok, -1))
        val = jnp.where(arrive, in_val, val)
        dist = jnp.where(arrive, in_dist, dist)

    off = sel_off.astype(I32)
    lane_e = lax.broadcasted_iota(I32, (ne, LANES), 1)
    ncol = cap // LANES
    out_tok = [jnp.zeros((ne, LANES), I32) for _ in range(ncol)]
    out_val = [jnp.zeros((ne, LANES), F32) for _ in range(ncol)]
    for k in range(nt):
        rs = slice(k * ne, (k + 1) * ne)
        o = off[rs]
        sh, col = o & (LANES - 1), lax.shift_right_logical(o, int(np.log2(LANES)))
        pat = (lane_e - sh) & (LANES - 1)
        r_tok = jnp.take_along_axis(tok[rs], pat, axis=1)
        r_val = jnp.take_along_axis(val[rs], pat, axis=1)
        ok = r_tok >= 0
        for c in range(ncol):
            here = ok & (((col == c) & (lane_e >= sh)) | ((col + 1 == c) & (lane_e < sh)))
            out_tok[c] = jnp.where(here, r_tok, out_tok[c])
            out_val[c] = jnp.where(here, r_val, out_val[c])
    idx_ref[0] = jnp.concatenate(out_tok, axis=1)
    val_ref[0] = jnp.concatenate(out_val, axis=1)


def _route(aff_t, cap):
    bn, ne, length = aff_t.shape
    assert ne & (ne - 1) == 0 and cap % LANES == 0 and length % LANES == 0
    return pl.pallas_call(
        functools.partial(_route_kernel, cap=cap),
        out_shape=[jax.ShapeDtypeStruct((bn, ne, cap), jnp.int32), jax.ShapeDtypeStruct((bn, ne, cap), F32)],
        grid=(bn,),
        in_specs=[pl.BlockSpec((1, ne, length), lambda b: (b, 0, 0))],
        out_specs=[pl.BlockSpec((1, ne, cap), lambda b: (b, 0, 0))] * 2,
        compiler_params=_cparams(1),
        name="expert_choice_route",
    )(aff_t)


def _expert_kernel(idx_ref, idxn_ref, hm_ref, wg_ref, wu_ref, wd_ref, ye_ref, buf0, buf1, wgb, wub, wdb, sem,
                   *, cap, nb, nsteps):
    s = pl.program_id(0)
    bufs = (buf0, buf1)
    dot = functools.partial(jnp.dot, preferred_element_type=F32)

    def row_copy(k, j, b, r):
        return pltpu.make_async_copy(hm_ref.at[b, pl.ds(r, 1)], bufs[k].at[pl.ds(j, 1)], sem.at[k])

    def wait_rows(k):
        pltpu.make_async_copy(hm_ref.at[0, pl.ds(0, cap)], bufs[k], sem.at[k]).wait()

    b0 = (2 * s) % nb

    @pl.when(b0 == 0)
    def _():
        wgb[...] = wg_ref[0].astype(BF16)
        wub[...] = wu_ref[0].astype(BF16)
        wdb[...] = wd_ref[0].astype(BF16)

    @pl.when(s == 0)
    def _():
        def issue(j, carry):
            row_copy(0, j, b0, idx_ref[0, 0, j]).start()
            return carry
        lax.fori_loop(0, cap, issue, 0)

    for k in range(2):
        wait_rows(k)
        x = _from_row_tiles(bufs[k][...]).astype(BF16)
        if k == 0:
            for j in range(cap):
                row_copy(1, j, b0 + 1, idx_ref[1, 0, j]).start(priority=j % 2)
            for j in range(cap):
                row_copy(0, j, (b0 + 2) % nb, idxn_ref[0, 0, j]).start(priority=j % 2)
        hid = (_silu(dot(x, wgb[...])) * dot(x, wub[...])).astype(BF16)
        ye_ref[k] = _to_row_tiles(dot(hid, wdb[...])).astype(BF16).reshape((cap // 2,) + PAIR_TILE)

    @pl.when(s == nsteps - 1)
    def _():
        wait_rows(0)


def _experts(idx, hm, wg, wu, wd):
    bn = hm.shape[0]
    ne, d, ff = wg.shape
    cap = idx.shape[-1]
    per = bn // 2
    nsteps = ne * per
    cur = lambda s: (s, 0, 0)
    nxt = lambda s: ((s + 1) % nsteps, 0, 0)
    wspec = pl.BlockSpec((1, d, ff), lambda s: (s // per, 0, 0))
    return pl.pallas_call(
        functools.partial(_expert_kernel, cap=cap, nb=bn, nsteps=nsteps),
        out_shape=jax.ShapeDtypeStruct((ne * bn, cap // 2) + PAIR_TILE, BF16),
        grid=(nsteps,),
        in_specs=[pl.BlockSpec((2, 1, cap), cur, memory_space=pltpu.SMEM),
                  pl.BlockSpec((2, 1, cap), nxt, memory_space=pltpu.SMEM),
                  pl.BlockSpec(memory_space=pl.ANY),
                  wspec, wspec, pl.BlockSpec((1, ff, d), lambda s: (s // per, 0, 0))],
        out_specs=pl.BlockSpec((2, cap // 2) + PAIR_TILE, lambda s: (s, 0, 0, 0)),
        scratch_shapes=[pltpu.VMEM((cap,) + ROW_TILE, F32), pltpu.VMEM((cap,) + ROW_TILE, F32),
                        pltpu.VMEM((d, ff), BF16), pltpu.VMEM((d, ff), BF16), pltpu.VMEM((ff, d), BF16),
                        pltpu.SemaphoreType.DMA((2,))],
        compiler_params=_cparams(1, VMEM_LIMIT),
        name="expert_ffn",
    )(idx, idx, hm, wg, wu, wd)


COMBINE_ROWS = 16


COMBINE_EXPERTS = 2


def _combine_kernel(*refs, cap, nb, rows_out):
    n = COMBINE_EXPERTS
    idx_refs, val_refs, ye_refs = refs[:n], refs[n:2 * n], refs[2 * n:3 * n]
    hx_ref, g2_ref, w_ref, o_ref, acc = refs[3 * n:]
    b = pl.program_id(0)
    e2 = pl.program_id(1)
    slot = b % 2
    acc_cur = acc.at[slot]

    @pl.when(jnp.logical_and(e2 == 0, b < nb))
    def _():
        acc_cur[...] = jnp.zeros(acc.shape[1:], F32)

    @pl.when(b < nb)
    def _():
        for idx_ref, val_ref, ye_ref in zip(idx_refs, val_refs, ye_refs):
            def body(i, carry, idx_ref=idx_ref, val_ref=val_ref, ye_ref=ye_ref):
                j0 = i * COMBINE_ROWS
                rows = [idx_ref[0, 0, j0 + u] for u in range(COMBINE_ROWS)]
                pairs = [ye_ref[0, i * (COMBINE_ROWS // 2) + p].astype(F32) for p in range(COMBINE_ROWS // 2)]
                ye = [pairs[u // 2][(u % 2) * ROW_TILE[0]:(u % 2 + 1) * ROW_TILE[0]] for u in range(COMBINE_ROWS)]
                new = [acc_cur[rows[u]] + ye[u] * val_ref[0, 0, j0 + u] for u in range(COMBINE_ROWS)]
                for u in range(COMBINE_ROWS):
                    acc_cur[rows[u]] = new[u]
                return carry

            lax.fori_loop(0, cap // COMBINE_ROWS, body, 0)

    @pl.when(b > 0)
    def _():
        f3 = acc[1 - slot, pl.ds(pl.multiple_of(e2 * rows_out, rows_out), rows_out)]
        ms = jnp.mean(f3 * f3, axis=(1, 2), keepdims=True)
        y = _from_row_tiles(f3 * lax.rsqrt(ms + RMS_EPS) * w_ref[...])
        o_ref[0] = hx_ref[0] + g2_ref[0] * y


def _combine_residual(idx, vals, ye, hx, g2, w):
    bn, length, d = hx.shape
    n = COMBINE_EXPERTS
    steps = N_EXPERTS // n
    cap = idx.shape[-1]
    rows_out = length // steps
    w3 = w.reshape((1,) + ROW_TILE)
    cur = lambda b: jnp.minimum(b, bn - 1)
    prev = lambda b: jnp.maximum(b - 1, 0)
    item = lambda k: (lambda b, e2: ((e2 * n + k) * bn + cur(b), 0, 0))
    item4 = lambda k: (lambda b, e2: ((e2 * n + k) * bn + cur(b), 0, 0, 0))
    smem = [pl.BlockSpec((1, 1, cap), item(k), memory_space=pltpu.SMEM) for k in range(n)]
    tok_prev = pl.BlockSpec((1, rows_out, d), lambda b, e2: (prev(b), jnp.where(b > 0, e2, 0), 0))
    return pl.pallas_call(
        functools.partial(_combine_kernel, cap=cap, nb=bn, rows_out=rows_out),
        out_shape=jax.ShapeDtypeStruct((bn, length, d), F32),
        grid=(bn + 1, steps),
        in_specs=smem + smem + [pl.BlockSpec((1, cap // 2) + PAIR_TILE, item4(k)) for k in range(n)]
                 + [tok_prev, pl.BlockSpec((1, 1, d), lambda b, e2: (prev(b), 0, 0)), _const_spec(w3.shape)],
        out_specs=tok_prev,
        scratch_shapes=[pltpu.VMEM((2, length) + ROW_TILE, F32)],
        compiler_params=_cparams(2, VMEM_LIMIT),
        name="expert_combine_residual",
    )(*([idx] * n + [vals] * n + [ye] * n), hx, g2, w3)


def kernel(x, c, ctx, c_ctx, w_mod, b_mod, norm_mix_pre, norm_mix_post, norm_ffn_pre, norm_ffn_post,
           w_in, conv_w, conv_b, dt_bias, a_log, d_skip, ssd_norm, w_fourier, w_ssd_out, b_gate, w_out,
           w_router, w_e_gate, w_e_up, w_e_down):
    bn, length, d = x.shape
    ctx_len = ctx.shape[1]
    l = 0
    row = lambda v: v.reshape(1, -1)

    cs = jnp.concatenate([c, c_ctx[None], jnp.zeros((7, d), F32)], axis=0)
    mods = _modulation(cs, w_mod[l], b_mod[l])
    mod_x = mods[:bn].reshape(bn, N_MOD, 1, d)
    mod_c = jnp.broadcast_to(mods[bn].reshape(1, N_MOD, 1, d), (bn, N_MOD, 1, d))
    sh1, sc1, g1, sh2, sc2, g2 = [mod_x[:, i] for i in range(N_MOD)]

    i1 = F_WIDTH
    i2 = i1 + D_INNER
    i3 = i2 + D_INNER
    i4 = i3 + BC_WIDTH
    i5 = i4 + BC_WIDTH
    i6 = i5 + 2 * N_HEADS
    wi = w_in[l]
    wdt = jnp.zeros((d, 256), F32).at[:, :N_HEADS].set(wi[:, i5:i5 + N_HEADS])
    wdt = wdt.at[:, 128:128 + N_HEADS].set(wi[:, i5 + N_HEADS:i6])
    dtb = jnp.zeros((1, 256), F32).at[0, :N_HEADS].set(dt_bias[l, 0]).at[0, 128:128 + N_HEADS].set(dt_bias[l, 1])
    dft_c, w1, tw_c, tw_s, w2 = _dft_constants()
    cw, cb = conv_w[l], conv_b[l]
    j1, j2 = D_INNER, D_INNER + BC_WIDTH
    w = dict(
        wuf=wi[:, :i1].astype(BF16), wz=wi[:, i1:i2].astype(BF16), wxs=wi[:, i2:i3].astype(BF16),
        wb=wi[:, i3:i4].astype(BF16), wc=wi[:, i4:i5].astype(BF16), wdt=wdt.astype(BF16),
        wg=wi[:, i6:].astype(BF16), dft=dft_c,
        cwx=cw[:, :j1], cbx=row(cb[:j1]), cwb=cw[:, j1:j2], cbb=row(cb[j1:j2]),
        cwc=cw[:, j2:], cbc=row(cb[j2:]), dtb=dtb, bg=row(b_gate[l]))
    gain_pre = row(norm_mix_pre[l])

    alog_pad = jnp.zeros((2, 1, 128), F32).at[:, 0, :N_HEADS].set(a_log[l])

    cxs, cbm, ccm, cdt = _in_projection(ctx, mod_c[:, 0], mod_c[:, 1], gain_pre, w, ctx_len, ctx_len, False)
    h_zero = jnp.zeros((bn, N_GROUPS, D_STATE, GROUP_COLS), F32)
    hc_f, hc_b = _ssd_scan(cxs, cbm, ccm, cdt, alog_pad, h_zero, h_zero, emit_y=False)

    zr, zi, z, xs, bm, cm, dt, gates = _in_projection(x, sh1, sc1, gain_pre, w, GRID_W, INPROJ_TILE, True)
    f = _sequence_dft_real(zr, zi, w1, tw_c, tw_s, w2)
    yf, _, yb, _ = _ssd_scan(xs, bm, cm, dt, alog_pad, hc_f, hc_b)

    consts = [row(jnp.repeat(d_skip[l], HEAD_DIM)), row(ssd_norm[l]), row(norm_mix_post[l]),
              row(norm_ffn_pre[l]), w_ssd_out[l].astype(BF16), w_fourier[l].astype(BF16),
              w_out[l].astype(BF16), jnp.pad(w_router[l], ((0, 0), (0, LANES - N_EXPERTS)))]
    hx, hm, aff_t = _mix(yf, yb, xs, z, f, gates, x, g1, sh2, sc2, consts, MIX_TILE)

    cap = EC_FACTOR * length // N_EXPERTS
    top_idx, top_aff = _route(aff_t, cap)
    idx = jnp.swapaxes(top_idx, 0, 1).reshape(N_EXPERTS * bn, 1, cap)
    vals = jnp.swapaxes(top_aff, 0, 1).reshape(N_EXPERTS * bn, 1, cap)
    ye = _experts(idx, hm, w_e_gate[l], w_e_up[l], w_e_down[l])
    return _combine_residual(idx, vals, ye, hx, g2, row(norm_ffn_post[l]))
```

```python
import functools

import numpy as np
import jax
import jax.numpy as jnp
from jax import lax
from jax.experimental import pallas as pl
from jax.experimental.pallas import tpu as pltpu

F32 = jnp.float32
BF16 = jnp.bfloat16
HIGHEST = lax.Precision.HIGHEST

D_MODEL = 1024
GRID_W = 64
F_GROUP_W = 128
F_WIDTH = 512
D_INNER = 1536
HEAD_DIM = 64
N_HEADS = 24
N_GROUPS = 4
HEADS_PER_GROUP = 6
D_STATE = 128
CHUNK = 128
BC_WIDTH = N_GROUPS * D_STATE
N_EXPERTS = 16
EC_FACTOR = 2
N_MOD = 6
RMS_EPS = 1e-6
GROUP_COLS = HEADS_PER_GROUP * HEAD_DIM

FFT_N1 = 128
FFT_N2 = 32
FFT_KB = 8

VMEM_LIMIT = 56 * 1024 * 1024
LANES = 128


def _cparams(n_axes, vmem=None):
    return pltpu.CompilerParams(dimension_semantics=("arbitrary",) * n_axes,
                                vmem_limit_bytes=vmem)


def _const_spec(shape):
    nd = len(shape)
    return pl.BlockSpec(shape, lambda *_: (0,) * nd, pipeline_mode=pl.Buffered(1))


def _rms(x, w):
    ms = jnp.mean(x * x, axis=-1, keepdims=True)
    return x * lax.rsqrt(ms + RMS_EPS) * w


def _sigmoid(x):
    return 1.0 / (1.0 + jnp.exp(-x))


def _silu(x):
    return x * _sigmoid(x)


ROW_TILE = (8, 128)
PAIR_TILE = (16, 128)


def _to_row_tiles(v):
    parts = [v[:, c * 128:(c + 1) * 128] for c in range(ROW_TILE[0])]
    return jnp.swapaxes(jnp.stack(parts, axis=0), 0, 1)


def _from_row_tiles(v3):
    t = jnp.swapaxes(v3, 0, 1)
    return jnp.concatenate([t[c] for c in range(ROW_TILE[0])], axis=1)


def _mod_kernel(c_ref, w_ref, b_ref, o_ref):
    c = c_ref[...]
    o_ref[...] = jnp.dot(_silu(c), w_ref[...], precision=HIGHEST,
                         preferred_element_type=F32) + b_ref[...]


def _modulation(cs, w_mod, b_mod):
    rows = cs.shape[0]
    n = w_mod.shape[1]
    blk = D_MODEL
    return pl.pallas_call(
        _mod_kernel,
        out_shape=jax.ShapeDtypeStruct((rows, n), F32),
        grid=(n // blk,),
        in_specs=[pl.BlockSpec((rows, D_MODEL), lambda j: (0, 0)),
                  pl.BlockSpec((D_MODEL, blk), lambda j: (0, j)),
                  pl.BlockSpec((1, blk), lambda j: (0, j))],
        out_specs=pl.BlockSpec((rows, blk), lambda j: (0, j)),
        compiler_params=_cparams(1),
        name="modulation",
    )(cs, w_mod, b_mod.reshape(1, n))


def _conv_silu(xbc, cw_ref, cb_ref, row_len):
    tm = xbc.shape[0]
    pos = lax.broadcasted_iota(jnp.int32, (tm, 1), 0) % row_len
    prev = jnp.where(pos == 0, 0.0, pltpu.roll(xbc, 1, axis=0))
    nxt = jnp.where(pos == row_len - 1, 0.0, pltpu.roll(xbc, tm - 1, axis=0))
    out = prev * cw_ref[0:1, :] + xbc * cw_ref[1:2, :] + nxt * cw_ref[2:3, :] + cb_ref[...]
    return _silu(out)


INPROJ_SUB = 128
INPROJ_TILE = 512


def _softplus(x):
    return jnp.maximum(x, 0.0) + jnp.log(1.0 + jnp.exp(-jnp.abs(x)))


def _inproj_kernel(x_ref, sh_ref, sc_ref, gain_ref, wuf_ref, wz_ref, wxs_ref, wb_ref, wc_ref,
                   wdt_ref, wg_ref, dft_ref, cwx_ref, cbx_ref, cwb_ref, cbb_ref, cwc_ref, cbc_ref,
                   dtb_ref, bg_ref,
                   zr_ref, zi_ref, z_ref, xs_ref, bm_ref, cm_ref, dt_ref, g_ref, *, row_len):
    dot = functools.partial(jnp.dot, preferred_element_type=F32)
    tm = x_ref.shape[1]
    sub = min(tm, max(INPROJ_SUB, row_len))
    assert sub % row_len == 0 and tm % sub == 0
    for r0 in range(0, tm, sub):
        rs = slice(r0, r0 + sub)
        h = _rms(x_ref[0, rs], gain_ref[...]) * (1.0 + sc_ref[0]) + sh_ref[0]
        hb = h.astype(BF16)
        xs_ref[0, rs] = _conv_silu(dot(hb, wxs_ref[...]), cwx_ref, cbx_ref, row_len).astype(BF16)
        bm_ref[0, rs] = _conv_silu(dot(hb, wb_ref[...]), cwb_ref, cbb_ref, row_len).astype(BF16)
        cm_ref[0, rs] = _conv_silu(dot(hb, wc_ref[...]), cwc_ref, cbc_ref, row_len).astype(BF16)
        dt_ref[0, rs] = _softplus(dot(hb, wdt_ref[...]) + dtb_ref[...])
        if zr_ref is None:
            continue
        uf = dot(hb, wuf_ref[...]).astype(BF16)
        for g in range(F_WIDTH // F_GROUP_W):
            sl = slice(g * F_GROUP_W, (g + 1) * F_GROUP_W)
            zz = dot(uf[:, sl], dft_ref[...].astype(BF16))
            zr_ref[0, rs, sl] = zz[:, :F_GROUP_W].astype(BF16)
            zi_ref[0, rs, sl] = zz[:, F_GROUP_W:].astype(BF16)
        z_ref[0, rs] = dot(hb, wz_ref[...]).astype(BF16)
        g_ref[0, rs] = _sigmoid(dot(hb, wg_ref[...]) + bg_ref[...]).astype(BF16)


def _inproj_ctx_kernel(x_ref, sh_ref, sc_ref, gain_ref, wxs_ref, wb_ref, wc_ref, wdt_ref,
                       cwx_ref, cbx_ref, cwb_ref, cbb_ref, cwc_ref, cbc_ref, dtb_ref,
                       xs_ref, bm_ref, cm_ref, dt_ref, *, row_len):
    _inproj_kernel(x_ref, sh_ref, sc_ref, gain_ref, None, None, wxs_ref, wb_ref, wc_ref,
                   wdt_ref, None, None, cwx_ref, cbx_ref, cwb_ref, cbb_ref, cwc_ref, cbc_ref,
                   dtb_ref, None, None, None, None, xs_ref, bm_ref, cm_ref, dt_ref, None,
                   row_len=row_len)


def _in_projection(x, shift, scale, gain, w, row_len, tm, full):
    bn, length, d = x.shape
    grid = (bn, length // tm)
    tok = lambda width: pl.BlockSpec((1, tm, width), lambda b, i: (b, i, 0))
    vec = pl.BlockSpec((1, 1, d), lambda b, i: (b, 0, 0))
    out = lambda width, dt: jax.ShapeDtypeStruct((bn, length, width), dt)
    conv_ops = [w["cwx"], w["cbx"], w["cwb"], w["cbb"], w["cwc"], w["cbc"]]
    ssd_shapes = [out(D_INNER, BF16), out(BC_WIDTH, BF16), out(BC_WIDTH, BF16), out(256, F32)]
    ssd_specs = [tok(D_INNER), tok(BC_WIDTH), tok(BC_WIDTH), tok(256)]
    if full:
        ops = [x, shift, scale, gain, w["wuf"], w["wz"], w["wxs"], w["wb"], w["wc"], w["wdt"], w["wg"],
               w["dft"]] + conv_ops + [w["dtb"], w["bg"]]
        kern = functools.partial(_inproj_kernel, row_len=row_len)
        out_shape = [out(F_WIDTH, BF16), out(F_WIDTH, BF16), out(D_INNER, BF16)] + ssd_shapes + \
                    [out(2 * D_MODEL, BF16)]
        out_specs = [tok(F_WIDTH), tok(F_WIDTH), tok(D_INNER)] + ssd_specs + [tok(2 * D_MODEL)]
        name = "in_projection"
    else:
        ops = [x, shift, scale, gain, w["wxs"], w["wb"], w["wc"], w["wdt"]] + conv_ops + [w["dtb"]]
        kern = functools.partial(_inproj_ctx_kernel, row_len=row_len)
        out_shape = ssd_shapes
        out_specs = ssd_specs
        name = "in_projection_ctx"
    in_specs = [tok(d), vec, vec] + [_const_spec(o.shape) for o in ops[3:]]
    return pl.pallas_call(kern, out_shape=out_shape, grid=grid, in_specs=in_specs,
                          out_specs=out_specs, compiler_params=_cparams(2, VMEM_LIMIT),
                          name=name)(*ops)


def _dft_constants():
    j = np.arange(F_GROUP_W)
    ang = 2.0 * np.pi * np.outer(j, j) / F_GROUP_W
    dft_c = np.concatenate([np.cos(ang), -np.sin(ang)], axis=1)
    k1 = np.arange(FFT_N1)
    n2 = np.arange(FFT_N2)
    phi = 2.0 * np.pi * k1[None, :, None] * (FFT_N2 * k1[None, None, :] + n2[:, None, None]) / (FFT_N1 * FFT_N2)
    c1, s1 = np.cos(phi), np.sin(phi)
    w1 = np.concatenate([np.concatenate([c1, s1], axis=2), np.concatenate([-s1, c1], axis=2)], axis=1)
    a2 = 2.0 * np.pi * np.outer(n2, n2) / FFT_N2
    w2 = np.zeros((FFT_N2, FFT_KB, 2, FFT_KB, FFT_N2))
    for jj in range(FFT_KB):
        w2[:, jj, 0, jj, :] = np.cos(a2)
        w2[:, jj, 1, jj, :] = np.sin(a2)
    w2 = w2.reshape(FFT_N2 * FFT_KB, 2 * FFT_KB * FFT_N2)
    return jnp.asarray(dft_c, F32), jnp.asarray(w1, F32), jnp.asarray(w2, F32)


FFT_QB = 16


def _seq_dft_kernel(zr_ref, zi_ref, w1_ref, w2_ref, o_ref, qr_s, qi_s, tr_s, ti_s, *, scale):
    for qb in range(FFT_N2 // FFT_QB):
        qs = slice(qb * FFT_QB, (qb + 1) * FFT_QB)
        zr_t = jnp.swapaxes(zr_ref[0, :, qs, :], 0, 1)
        zi_t = jnp.swapaxes(zi_ref[0, :, qs, :], 0, 1)
        for q in range(FFT_QB):
            n2 = qb * FFT_QB + q
            rhs = jnp.concatenate([zr_t[q], zi_t[q]], axis=0)
            t = jnp.dot(w1_ref[n2].astype(BF16), rhs, preferred_element_type=F32)
            qr_s[q] = t[:FFT_N1].astype(BF16)
            qi_s[q] = t[FFT_N1:].astype(BF16)
        tr_s[:, qs, :] = jnp.swapaxes(qr_s[...], 0, 1)
        ti_s[:, qs, :] = jnp.swapaxes(qi_s[...], 0, 1)
    w2 = w2_ref[...].astype(BF16)
    rows = FFT_KB * FFT_N2
    for kb in range(FFT_N1 // FFT_KB):
        ks = slice(kb * FFT_KB, (kb + 1) * FFT_KB)
        rhs = jnp.concatenate([tr_s[ks].reshape(rows, F_WIDTH), ti_s[ks].reshape(rows, F_WIDTH)], axis=0)
        y = jnp.dot(w2, rhs, preferred_element_type=F32) * scale
        o_ref[0, :, ks, :] = y.reshape(FFT_N2, FFT_KB, F_WIDTH)


def _sequence_dft_real(zr, zi, w1, w2):
    bn, length, width = zr.shape
    blk = pl.BlockSpec((1, FFT_N1, FFT_N2, width), lambda b: (b, 0, 0, 0))
    scale = 1.0 / float(np.sqrt(length * F_GROUP_W))
    f = pl.pallas_call(
        functools.partial(_seq_dft_kernel, scale=scale),
        out_shape=jax.ShapeDtypeStruct((bn, FFT_N2, FFT_N1, width), F32),
        grid=(bn,),
        in_specs=[blk, blk, _const_spec(w1.shape), _const_spec(w2.shape)],
        out_specs=pl.BlockSpec((1, FFT_N2, FFT_N1, width), lambda b: (b, 0, 0, 0)),
        scratch_shapes=[pltpu.VMEM((FFT_QB, FFT_N1, width), BF16)] * 2
                       + [pltpu.VMEM((FFT_N1, FFT_N2, width), BF16)] * 2,
        compiler_params=_cparams(1, VMEM_LIMIT),
        name="seq_dft",
    )(zr.reshape(bn, FFT_N1, FFT_N2, width), zi.reshape(bn, FFT_N1, FFT_N2, width), w1, w2)
    return f.reshape(bn, length, width)


SSD_SUB = 64
SSD_CHUNKS_PER_STEP = 4
LOG2E = 1.4426950408889634


class _Dir:
    pass


def _ssd_prepare(dt_ref, alog_ref, reverse, row0):
    q, sb = CHUNK, SSD_SUB
    d = _Dir()
    d.reverse = reverse
    r_i = lax.broadcasted_iota(jnp.int32, (q, q), 0)
    c_i = lax.broadcasted_iota(jnp.int32, (q, q), 1)
    tri = ((r_i <= c_i) if reverse else (r_i >= c_i)).astype(BF16)
    a = -jnp.exp(alog_ref[0])
    d.row0 = row0
    d.dt = dt_ref[0, row0:row0 + q]
    da = d.dt * (a * LOG2E)
    p0 = da.astype(BF16)
    r1 = da - p0.astype(F32)
    p1 = r1.astype(BF16)
    p2 = (r1 - p1.astype(F32)).astype(BF16)
    parts = jnp.dot(tri, jnp.concatenate([p0, p1, p2], axis=1), preferred_element_type=F32)
    d.acum = parts[:, :LANES] + (parts[:, LANES:2 * LANES] + parts[:, 2 * LANES:])
    lane = lax.broadcasted_iota(jnp.int32, (sb, 128), 1)
    d.lo = lane < HEAD_DIM
    lo_h = lax.broadcasted_iota(jnp.int32, (q, 128), 1) < HEAD_DIM
    row_l = lax.broadcasted_iota(jnp.int32, (sb, 128), 0)
    s_l = jnp.where(d.lo, lane, lane - HEAD_DIM)
    d.mask = (s_l >= row_l) if reverse else (s_l <= row_l)
    d.order = (1, 0) if reverse else (0, 1)

    def halves(m):
        m_t = m.T
        m_sw = pltpu.roll(m_t, sb, axis=1)
        return (jnp.where(lo_h, m_t, m_sw), jnp.where(lo_h, m_sw, m_t))

    d.rt = halves(d.acum)
    d.rt_dt = halves(d.dt)
    return d


def _ssd_block_rows(d, step):
    sb = SSD_SUB
    blk = d.order[step]
    end_r = blk * sb if d.reverse else blk * sb + sb - 1
    end_row = d.acum[end_r:end_r + 1, :]
    if step == 0:
        base_row = jnp.zeros((1, 128), F32)
    else:
        pr = d.order[0] * sb if d.reverse else d.order[0] * sb + sb - 1
        base_row = d.acum[pr:pr + 1, :]
    d.rows8 = jnp.concatenate([base_row, jnp.exp2(end_row - base_row), jnp.zeros((6, 128), F32)], axis=0)
    local = slice(blk * sb, (blk + 1) * sb)
    d.rs = slice(d.row0 + blk * sb, d.row0 + (blk + 1) * sb)
    d.blk = blk
    d.acum_b = d.acum[local]
    d.dw_b = d.dt[local] * jnp.exp2(end_row - d.acum_b)


def _ssd_group_begin(d, refs, g):
    xs_ref, bm_ref, cm_ref, y_ref, st_ref = refs
    gs = slice(g * D_STATE, (g + 1) * D_STATE)
    bg = bm_ref[0, d.rs, gs]
    cg = cm_ref[0, d.rs, gs]
    bg_t = bg.astype(F32).T.astype(BF16)
    h_t = st_ref[g]
    d.grp = dict(bg_t=bg_t, h_t=h_t, xw=[], cd=[])
    if y_ref is not None:
        d.grp["cb2"] = jnp.dot(cg, jnp.concatenate([bg_t, bg_t], axis=1), preferred_element_type=F32)
        d.grp["yoff"] = jnp.dot(cg, h_t.astype(BF16), preferred_element_type=F32)


def _ssd_pair(d, refs, g, pr_i):
    xs_ref, bm_ref, cm_ref, y_ref, st_ref = refs
    lo = d.lo
    lo_row = lo[0:1]
    h0 = g * HEADS_PER_GROUP + 2 * pr_i
    h1 = h0 + 1
    ps = slice((g * 3 + pr_i) * 128, (g * 3 + pr_i + 1) * 128)
    pat = jnp.where(lo, h0, h1)
    dwp = jnp.take_along_axis(d.dw_b, pat, axis=1)
    r8 = jnp.take_along_axis(d.rows8, pat[0:8], axis=1)
    base_p, cd_p = r8[0:1], r8[1:2]
    x = xs_ref[0, d.rs, ps]
    gd = d.grp
    gd["xw"].append((x.astype(F32) * dwp).astype(BF16))
    gd["cd"].append(cd_p)
    if y_ref is None:
        return
    colp = jnp.take_along_axis(d.acum_b, pat, axis=1)
    rt, rt_dt = d.rt[d.blk], d.rt_dt[d.blk]
    rowp = jnp.where(lo_row, rt[h0:h0 + 1, :], rt[h1:h1 + 1, :])
    dt_row = jnp.where(lo_row, rt_dt[h0:h0 + 1, :], rt_dt[h1:h1 + 1, :])
    m = (jnp.exp2(jnp.where(d.mask, colp - rowp, -1e30)) * (gd["cb2"] * dt_row)).astype(BF16)
    zero = jnp.zeros_like(x)
    rhs = jnp.concatenate([jnp.where(lo, x, zero), jnp.where(lo, zero, x)], axis=0)
    ydiag = jnp.dot(m, rhs, preferred_element_type=F32)
    y = ydiag + gd["yoff"][:, pr_i * 128:(pr_i + 1) * 128] * jnp.exp2(colp - base_p)
    y_ref[0, d.rs, ps] = y.astype(BF16)


def _ssd_group_end(d, refs, g):
    st_ref = refs[4]
    gd = d.grp
    xw = jnp.concatenate(gd["xw"], axis=1)
    cd = jnp.concatenate(gd["cd"], axis=1)
    st_ref[g] = gd["h_t"] * cd + jnp.dot(gd["bg_t"], xw, preferred_element_type=F32)


def _ssd_kernel(xf, bf, cf, df, af, hf, xb, bb, cb_, db, ab, hb, *out_and_scratch, nchunks, emit_y):
    if emit_y:
        yf, hff, yb, hfb, stf, stb = out_and_scratch
    else:
        (hff, hfb, stf, stb), yf, yb = out_and_scratch, None, None
    c = pl.program_id(1)

    @pl.when(c == 0)
    def _():
        stf[...] = hf[0]
        stb[...] = hb[0]

    refs = ((xf, bf, cf, yf, stf), (xb, bb, cb_, yb, stb))
    per_step = xf.shape[1] // CHUNK
    for ci in range(per_step):
        dirs = (_ssd_prepare(df, af, False, ci * CHUNK), _ssd_prepare(db, ab, True, (per_step - 1 - ci) * CHUNK))
        for step in range(CHUNK // SSD_SUB):
            for d in dirs:
                _ssd_block_rows(d, step)
            for g in range(N_GROUPS):
                for d, r in zip(dirs, refs):
                    _ssd_group_begin(d, r, g)
                for pr_i in range(HEADS_PER_GROUP // 2):
                    for d, r in zip(dirs, refs):
                        _ssd_pair(d, r, g, pr_i)
                for d, r in zip(dirs, refs):
                    _ssd_group_end(d, r, g)

    @pl.when(c == nchunks - 1)
    def _():
        hff[0] = stf[...]
        hfb[0] = stb[...]


def _ssd_scan(xs, bm, cm, dt, alog_pad, h0f, h0b, emit_y=True):
    bn, length, _ = xs.shape
    rows = CHUNK * min(SSD_CHUNKS_PER_STEP, length // CHUNK)
    nchunks = length // rows
    st_shape = (N_GROUPS, D_STATE, GROUP_COLS)
    st_spec = pl.BlockSpec((1,) + st_shape, lambda b, c: (b, 0, 0, 0))

    def specs(reverse):
        cc = (lambda c: nchunks - 1 - c) if reverse else (lambda c: c)
        di = 1 if reverse else 0
        tok = lambda width: pl.BlockSpec((1, rows, width), lambda b, c: (b, cc(c), 0))
        ins = [tok(D_INNER), tok(BC_WIDTH), tok(BC_WIDTH),
               pl.BlockSpec((1, rows, 128), lambda b, c: (b, cc(c), di)),
               pl.BlockSpec((1, 1, 128), lambda b, c: (di, 0, 0)), st_spec]
        return ins, ([tok(D_INNER)] if emit_y else []) + [st_spec]

    in_f, out_f = specs(False)
    in_b, out_b = specs(True)
    y_shape = [jax.ShapeDtypeStruct((bn, length, D_INNER), BF16)] if emit_y else []
    h_shape = [jax.ShapeDtypeStruct((bn,) + st_shape, F32)]
    return pl.pallas_call(
        functools.partial(_ssd_kernel, nchunks=nchunks, emit_y=emit_y),
        out_shape=y_shape + h_shape + y_shape + h_shape,
        grid=(bn, nchunks),
        in_specs=in_f + in_b,
        out_specs=out_f + out_b,
        scratch_shapes=[pltpu.VMEM(st_shape, F32), pltpu.VMEM(st_shape, F32)],
        compiler_params=_cparams(2),
        name="ssd_scan",
    )(xs, bm, cm, dt, alog_pad, h0f, xs, bm, cm, dt, alog_pad, h0b)


MIX_SUB = 128
MIX_TILE = 512


def _split_bf16(v):
    hi = v.astype(BF16)
    return hi, (v - hi.astype(F32)).astype(BF16)


def _mix_kernel(yf_ref, yb_ref, xs_ref, z_ref, f_ref, g_ref, x_ref, g1_ref, sh2_ref, sc2_ref,
                dsk_ref, nssd_ref, npost_ref, npre_ref, wso_ref, wf_ref, wo_ref, wr_ref,
                hx_ref, hm_ref, aff_ref):
    dot = functools.partial(jnp.dot, preferred_element_type=F32)
    wr_hi, wr_lo = _split_bf16(wr_ref[...])
    subs = [slice(r0, r0 + MIX_SUB) for r0 in range(0, x_ref.shape[1], MIX_SUB)]
    ys = []
    for rs in subs:
        y = (yf_ref[0, rs].astype(F32) + yb_ref[0, rs].astype(F32)
             + dsk_ref[...] * xs_ref[0, rs].astype(F32))
        ys.append(_rms(y * _silu(z_ref[0, rs].astype(F32)), nssd_ref[...]).astype(BF16))
    s_branches = [dot(y, wso_ref[...]) for y in ys]
    f_branches = [dot(f_ref[0, rs].astype(BF16), wf_ref[...]) for rs in subs]
    merged = []
    for rs, s_branch, f_branch in zip(subs, s_branches, f_branches):
        gates = g_ref[0, rs].astype(F32)
        merged.append((gates[:, :D_MODEL] * f_branch + gates[:, D_MODEL:] * s_branch).astype(BF16))
    mixes = [dot(m, wo_ref[...]) for m in merged]
    hms = []
    for rs, mix in zip(subs, mixes):
        hx = x_ref[0, rs] + g1_ref[0] * _rms(mix, npost_ref[...])
        hx_ref[0, rs] = hx
        hm = _rms(hx, npre_ref[...]) * (1.0 + sc2_ref[0]) + sh2_ref[0]
        hm_ref[0, rs] = _to_row_tiles(hm)
        hms.append(hm)
    for rs, hm in zip(subs, hms):
        hm_hi, hm_lo = _split_bf16(hm)
        logits = dot(hm_hi, wr_hi) + (dot(hm_hi, wr_lo) + dot(hm_lo, wr_hi))
        real = lax.broadcasted_iota(jnp.int32, logits.shape, 1) < N_EXPERTS
        logits = jnp.where(real, logits, -1e30)
        e = jnp.exp(logits - jnp.max(logits, axis=-1, keepdims=True))
        aff = e / jnp.sum(e, axis=-1, keepdims=True)
        aff_ref[0, :, rs] = aff.T[:N_EXPERTS]


def _mix(yf, yb, xs, z, f, gates, x, g1, sh2, sc2, consts, tm):
    bn, length, d = x.shape
    tok = lambda width: pl.BlockSpec((1, tm, width), lambda b, i: (b, i, 0))
    vec = pl.BlockSpec((1, 1, d), lambda b, i: (b, 0, 0))
    return pl.pallas_call(
        _mix_kernel,
        out_shape=[jax.ShapeDtypeStruct((bn, length, d), F32),
                   jax.ShapeDtypeStruct((bn, length) + ROW_TILE, F32),
                   jax.ShapeDtypeStruct((bn, N_EXPERTS, length), F32)],
        grid=(bn, length // tm),
        in_specs=[tok(D_INNER), tok(D_INNER), tok(D_INNER), tok(D_INNER), tok(F_WIDTH), tok(2 * D_MODEL), tok(d),
                  vec, vec, vec] + [_const_spec(c.shape) for c in consts],
        out_specs=[tok(d), pl.BlockSpec((1, tm) + ROW_TILE, lambda b, i: (b, i, 0, 0)),
                   pl.BlockSpec((1, N_EXPERTS, tm), lambda b, i: (b, 0, i))],
        compiler_params=_cparams(2, VMEM_LIMIT),
        name="merge_out_router",
    )(yf, yb, xs, z, f, gates, x, g1, sh2, sc2, *consts)


def _to_tile_rows(v):
    return jnp.concatenate([v[:, k * LANES:(k + 1) * LANES] for k in range(v.shape[1] // LANES)], axis=0)


def _route_kernel(aff_ref, idx_ref, val_ref, *, cap):
    I32 = jnp.int32
    a = aff_ref[0]
    ne, length = a.shape
    nt = length // LANES
    rows = nt * ne

    def search(i, t):
        cand = t | lax.shift_left(jnp.int32(1), 30 - i)
        cnt = jnp.sum((a >= pltpu.bitcast(cand, F32)).astype(F32), axis=1, keepdims=True)
        return jnp.where(cnt >= cap, cand, t)

    thr = pltpu.bitcast(lax.fori_loop(0, 31, search, jnp.zeros((ne, 1), I32)), F32)
    n_gt = jnp.sum((a > thr).astype(F32), axis=1, keepdims=True)
    need = cap - n_gt

    val_r = _to_tile_rows(a)
    tile_rows = lambda v: jnp.concatenate([v] * nt, axis=0)
    thr_r = tile_rows(thr)
    need_r = tile_rows(need)
    li = lax.broadcasted_iota(I32, (LANES, LANES), 0)
    lj = lax.broadcasted_iota(I32, (LANES, LANES), 1)
    upper = (li < lj).astype(BF16)
    ri = lax.broadcasted_iota(I32, (rows, rows), 0)
    rj = lax.broadcasted_iota(I32, (rows, rows), 1)
    earlier = (((ri & (ne - 1)) == (rj & (ne - 1))) & (rj < ri)).astype(BF16)

    def prefix(mask):
        local = jnp.dot(mask.astype(BF16), upper, preferred_element_type=F32)
        total = jnp.sum(mask.astype(F32), axis=1, keepdims=True)
        tot_b = jnp.broadcast_to(total, (rows, LANES)).astype(BF16)
        offs = jnp.dot(earlier, tot_b, preferred_element_type=F32)[:, 0:1]
        return local, offs

    eq = val_r == thr_r
    eq_local, eq_off = prefix(eq)
    sel = (val_r > thr_r) | (eq & (eq_local + eq_off < need_r))
    sel_local, sel_off = prefix(sel)

    lane = lax.broadcasted_iota(I32, (rows, LANES), 1)
    k_row = lax.shift_right_logical(lax.broadcasted_iota(I32, (rows, LANES), 0), int(np.log2(ne)))
    tok = jnp.where(sel, k_row * LANES + lane, -1)
    val = val_r
    dist = jnp.where(sel, lane - sel_local.astype(I32), 0)
    for s in range(int(np.log2(LANES))):
        sh = 1 << s
        mv = (tok >= 0) & ((lax.shift_right_logical(dist, s) & 1) == 1)
        in_tok = pltpu.roll(jnp.where(mv, tok, -1), LANES - sh, axis=1)
        in_val = pltpu.roll(val, LANES - sh, axis=1)
        in_dist = pltpu.roll(dist, LANES - sh, axis=1)
        arrive = in_tok >= 0
        stay = (tok >= 0) & jnp.logical_not(mv)
        tok = jnp.where(arrive, in_tok, jnp.where(stay, tok, -1))
        val = jnp.where(arrive, in_val, val)
        dist = jnp.where(arrive, in_dist, dist)

    off = sel_off.astype(I32)
    lane_e = lax.broadcasted_iota(I32, (ne, LANES), 1)
    ncol = cap // LANES
    out_tok = [jnp.zeros((ne, LANES), I32) for _ in range(ncol)]
    out_val = [jnp.zeros((ne, LANES), F32) for _ in range(ncol)]
    for k in range(nt):
        rs = slice(k * ne, (k + 1) * ne)
        o = off[rs]
        sh, col = o & (LANES - 1), lax.shift_right_logical(o, int(np.log2(LANES)))
        pat = (lane_e - sh) & (LANES - 1)
        r_tok = jnp.take_along_axis(tok[rs], pat, axis=1)
        r_val = jnp.take_along_axis(val[rs], pat, axis=1)
        ok = r_tok >= 0
        for c in range(ncol):
            here = ok & (((col == c) & (lane_e >= sh)) | ((col + 1 == c) & (lane_e < sh)))
            out_tok[c] = jnp.where(here, r_tok, out_tok[c])
            out_val[c] = jnp.where(here, r_val, out_val[c])
    idx_ref[0] = jnp.concatenate(out_tok, axis=1)
    val_ref[0] = jnp.concatenate(out_val, axis=1)


def _route(aff_t, cap):
    bn, ne, length = aff_t.shape
    assert ne & (ne - 1) == 0 and cap % LANES == 0 and length % LANES == 0
    return pl.pallas_call(
        functools.partial(_route_kernel, cap=cap),
        out_shape=[jax.ShapeDtypeStruct((bn, ne, cap), jnp.int32), jax.ShapeDtypeStruct((bn, ne, cap), F32)],
        grid=(bn,),
        in_specs=[pl.BlockSpec((1, ne, length), lambda b: (b, 0, 0))],
        out_specs=[pl.BlockSpec((1, ne, cap), lambda b: (b, 0, 0))] * 2,
        compiler_params=_cparams(1),
        name="expert_choice_route",
    )(aff_t)


def _expert_kernel(idx_ref, idxn_ref, hm_ref, wg_ref, wu_ref, wd_ref, ye_ref, buf0, buf1, wgb, wub, wdb, sem,
                   *, cap, nb, nsteps):
    s = pl.program_id(0)
    bufs = (buf0, buf1)
    dot = functools.partial(jnp.dot, preferred_element_type=F32)

    def row_copy(k, j, b, r):
        return pltpu.make_async_copy(hm_ref.at[b, pl.ds(r, 1)], bufs[k].at[pl.ds(j, 1)], sem.at[k])

    def wait_rows(k):
        pltpu.make_async_copy(hm_ref.at[0, pl.ds(0, cap)], bufs[k], sem.at[k]).wait()

    b0 = (2 * s) % nb

    @pl.when(b0 == 0)
    def _():
        wgb[...] = wg_ref[0].astype(BF16)
        wub[...] = wu_ref[0].astype(BF16)
        wdb[...] = wd_ref[0].astype(BF16)

    @pl.when(s == 0)
    def _():
        def issue(j, carry):
            row_copy(0, j, b0, idx_ref[0, 0, j]).start()
            return carry
        lax.fori_loop(0, cap, issue, 0)

    for k in range(2):
        wait_rows(k)
        x = _from_row_tiles(bufs[k][...]).astype(BF16)
        if k == 0:
            for j in range(cap):
                row_copy(1, j, b0 + 1, idx_ref[1, 0, j]).start(priority=j % 2)
            for j in range(cap):
                row_copy(0, j, (b0 + 2) % nb, idxn_ref[0, 0, j]).start(priority=j % 2)
        hid = (_silu(dot(x, wgb[...])) * dot(x, wub[...])).astype(BF16)
        ye_ref[k] = _to_row_tiles(dot(hid, wdb[...])).astype(BF16).reshape((cap // 2,) + PAIR_TILE)

    @pl.when(s == nsteps - 1)
    def _():
        wait_rows(0)


def _experts(idx, hm, wg, wu, wd):
    bn = hm.shape[0]
    ne, d, ff = wg.shape
    cap = idx.shape[-1]
    per = bn // 2
    nsteps = ne * per
    cur = lambda s: (s, 0, 0)
    nxt = lambda s: ((s + 1) % nsteps, 0, 0)
    wspec = pl.BlockSpec((1, d, ff), lambda s: (s // per, 0, 0))
    return pl.pallas_call(
        functools.partial(_expert_kernel, cap=cap, nb=bn, nsteps=nsteps),
        out_shape=jax.ShapeDtypeStruct((ne * bn, cap // 2) + PAIR_TILE, BF16),
        grid=(nsteps,),
        in_specs=[pl.BlockSpec((2, 1, cap), cur, memory_space=pltpu.SMEM),
                  pl.BlockSpec((2, 1, cap), nxt, memory_space=pltpu.SMEM),
                  pl.BlockSpec(memory_space=pl.ANY),
                  wspec, wspec, pl.BlockSpec((1, ff, d), lambda s: (s // per, 0, 0))],
        out_specs=pl.BlockSpec((2, cap // 2) + PAIR_TILE, lambda s: (s, 0, 0, 0)),
        scratch_shapes=[pltpu.VMEM((cap,) + ROW_TILE, F32), pltpu.VMEM((cap,) + ROW_TILE, F32),
                        pltpu.VMEM((d, ff), BF16), pltpu.VMEM((d, ff), BF16), pltpu.VMEM((ff, d), BF16),
                        pltpu.SemaphoreType.DMA((2,))],
        compiler_params=_cparams(1, VMEM_LIMIT),
        name="expert_ffn",
    )(idx, idx, hm, wg, wu, wd)


COMBINE_ROWS = 16


COMBINE_EXPERTS = 2


def _combine_kernel(*refs, cap, nb, rows_out):
    n = COMBINE_EXPERTS
    idx_refs, val_refs, ye_refs = refs[:n], refs[n:2 * n], refs[2 * n:3 * n]
    hx_ref, g2_ref, w_ref, o_ref, acc = refs[3 * n:]
    b = pl.program_id(0)
    e2 = pl.program_id(1)
    slot = b % 2
    acc_cur = acc.at[slot]

    @pl.when(jnp.logical_and(e2 == 0, b < nb))
    def _():
        acc_cur[...] = jnp.zeros(acc.shape[1:], F32)

    @pl.when(b < nb)
    def _():
        for idx_ref, val_ref, ye_ref in zip(idx_refs, val_refs, ye_refs):
            def body(i, carry, idx_ref=idx_ref, val_ref=val_ref, ye_ref=ye_ref):
                j0 = i * COMBINE_ROWS
                rows = [idx_ref[0, 0, j0 + u] for u in range(COMBINE_ROWS)]
                pairs = [ye_ref[0, i * (COMBINE_ROWS // 2) + p].astype(F32) for p in range(COMBINE_ROWS // 2)]
                ye = [pairs[u // 2][(u % 2) * ROW_TILE[0]:(u % 2 + 1) * ROW_TILE[0]] for u in range(COMBINE_ROWS)]
                new = [acc_cur[rows[u]] + ye[u] * val_ref[0, 0, j0 + u] for u in range(COMBINE_ROWS)]
                for u in range(COMBINE_ROWS):
                    acc_cur[rows[u]] = new[u]
                return carry

            lax.fori_loop(0, cap // COMBINE_ROWS, body, 0)

    @pl.when(b > 0)
    def _():
        f3 = acc[1 - slot, pl.ds(pl.multiple_of(e2 * rows_out, rows_out), rows_out)]
        ms = jnp.mean(f3 * f3, axis=(1, 2), keepdims=True)
        y = _from_row_tiles(f3 * lax.rsqrt(ms + RMS_EPS) * w_ref[...])
        o_ref[0] = hx_ref[0] + g2_ref[0] * y


def _combine_residual(idx, vals, ye, hx, g2, w):
    bn, length, d = hx.shape
    n = COMBINE_EXPERTS
    steps = N_EXPERTS // n
    cap = idx.shape[-1]
    rows_out = length // steps
    w3 = w.reshape((1,) + ROW_TILE)
    cur = lambda b: jnp.minimum(b, bn - 1)
    prev = lambda b: jnp.maximum(b - 1, 0)
    item = lambda k: (lambda b, e2: ((e2 * n + k) * bn + cur(b), 0, 0))
    item4 = lambda k: (lambda b, e2: ((e2 * n + k) * bn + cur(b), 0, 0, 0))
    smem = [pl.BlockSpec((1, 1, cap), item(k), memory_space=pltpu.SMEM) for k in range(n)]
    tok_prev = pl.BlockSpec((1, rows_out, d), lambda b, e2: (prev(b), jnp.where(b > 0, e2, 0), 0))
    return pl.pallas_call(
        functools.partial(_combine_kernel, cap=cap, nb=bn, rows_out=rows_out),
        out_shape=jax.ShapeDtypeStruct((bn, length, d), F32),
        grid=(bn + 1, steps),
        in_specs=smem + smem + [pl.BlockSpec((1, cap // 2) + PAIR_TILE, item4(k)) for k in range(n)]
                 + [tok_prev, pl.BlockSpec((1, 1, d), lambda b, e2: (prev(b), 0, 0)), _const_spec(w3.shape)],
        out_specs=tok_prev,
        scratch_shapes=[pltpu.VMEM((2, length) + ROW_TILE, F32)],
        compiler_params=_cparams(2, VMEM_LIMIT),
        name="expert_combine_residual",
    )(*([idx] * n + [vals] * n + [ye] * n), hx, g2, w3)


def kernel(x, c, ctx, c_ctx, w_mod, b_mod, norm_mix_pre, norm_mix_post, norm_ffn_pre, norm_ffn_post,
           w_in, conv_w, conv_b, dt_bias, a_log, d_skip, ssd_norm, w_fourier, w_ssd_out, b_gate, w_out,
           w_router, w_e_gate, w_e_up, w_e_down):
    bn, length, d = x.shape
    ctx_len = ctx.shape[1]
    l = 0
    row = lambda v: v.reshape(1, -1)

    cs = jnp.concatenate([c, c_ctx[None], jnp.zeros((7, d), F32)], axis=0)
    mods = _modulation(cs, w_mod[l], b_mod[l])
    mod_x = mods[:bn].reshape(bn, N_MOD, 1, d)
    mod_c = jnp.broadcast_to(mods[bn].reshape(1, N_MOD, 1, d), (bn, N_MOD, 1, d))
    sh1, sc1, g1, sh2, sc2, g2 = [mod_x[:, i] for i in range(N_MOD)]

    i1 = F_WIDTH
    i2 = i1 + D_INNER
    i3 = i2 + D_INNER
    i4 = i3 + BC_WIDTH
    i5 = i4 + BC_WIDTH
    i6 = i5 + 2 * N_HEADS
    wi = w_in[l]
    wdt = jnp.zeros((d, 256), F32).at[:, :N_HEADS].set(wi[:, i5:i5 + N_HEADS])
    wdt = wdt.at[:, 128:128 + N_HEADS].set(wi[:, i5 + N_HEADS:i6])
    dtb = jnp.zeros((1, 256), F32).at[0, :N_HEADS].set(dt_bias[l, 0]).at[0, 128:128 + N_HEADS].set(dt_bias[l, 1])
    dft_c, w1, w2 = _dft_constants()
    cw, cb = conv_w[l], conv_b[l]
    j1, j2 = D_INNER, D_INNER + BC_WIDTH
    w = dict(
        wuf=wi[:, :i1].astype(BF16), wz=wi[:, i1:i2].astype(BF16), wxs=wi[:, i2:i3].astype(BF16),
        wb=wi[:, i3:i4].astype(BF16), wc=wi[:, i4:i5].astype(BF16), wdt=wdt.astype(BF16),
        wg=wi[:, i6:].astype(BF16), dft=dft_c,
        cwx=cw[:, :j1], cbx=row(cb[:j1]), cwb=cw[:, j1:j2], cbb=row(cb[j1:j2]),
        cwc=cw[:, j2:], cbc=row(cb[j2:]), dtb=dtb, bg=row(b_gate[l]))
    gain_pre = row(norm_mix_pre[l])

    alog_pad = jnp.zeros((2, 1, 128), F32).at[:, 0, :N_HEADS].set(a_log[l])

    cxs, cbm, ccm, cdt = _in_projection(ctx, mod_c[:, 0], mod_c[:, 1], gain_pre, w, ctx_len, ctx_len, False)
    h_zero = jnp.zeros((bn, N_GROUPS, D_STATE, GROUP_COLS), F32)
    hc_f, hc_b = _ssd_scan(cxs, cbm, ccm, cdt, alog_pad, h_zero, h_zero, emit_y=False)

    zr, zi, z, xs, bm, cm, dt, gates = _in_projection(x, sh1, sc1, gain_pre, w, GRID_W, INPROJ_TILE, True)
    f = _sequence_dft_real(zr, zi, w1, w2)
    yf, _, yb, _ = _ssd_scan(xs, bm, cm, dt, alog_pad, hc_f, hc_b)

    consts = [row(jnp.repeat(d_skip[l], HEAD_DIM)), row(ssd_norm[l]), row(norm_mix_post[l]),
              row(norm_ffn_pre[l]), w_ssd_out[l].astype(BF16), w_fourier[l].astype(BF16),
              w_out[l].astype(BF16), jnp.pad(w_router[l], ((0, 0), (0, LANES - N_EXPERTS)))]
    hx, hm, aff_t = _mix(yf, yb, xs, z, f, gates, x, g1, sh2, sc2, consts, MIX_TILE)

    cap = EC_FACTOR * length // N_EXPERTS
    top_idx, top_aff = _route(aff_t, cap)
    idx = jnp.swapaxes(top_idx, 0, 1).reshape(N_EXPERTS * bn, 1, cap)
    vals = jnp.swapaxes(top_aff, 0, 1).reshape(N_EXPERTS * bn, 1, cap)
    ye = _experts(idx, hm, w_e_gate[l], w_e_up[l], w_e_down[l])
    return _combine_residual(idx, vals, ye, hx, g2, row(norm_ffn_post[l]))
```

```python
import functools

import numpy as np
import jax
import jax.numpy as jnp
from jax import lax
from jax.experimental import pallas as pl
from jax.experimental.pallas import tpu as pltpu

F32 = jnp.float32
BF16 = jnp.bfloat16
HIGHEST = lax.Precision.HIGHEST

D_MODEL = 1024
GRID_W = 64
F_GROUP_W = 128
F_WIDTH = 512
D_INNER = 1536
HEAD_DIM = 64
N_HEADS = 24
N_GROUPS = 4
HEADS_PER_GROUP = 6
D_STATE = 128
CHUNK = 128
BC_WIDTH = N_GROUPS * D_STATE
N_EXPERTS = 16
EC_FACTOR = 2
N_MOD = 6
RMS_EPS = 1e-6
GROUP_COLS = HEADS_PER_GROUP * HEAD_DIM

FFT_N1 = 128
FFT_N2 = 32
FFT_KB = 8

VMEM_LIMIT = 56 * 1024 * 1024
LANES = 128


def _cparams(n_axes, vmem=None):
    return pltpu.CompilerParams(dimension_semantics=("arbitrary",) * n_axes,
                                vmem_limit_bytes=vmem)


def _const_spec(shape):
    nd = len(shape)
    return pl.BlockSpec(shape, lambda *_: (0,) * nd, pipeline_mode=pl.Buffered(1))


def _rms(x, w):
    ms = jnp.mean(x * x, axis=-1, keepdims=True)
    return x * lax.rsqrt(ms + RMS_EPS) * w


def _sigmoid(x):
    return 1.0 / (1.0 + jnp.exp(-x))


def _silu(x):
    return x * _sigmoid(x)


ROW_TILE = (8, 128)
PAIR_TILE = (16, 128)


def _to_row_tiles(v):
    parts = [v[:, c * 128:(c + 1) * 128] for c in range(ROW_TILE[0])]
    return jnp.swapaxes(jnp.stack(parts, axis=0), 0, 1)


def _from_row_tiles(v3):
    t = jnp.swapaxes(v3, 0, 1)
    return jnp.concatenate([t[c] for c in range(ROW_TILE[0])], axis=1)


def _mod_kernel(c_ref, w_ref, b_ref, o_ref):
    c = c_ref[...]
    o_ref[...] = jnp.dot(_silu(c), w_ref[...], precision=HIGHEST,
                         preferred_element_type=F32) + b_ref[...]


def _modulation(cs, w_mod, b_mod):
    rows = cs.shape[0]
    n = w_mod.shape[1]
    blk = D_MODEL
    return pl.pallas_call(
        _mod_kernel,
        out_shape=jax.ShapeDtypeStruct((rows, n), F32),
        grid=(n // blk,),
        in_specs=[pl.BlockSpec((rows, D_MODEL), lambda j: (0, 0)),
                  pl.BlockSpec((D_MODEL, blk), lambda j: (0, j)),
                  pl.BlockSpec((1, blk), lambda j: (0, j))],
        out_specs=pl.BlockSpec((rows, blk), lambda j: (0, j)),
        compiler_params=_cparams(1),
        name="modulation",
    )(cs, w_mod, b_mod.reshape(1, n))


def _conv_silu(xbc, cw_ref, cb_ref, row_len):
    tm = xbc.shape[0]
    pos = lax.broadcasted_iota(jnp.int32, (tm, 1), 0) % row_len
    prev = jnp.where(pos == 0, 0.0, pltpu.roll(xbc, 1, axis=0))
    nxt = jnp.where(pos == row_len - 1, 0.0, pltpu.roll(xbc, tm - 1, axis=0))
    out = prev * cw_ref[0:1, :] + xbc * cw_ref[1:2, :] + nxt * cw_ref[2:3, :] + cb_ref[...]
    return _silu(out)


INPROJ_SUB = 128
INPROJ_TILE = 512


def _softplus(x):
    return jnp.maximum(x, 0.0) + jnp.log(1.0 + jnp.exp(-jnp.abs(x)))


def _inproj_kernel(x_ref, sh_ref, sc_ref, gain_ref, wuf_ref, wz_ref, wxs_ref, wb_ref, wc_ref,
                   wdt_ref, wg_ref, dft_ref, cwx_ref, cbx_ref, cwb_ref, cbb_ref, cwc_ref, cbc_ref,
                   dtb_ref, bg_ref,
                   zr_ref, zi_ref, z_ref, xs_ref, bm_ref, cm_ref, dt_ref, g_ref, *, row_len):
    dot = functools.partial(jnp.dot, preferred_element_type=F32)
    tm = x_ref.shape[1]
    sub = min(tm, max(INPROJ_SUB, row_len))
    assert sub % row_len == 0 and tm % sub == 0
    for r0 in range(0, tm, sub):
        rs = slice(r0, r0 + sub)
        h = _rms(x_ref[0, rs], gain_ref[...]) * (1.0 + sc_ref[0]) + sh_ref[0]
        hb = h.astype(BF16)
        xs_ref[0, rs] = _conv_silu(dot(hb, wxs_ref[...]), cwx_ref, cbx_ref, row_len).astype(BF16)
        bm_ref[0, rs] = _conv_silu(dot(hb, wb_ref[...]), cwb_ref, cbb_ref, row_len).astype(BF16)
        cm_ref[0, rs] = _conv_silu(dot(hb, wc_ref[...]), cwc_ref, cbc_ref, row_len).astype(BF16)
        dt_ref[0, rs] = _softplus(dot(hb, wdt_ref[...]) + dtb_ref[...])
        if zr_ref is None:
            continue
        uf = dot(hb, wuf_ref[...]).astype(BF16)
        for g in range(F_WIDTH // F_GROUP_W):
            sl = slice(g * F_GROUP_W, (g + 1) * F_GROUP_W)
            zz = dot(uf[:, sl], dft_ref[...].astype(BF16))
            zr_ref[0, rs, sl] = zz[:, :F_GROUP_W].astype(BF16)
            zi_ref[0, rs, sl] = zz[:, F_GROUP_W:].astype(BF16)
        z_ref[0, rs] = dot(hb, wz_ref[...]).astype(BF16)
        g_ref[0, rs] = _sigmoid(dot(hb, wg_ref[...]) + bg_ref[...]).astype(BF16)


def _inproj_ctx_kernel(x_ref, sh_ref, sc_ref, gain_ref, wxs_ref, wb_ref, wc_ref, wdt_ref,
                       cwx_ref, cbx_ref, cwb_ref, cbb_ref, cwc_ref, cbc_ref, dtb_ref,
                       xs_ref, bm_ref, cm_ref, dt_ref, *, row_len):
    _inproj_kernel(x_ref, sh_ref, sc_ref, gain_ref, None, None, wxs_ref, wb_ref, wc_ref,
                   wdt_ref, None, None, cwx_ref, cbx_ref, cwb_ref, cbb_ref, cwc_ref, cbc_ref,
                   dtb_ref, None, None, None, None, xs_ref, bm_ref, cm_ref, dt_ref, None,
                   row_len=row_len)


def _in_projection(x, shift, scale, gain, w, row_len, tm, full):
    bn, length, d = x.shape
    grid = (bn, length // tm)
    tok = lambda width: pl.BlockSpec((1, tm, width), lambda b, i: (b, i, 0))
    vec = pl.BlockSpec((1, 1, d), lambda b, i: (b, 0, 0))
    out = lambda width, dt: jax.ShapeDtypeStruct((bn, length, width), dt)
    conv_ops = [w["cwx"], w["cbx"], w["cwb"], w["cbb"], w["cwc"], w["cbc"]]
    ssd_shapes = [out(D_INNER, BF16), out(BC_WIDTH, BF16), out(BC_WIDTH, BF16), out(256, F32)]
    ssd_specs = [tok(D_INNER), tok(BC_WIDTH), tok(BC_WIDTH), tok(256)]
    if full:
        ops = [x, shift, scale, gain, w["wuf"], w["wz"], w["wxs"], w["wb"], w["wc"], w["wdt"], w["wg"],
               w["dft"]] + conv_ops + [w["dtb"], w["bg"]]
        kern = functools.partial(_inproj_kernel, row_len=row_len)
        out_shape = [out(F_WIDTH, BF16), out(F_WIDTH, BF16), out(D_INNER, BF16)] + ssd_shapes + \
                    [out(2 * D_MODEL, BF16)]
        out_specs = [tok(F_WIDTH), tok(F_WIDTH), tok(D_INNER)] + ssd_specs + [tok(2 * D_MODEL)]
        name = "in_projection"
    else:
        ops = [x, shift, scale, gain, w["wxs"], w["wb"], w["wc"], w["wdt"]] + conv_ops + [w["dtb"]]
        kern = functools.partial(_inproj_ctx_kernel, row_len=row_len)
        out_shape = ssd_shapes
        out_specs = ssd_specs
        name = "in_projection_ctx"
    in_specs = [tok(d), vec, vec] + [_const_spec(o.shape) for o in ops[3:]]
    return pl.pallas_call(kern, out_shape=out_shape, grid=grid, in_specs=in_specs,
                          out_specs=out_specs, compiler_params=_cparams(2, VMEM_LIMIT),
                          name=name)(*ops)


def _dft_constants():
    j = np.arange(F_GROUP_W)
    ang = 2.0 * np.pi * np.outer(j, j) / F_GROUP_W
    dft_c = np.concatenate([np.cos(ang), -np.sin(ang)], axis=1)
    k1 = np.arange(FFT_N1)
    n2 = np.arange(FFT_N2)
    phi = 2.0 * np.pi * k1[None, :, None] * (FFT_N2 * k1[None, None, :] + n2[:, None, None]) / (FFT_N1 * FFT_N2)
    c1, s1 = np.cos(phi), np.sin(phi)
    w1 = np.concatenate([np.concatenate([c1, s1], axis=2), np.concatenate([-s1, c1], axis=2)], axis=1)
    a2 = 2.0 * np.pi * np.outer(n2, n2) / FFT_N2
    w2 = np.zeros((FFT_N2, FFT_KB, 2, FFT_KB, FFT_N2))
    for jj in range(FFT_KB):
        w2[:, jj, 0, jj, :] = np.cos(a2)
        w2[:, jj, 1, jj, :] = np.sin(a2)
    w2 = w2.reshape(FFT_N2 * FFT_KB, 2 * FFT_KB * FFT_N2)
    return jnp.asarray(dft_c, F32), jnp.asarray(w1, F32), jnp.asarray(w2, F32)


FFT_QB = 16


def _seq_dft_kernel(zr_ref, zi_ref, w1_ref, w2_ref, o_ref, qr_s, qi_s, tr_s, ti_s, *, scale):
    for qb in range(FFT_N2 // FFT_QB):
        qs = slice(qb * FFT_QB, (qb + 1) * FFT_QB)
        zr_t = jnp.swapaxes(zr_ref[0, :, qs, :], 0, 1)
        zi_t = jnp.swapaxes(zi_ref[0, :, qs, :], 0, 1)
        for q in range(FFT_QB):
            n2 = qb * FFT_QB + q
            rhs = jnp.concatenate([zr_t[q], zi_t[q]], axis=0)
            t = jnp.dot(w1_ref[n2].astype(BF16), rhs, preferred_element_type=F32)
            qr_s[q] = t[:FFT_N1].astype(BF16)
            qi_s[q] = t[FFT_N1:].astype(BF16)
        tr_s[:, qs, :] = jnp.swapaxes(qr_s[...], 0, 1)
        ti_s[:, qs, :] = jnp.swapaxes(qi_s[...], 0, 1)
    w2 = w2_ref[...].astype(BF16)
    rows = FFT_KB * FFT_N2
    for kb in range(FFT_N1 // FFT_KB):
        ks = slice(kb * FFT_KB, (kb + 1) * FFT_KB)
        rhs = jnp.concatenate([tr_s[ks].reshape(rows, F_WIDTH), ti_s[ks].reshape(rows, F_WIDTH)], axis=0)
        y = jnp.dot(w2, rhs, preferred_element_type=F32) * scale
        o_ref[0, :, ks, :] = y.reshape(FFT_N2, FFT_KB, F_WIDTH)


def _sequence_dft_real(zr, zi, w1, w2):
    bn, length, width = zr.shape
    blk = pl.BlockSpec((1, FFT_N1, FFT_N2, width), lambda b: (b, 0, 0, 0))
    scale = 1.0 / float(np.sqrt(length * F_GROUP_W))
    f = pl.pallas_call(
        functools.partial(_seq_dft_kernel, scale=scale),
        out_shape=jax.ShapeDtypeStruct((bn, FFT_N2, FFT_N1, width), F32),
        grid=(bn,),
        in_specs=[blk, blk, _const_spec(w1.shape), _const_spec(w2.shape)],
        out_specs=pl.BlockSpec((1, FFT_N2, FFT_N1, width), lambda b: (b, 0, 0, 0)),
        scratch_shapes=[pltpu.VMEM((FFT_QB, FFT_N1, width), BF16)] * 2
                       + [pltpu.VMEM((FFT_N1, FFT_N2, width), BF16)] * 2,
        compiler_params=_cparams(1, VMEM_LIMIT),
        name="seq_dft",
    )(zr.reshape(bn, FFT_N1, FFT_N2, width), zi.reshape(bn, FFT_N1, FFT_N2, width), w1, w2)
    return f.reshape(bn, length, width)


SSD_SUB = 64
SSD_CHUNKS_PER_STEP = 8
LOG2E = 1.4426950408889634


class _Dir:
    pass


def _ssd_prepare(dt_ref, alog_ref, reverse, row0):
    q, sb = CHUNK, SSD_SUB
    d = _Dir()
    d.reverse = reverse
    r_i = lax.broadcasted_iota(jnp.int32, (q, q), 0)
    c_i = lax.broadcasted_iota(jnp.int32, (q, q), 1)
    tri = ((r_i <= c_i) if reverse else (r_i >= c_i)).astype(BF16)
    a = -jnp.exp(alog_ref[0])
    d.row0 = row0
    d.dt = dt_ref[0, row0:row0 + q]
    da = d.dt * (a * LOG2E)
    p0 = da.astype(BF16)
    r1 = da - p0.astype(F32)
    p1 = r1.astype(BF16)
    p2 = (r1 - p1.astype(F32)).astype(BF16)
    parts = jnp.dot(tri, jnp.concatenate([p0, p1, p2], axis=1), preferred_element_type=F32)
    d.acum = parts[:, :LANES] + (parts[:, LANES:2 * LANES] + parts[:, 2 * LANES:])
    lane = lax.broadcasted_iota(jnp.int32, (sb, 128), 1)
    d.lo = lane < HEAD_DIM
    lo_h = lax.broadcasted_iota(jnp.int32, (q, 128), 1) < HEAD_DIM
    row_l = lax.broadcasted_iota(jnp.int32, (sb, 128), 0)
    s_l = jnp.where(d.lo, lane, lane - HEAD_DIM)
    d.mask = (s_l >= row_l) if reverse else (s_l <= row_l)
    d.order = (1, 0) if reverse else (0, 1)

    def halves(m):
        m_t = m.T
        m_sw = pltpu.roll(m_t, sb, axis=1)
        return (jnp.where(lo_h, m_t, m_sw), jnp.where(lo_h, m_sw, m_t))

    d.rt = halves(d.acum)
    d.rt_dt = halves(d.dt)
    return d


def _ssd_block_rows(d, step):
    sb = SSD_SUB
    blk = d.order[step]
    end_r = blk * sb if d.reverse else blk * sb + sb - 1
    end_row = d.acum[end_r:end_r + 1, :]
    if step == 0:
        base_row = jnp.zeros((1, 128), F32)
    else:
        pr = d.order[0] * sb if d.reverse else d.order[0] * sb + sb - 1
        base_row = d.acum[pr:pr + 1, :]
    d.rows8 = jnp.concatenate([base_row, jnp.exp2(end_row - base_row), jnp.zeros((6, 128), F32)], axis=0)
    local = slice(blk * sb, (blk + 1) * sb)
    d.rs = slice(d.row0 + blk * sb, d.row0 + (blk + 1) * sb)
    d.blk = blk
    d.acum_b = d.acum[local]
    d.dw_b = d.dt[local] * jnp.exp2(end_row - d.acum_b)


def _ssd_group_begin(d, refs, g):
    xs_ref, bm_ref, cm_ref, y_ref, st_ref = refs
    gs = slice(g * D_STATE, (g + 1) * D_STATE)
    bg = bm_ref[0, d.rs, gs]
    cg = cm_ref[0, d.rs, gs]
    bg_t = bg.astype(F32).T.astype(BF16)
    h_t = st_ref[g]
    d.grp = dict(bg_t=bg_t, h_t=h_t, xw=[], cd=[])
    if y_ref is not None:
        d.grp["cb2"] = jnp.dot(cg, jnp.concatenate([bg_t, bg_t], axis=1), preferred_element_type=F32)
        d.grp["yoff"] = jnp.dot(cg, h_t.astype(BF16), preferred_element_type=F32)


def _ssd_pair(d, refs, g, pr_i):
    xs_ref, bm_ref, cm_ref, y_ref, st_ref = refs
    lo = d.lo
    lo_row = lo[0:1]
    h0 = g * HEADS_PER_GROUP + 2 * pr_i
    h1 = h0 + 1
    ps = slice((g * 3 + pr_i) * 128, (g * 3 + pr_i + 1) * 128)
    pat = jnp.where(lo, h0, h1)
    dwp = jnp.take_along_axis(d.dw_b, pat, axis=1)
    r8 = jnp.take_along_axis(d.rows8, pat[0:8], axis=1)
    base_p, cd_p = r8[0:1], r8[1:2]
    x = xs_ref[0, d.rs, ps]
    gd = d.grp
    gd["xw"].append((x.astype(F32) * dwp).astype(BF16))
    gd["cd"].append(cd_p)
    if y_ref is None:
        return
    colp = jnp.take_along_axis(d.acum_b, pat, axis=1)
    rt, rt_dt = d.rt[d.blk], d.rt_dt[d.blk]
    rowp = jnp.where(lo_row, rt[h0:h0 + 1, :], rt[h1:h1 + 1, :])
    dt_row = jnp.where(lo_row, rt_dt[h0:h0 + 1, :], rt_dt[h1:h1 + 1, :])
    m = (jnp.exp2(jnp.where(d.mask, colp - rowp, -1e30)) * (gd["cb2"] * dt_row)).astype(BF16)
    zero = jnp.zeros_like(x)
    rhs = jnp.concatenate([jnp.where(lo, x, zero), jnp.where(lo, zero, x)], axis=0)
    ydiag = jnp.dot(m, rhs, preferred_element_type=F32)
    y = ydiag + gd["yoff"][:, pr_i * 128:(pr_i + 1) * 128] * jnp.exp2(colp - base_p)
    y_ref[0, d.rs, ps] = y.astype(BF16)


def _ssd_group_end(d, refs, g):
    st_ref = refs[4]
    gd = d.grp
    xw = jnp.concatenate(gd["xw"], axis=1)
    cd = jnp.concatenate(gd["cd"], axis=1)
    st_ref[g] = gd["h_t"] * cd + jnp.dot(gd["bg_t"], xw, preferred_element_type=F32)


def _ssd_kernel(xf, bf, cf, df, af, hf, xb, bb, cb_, db, ab, hb, *out_and_scratch, nchunks, emit_y):
    if emit_y:
        yf, hff, yb, hfb, stf, stb = out_and_scratch
    else:
        (hff, hfb, stf, stb), yf, yb = out_and_scratch, None, None
    c = pl.program_id(1)

    @pl.when(c == 0)
    def _():
        stf[...] = hf[0]
        stb[...] = hb[0]

    refs = ((xf, bf, cf, yf, stf), (xb, bb, cb_, yb, stb))
    per_step = xf.shape[1] // CHUNK
    for ci in range(per_step):
        dirs = (_ssd_prepare(df, af, False, ci * CHUNK), _ssd_prepare(db, ab, True, (per_step - 1 - ci) * CHUNK))
        for step in range(CHUNK // SSD_SUB):
            for d in dirs:
                _ssd_block_rows(d, step)
            for g in range(N_GROUPS):
                for d, r in zip(dirs, refs):
                    _ssd_group_begin(d, r, g)
                for pr_i in range(HEADS_PER_GROUP // 2):
                    for d, r in zip(dirs, refs):
                        _ssd_pair(d, r, g, pr_i)
                for d, r in zip(dirs, refs):
                    _ssd_group_end(d, r, g)

    @pl.when(c == nchunks - 1)
    def _():
        hff[0] = stf[...]
        hfb[0] = stb[...]


def _ssd_scan(xs, bm, cm, dt, alog_pad, h0f, h0b, emit_y=True):
    bn, length, _ = xs.shape
    rows = CHUNK * min(SSD_CHUNKS_PER_STEP, length // CHUNK)
    nchunks = length // rows
    st_shape = (N_GROUPS, D_STATE, GROUP_COLS)
    st_spec = pl.BlockSpec((1,) + st_shape, lambda b, c: (b, 0, 0, 0))

    def specs(reverse):
        cc = (lambda c: nchunks - 1 - c) if reverse else (lambda c: c)
        di = 1 if reverse else 0
        tok = lambda width: pl.BlockSpec((1, rows, width), lambda b, c: (b, cc(c), 0))
        ins = [tok(D_INNER), tok(BC_WIDTH), tok(BC_WIDTH),
               pl.BlockSpec((1, rows, 128), lambda b, c: (b, cc(c), di)),
               pl.BlockSpec((1, 1, 128), lambda b, c: (di, 0, 0)), st_spec]
        return ins, ([tok(D_INNER)] if emit_y else []) + [st_spec]

    in_f, out_f = specs(False)
    in_b, out_b = specs(True)
    y_shape = [jax.ShapeDtypeStruct((bn, length, D_INNER), BF16)] if emit_y else []
    h_shape = [jax.ShapeDtypeStruct((bn,) + st_shape, F32)]
    return pl.pallas_call(
        functools.partial(_ssd_kernel, nchunks=nchunks, emit_y=emit_y),
        out_shape=y_shape + h_shape + y_shape + h_shape,
        grid=(bn, nchunks),
        in_specs=in_f + in_b,
        out_specs=out_f + out_b,
        scratch_shapes=[pltpu.VMEM(st_shape, F32), pltpu.VMEM(st_shape, F32)],
        compiler_params=_cparams(2),
        name="ssd_scan",
    )(xs, bm, cm, dt, alog_pad, h0f, xs, bm, cm, dt, alog_pad, h0b)


MIX_SUB = 128
MIX_TILE = 512


def _split_bf16(v):
    hi = v.astype(BF16)
    return hi, (v - hi.astype(F32)).astype(BF16)


def _mix_kernel(yf_ref, yb_ref, xs_ref, z_ref, f_ref, g_ref, x_ref, g1_ref, sh2_ref, sc2_ref,
                dsk_ref, nssd_ref, npost_ref, npre_ref, wso_ref, wf_ref, wo_ref, wr_ref,
                hx_ref, hm_ref, aff_ref):
    dot = functools.partial(jnp.dot, preferred_element_type=F32)
    wr_hi, wr_lo = _split_bf16(wr_ref[...])
    subs = [slice(r0, r0 + MIX_SUB) for r0 in range(0, x_ref.shape[1], MIX_SUB)]
    ys = []
    for rs in subs:
        y = (yf_ref[0, rs].astype(F32) + yb_ref[0, rs].astype(F32)
             + dsk_ref[...] * xs_ref[0, rs].astype(F32))
        ys.append(_rms(y * _silu(z_ref[0, rs].astype(F32)), nssd_ref[...]).astype(BF16))
    s_branches = [dot(y, wso_ref[...]) for y in ys]
    f_branches = [dot(f_ref[0, rs].astype(BF16), wf_ref[...]) for rs in subs]
    merged = []
    for rs, s_branch, f_branch in zip(subs, s_branches, f_branches):
        gates = g_ref[0, rs].astype(F32)
        merged.append((gates[:, :D_MODEL] * f_branch + gates[:, D_MODEL:] * s_branch).astype(BF16))
    mixes = [dot(m, wo_ref[...]) for m in merged]
    hms = []
    for rs, mix in zip(subs, mixes):
        hx = x_ref[0, rs] + g1_ref[0] * _rms(mix, npost_ref[...])
        hx_ref[0, rs] = hx
        hm = _rms(hx, npre_ref[...]) * (1.0 + sc2_ref[0]) + sh2_ref[0]
        hm_ref[0, rs] = _to_row_tiles(hm)
        hms.append(hm)
    for rs, hm in zip(subs, hms):
        hm_hi, hm_lo = _split_bf16(hm)
        logits = dot(hm_hi, wr_hi) + (dot(hm_hi, wr_lo) + dot(hm_lo, wr_hi))
        real = lax.broadcasted_iota(jnp.int32, logits.shape, 1) < N_EXPERTS
        logits = jnp.where(real, logits, -1e30)
        e = jnp.exp(logits - jnp.max(logits, axis=-1, keepdims=True))
        aff = e / jnp.sum(e, axis=-1, keepdims=True)
        aff_ref[0, :, rs] = aff.T[:N_EXPERTS]


def _mix(yf, yb, xs, z, f, gates, x, g1, sh2, sc2, consts, tm):
    bn, length, d = x.shape
    tok = lambda width: pl.BlockSpec((1, tm, width), lambda b, i: (b, i, 0))
    vec = pl.BlockSpec((1, 1, d), lambda b, i: (b, 0, 0))
    return pl.pallas_call(
        _mix_kernel,
        out_shape=[jax.ShapeDtypeStruct((bn, length, d), F32),
                   jax.ShapeDtypeStruct((bn, length) + ROW_TILE, F32),
                   jax.ShapeDtypeStruct((bn, N_EXPERTS, length), F32)],
        grid=(bn, length // tm),
        in_specs=[tok(D_INNER), tok(D_INNER), tok(D_INNER), tok(D_INNER), tok(F_WIDTH), tok(2 * D_MODEL), tok(d),
                  vec, vec, vec] + [_const_spec(c.shape) for c in consts],
        out_specs=[tok(d), pl.BlockSpec((1, tm) + ROW_TILE, lambda b, i: (b, i, 0, 0)),
                   pl.BlockSpec((1, N_EXPERTS, tm), lambda b, i: (b, 0, i))],
        compiler_params=_cparams(2, VMEM_LIMIT),
        name="merge_out_router",
    )(yf, yb, xs, z, f, gates, x, g1, sh2, sc2, *consts)


def _to_tile_rows(v):
    return jnp.concatenate([v[:, k * LANES:(k + 1) * LANES] for k in range(v.shape[1] // LANES)], axis=0)


def _route_kernel(aff_ref, idx_ref, val_ref, *, cap):
    I32 = jnp.int32
    a = aff_ref[0]
    ne, length = a.shape
    nt = length // LANES
    rows = nt * ne

    def search(i, t):
        cand = t | lax.shift_left(jnp.int32(1), 30 - i)
        cnt = jnp.sum((a >= pltpu.bitcast(cand, F32)).astype(F32), axis=1, keepdims=True)
        return jnp.where(cnt >= cap, cand, t)

    thr = pltpu.bitcast(lax.fori_loop(0, 31, search, jnp.zeros((ne, 1), I32)), F32)
    n_gt = jnp.sum((a > thr).astype(F32), axis=1, keepdims=True)
    need = cap - n_gt

    val_r = _to_tile_rows(a)
    tile_rows = lambda v: jnp.concatenate([v] * nt, axis=0)
    thr_r = tile_rows(thr)
    need_r = tile_rows(need)
    li = lax.broadcasted_iota(I32, (LANES, LANES), 0)
    lj = lax.broadcasted_iota(I32, (LANES, LANES), 1)
    upper = (li < lj).astype(BF16)
    ri = lax.broadcasted_iota(I32, (rows, rows), 0)
    rj = lax.broadcasted_iota(I32, (rows, rows), 1)
    earlier = (((ri & (ne - 1)) == (rj & (ne - 1))) & (rj < ri)).astype(BF16)

    def prefix(mask):
        local = jnp.dot(mask.astype(BF16), upper, preferred_element_type=F32)
        total = jnp.sum(mask.astype(F32), axis=1, keepdims=True)
        tot_b = jnp.broadcast_to(total, (rows, LANES)).astype(BF16)
        offs = jnp.dot(earlier, tot_b, preferred_element_type=F32)[:, 0:1]
        return local, offs

    eq = val_r == thr_r
    eq_local, eq_off = prefix(eq)
    sel = (val_r > thr_r) | (eq & (eq_local + eq_off < need_r))
    sel_local, sel_off = prefix(sel)

    lane = lax.broadcasted_iota(I32, (rows, LANES), 1)
    k_row = lax.shift_right_logical(lax.broadcasted_iota(I32, (rows, LANES), 0), int(np.log2(ne)))
    tok = jnp.where(sel, k_row * LANES + lane, -1)
    val = val_r
    dist = jnp.where(sel, lane - sel_local.astype(I32), 0)
    for s in range(int(np.log2(LANES))):
        sh = 1 << s
        mv = (tok >= 0) & ((lax.shift_right_logical(dist, s) & 1) == 1)
        in_tok = pltpu.roll(jnp.where(mv, tok, -1), LANES - sh, axis=1)
        in_val = pltpu.roll(val, LANES - sh, axis=1)
        in_dist = pltpu.roll(dist, LANES - sh, axis=1)
        arrive = in_tok >= 0
        stay = (tok >= 0) & jnp.logical_not(mv)
        tok = jnp.where(arrive, in_tok, jnp.where(stay, tok, -1))
        val = jnp.where(arrive, in_val, val)
        dist = jnp.where(arrive, in_dist, dist)

    off = sel_off.astype(I32)
    lane_e = lax.broadcasted_iota(I32, (ne, LANES), 1)
    ncol = cap // LANES
    out_tok = [jnp.zeros((ne, LANES), I32) for _ in range(ncol)]
    out_val = [jnp.zeros((ne, LANES), F32) for _ in range(ncol)]
    for k in range(nt):
        rs = slice(k * ne, (k + 1) * ne)
        o = off[rs]
        sh, col = o & (LANES - 1), lax.shift_right_logical(o, int(np.log2(LANES)))
        pat = (lane_e - sh) & (LANES - 1)
        r_tok = jnp.take_along_axis(tok[rs], pat, axis=1)
        r_val = jnp.take_along_axis(val[rs], pat, axis=1)
        ok = r_tok >= 0
        for c in range(ncol):
            here = ok & (((col == c) & (lane_e >= sh)) | ((col + 1 == c) & (lane_e < sh)))
            out_tok[c] = jnp.where(here, r_tok, out_tok[c])
            out_val[c] = jnp.where(here, r_val, out_val[c])
    idx_ref[0] = jnp.concatenate(out_tok, axis=1)
    val_ref[0] = jnp.concatenate(out_val, axis=1)


def _route(aff_t, cap):
    bn, ne, length = aff_t.shape
    assert ne & (ne - 1) == 0 and cap % LANES == 0 and length % LANES == 0
    return pl.pallas_call(
        functools.partial(_route_kernel, cap=cap),
        out_shape=[jax.ShapeDtypeStruct((bn, ne, cap), jnp.int32), jax.ShapeDtypeStruct((bn, ne, cap), F32)],
        grid=(bn,),
        in_specs=[pl.BlockSpec((1, ne, length), lambda b: (b, 0, 0))],
        out_specs=[pl.BlockSpec((1, ne, cap), lambda b: (b, 0, 0))] * 2,
        compiler_params=_cparams(1),
        name="expert_choice_route",
    )(aff_t)


def _expert_kernel(idx_ref, idxn_ref, hm_ref, wg_ref, wu_ref, wd_ref, ye_ref, buf0, buf1, wgb, wub, wdb, sem,
                   *, cap, nb, nsteps):
    s = pl.program_id(0)
    bufs = (buf0, buf1)
    dot = functools.partial(jnp.dot, preferred_element_type=F32)

    def row_copy(k, j, b, r):
        return pltpu.make_async_copy(hm_ref.at[b, pl.ds(r, 1)], bufs[k].at[pl.ds(j, 1)], sem.at[k])

    def wait_rows(k):
        pltpu.make_async_copy(hm_ref.at[0, pl.ds(0, cap)], bufs[k], sem.at[k]).wait()

    b0 = (2 * s) % nb

    @pl.when(b0 == 0)
    def _():
        wgb[...] = wg_ref[0].astype(BF16)
        wub[...] = wu_ref[0].astype(BF16)
        wdb[...] = wd_ref[0].astype(BF16)

    @pl.when(s == 0)
    def _():
        def issue(j, carry):
            row_copy(0, j, b0, idx_ref[0, 0, j]).start()
            return carry
        lax.fori_loop(0, cap, issue, 0)

    for k in range(2):
        wait_rows(k)
        x = _from_row_tiles(bufs[k][...]).astype(BF16)
        if k == 0:
            for j in range(cap):
                row_copy(1, j, b0 + 1, idx_ref[1, 0, j]).start(priority=j % 2)
            for j in range(cap):
                row_copy(0, j, (b0 + 2) % nb, idxn_ref[0, 0, j]).start(priority=j % 2)
        hid = (_silu(dot(x, wgb[...])) * dot(x, wub[...])).astype(BF16)
        ye_ref[k] = _to_row_tiles(dot(hid, wdb[...])).astype(BF16).reshape((cap // 2,) + PAIR_TILE)

    @pl.when(s == nsteps - 1)
    def _():
        wait_rows(0)


def _experts(idx, hm, wg, wu, wd):
    bn = hm.shape[0]
    ne, d, ff = wg.shape
    cap = idx.shape[-1]
    per = bn // 2
    nsteps = ne * per
    cur = lambda s: (s, 0, 0)
    nxt = lambda s: ((s + 1) % nsteps, 0, 0)
    wspec = pl.BlockSpec((1, d, ff), lambda s: (s // per, 0, 0))
    return pl.pallas_call(
        functools.partial(_expert_kernel, cap=cap, nb=bn, nsteps=nsteps),
        out_shape=jax.ShapeDtypeStruct((ne * bn, cap // 2) + PAIR_TILE, BF16),
        grid=(nsteps,),
        in_specs=[pl.BlockSpec((2, 1, cap), cur, memory_space=pltpu.SMEM),
                  pl.BlockSpec((2, 1, cap), nxt, memory_space=pltpu.SMEM),
                  pl.BlockSpec(memory_space=pl.ANY),
                  wspec, wspec, pl.BlockSpec((1, ff, d), lambda s: (s // per, 0, 0))],
        out_specs=pl.BlockSpec((2, cap // 2) + PAIR_TILE, lambda s: (s, 0, 0, 0)),
        scratch_shapes=[pltpu.VMEM((cap,) + ROW_TILE, F32), pltpu.VMEM((cap,) + ROW_TILE, F32),
                        pltpu.VMEM((d, ff), BF16), pltpu.VMEM((d, ff), BF16), pltpu.VMEM((ff, d), BF16),
                        pltpu.SemaphoreType.DMA((2,))],
        compiler_params=_cparams(1, VMEM_LIMIT),
        name="expert_ffn",
    )(idx, idx, hm, wg, wu, wd)


COMBINE_ROWS = 16


COMBINE_EXPERTS = 2


def _combine_kernel(*refs, cap, nb, rows_out):
    n = COMBINE_EXPERTS
    idx_refs, val_refs, ye_refs = refs[:n], refs[n:2 * n], refs[2 * n:3 * n]
    hx_ref, g2_ref, w_ref, o_ref, acc = refs[3 * n:]
    b = pl.program_id(0)
    e2 = pl.program_id(1)
    slot = b % 2
    acc_cur = acc.at[slot]

    @pl.when(jnp.logical_and(e2 == 0, b < nb))
    def _():
        acc_cur[...] = jnp.zeros(acc.shape[1:], F32)

    @pl.when(b < nb)
    def _():
        for idx_ref, val_ref, ye_ref in zip(idx_refs, val_refs, ye_refs):
            def body(i, carry, idx_ref=idx_ref, val_ref=val_ref, ye_ref=ye_ref):
                j0 = i * COMBINE_ROWS
                rows = [idx_ref[0, 0, j0 + u] for u in range(COMBINE_ROWS)]
                pairs = [ye_ref[0, i * (COMBINE_ROWS // 2) + p].astype(F32) for p in range(COMBINE_ROWS // 2)]
                ye = [pairs[u // 2][(u % 2) * ROW_TILE[0]:(u % 2 + 1) * ROW_TILE[0]] for u in range(COMBINE_ROWS)]
                new = [acc_cur[rows[u]] + ye[u] * val_ref[0, 0, j0 + u] for u in range(COMBINE_ROWS)]
                for u in range(COMBINE_ROWS):
                    acc_cur[rows[u]] = new[u]
                return carry

            lax.fori_loop(0, cap // COMBINE_ROWS, body, 0)

    @pl.when(b > 0)
    def _():
        f3 = acc[1 - slot, pl.ds(pl.multiple_of(e2 * rows_out, rows_out), rows_out)]
        ms = jnp.mean(f3 * f3, axis=(1, 2), keepdims=True)
        y = _from_row_tiles(f3 * lax.rsqrt(ms + RMS_EPS) * w_ref[...])
        o_ref[0] = hx_ref[0] + g2_ref[0] * y


def _combine_residual(idx, vals, ye, hx, g2, w):
    bn, length, d = hx.shape
    n = COMBINE_EXPERTS
    steps = N_EXPERTS // n
    cap = idx.shape[-1]
    rows_out = length // steps
    w3 = w.reshape((1,) + ROW_TILE)
    cur = lambda b: jnp.minimum(b, bn - 1)
    prev = lambda b: jnp.maximum(b - 1, 0)
    item = lambda k: (lambda b, e2: ((e2 * n + k) * bn + cur(b), 0, 0))
    item4 = lambda k: (lambda b, e2: ((e2 * n + k) * bn + cur(b), 0, 0, 0))
    smem = [pl.BlockSpec((1, 1, cap), item(k), memory_space=pltpu.SMEM) for k in range(n)]
    tok_prev = pl.BlockSpec((1, rows_out, d), lambda b, e2: (prev(b), jnp.where(b > 0, e2, 0), 0))
    return pl.pallas_call(
        functools.partial(_combine_kernel, cap=cap, nb=bn, rows_out=rows_out),
        out_shape=jax.ShapeDtypeStruct((bn, length, d), F32),
        grid=(bn + 1, steps),
        in_specs=smem + smem + [pl.BlockSpec((1, cap // 2) + PAIR_TILE, item4(k)) for k in range(n)]
                 + [tok_prev, pl.BlockSpec((1, 1, d), lambda b, e2: (prev(b), 0, 0)), _const_spec(w3.shape)],
        out_specs=tok_prev,
        scratch_shapes=[pltpu.VMEM((2, length) + ROW_TILE, F32)],
        compiler_params=_cparams(2, VMEM_LIMIT),
        name="expert_combine_residual",
    )(*([idx] * n + [vals] * n + [ye] * n), hx, g2, w3)


def kernel(x, c, ctx, c_ctx, w_mod, b_mod, norm_mix_pre, norm_mix_post, norm_ffn_pre, norm_ffn_post,
           w_in, conv_w, conv_b, dt_bias, a_log, d_skip, ssd_norm, w_fourier, w_ssd_out, b_gate, w_out,
           w_router, w_e_gate, w_e_up, w_e_down):
    bn, length, d = x.shape
    ctx_len = ctx.shape[1]
    l = 0
    row = lambda v: v.reshape(1, -1)

    cs = jnp.concatenate([c, c_ctx[None], jnp.zeros((7, d), F32)], axis=0)
    mods = _modulation(cs, w_mod[l], b_mod[l])
    mod_x = mods[:bn].reshape(bn, N_MOD, 1, d)
    mod_c = jnp.broadcast_to(mods[bn].reshape(1, N_MOD, 1, d), (bn, N_MOD, 1, d))
    sh1, sc1, g1, sh2, sc2, g2 = [mod_x[:, i] for i in range(N_MOD)]

    i1 = F_WIDTH
    i2 = i1 + D_INNER
    i3 = i2 + D_INNER
    i4 = i3 + BC_WIDTH
    i5 = i4 + BC_WIDTH
    i6 = i5 + 2 * N_HEADS
    wi = w_in[l]
    wdt = jnp.zeros((d, 256), F32).at[:, :N_HEADS].set(wi[:, i5:i5 + N_HEADS])
    wdt = wdt.at[:, 128:128 + N_HEADS].set(wi[:, i5 + N_HEADS:i6])
    dtb = jnp.zeros((1, 256), F32).at[0, :N_HEADS].set(dt_bias[l, 0]).at[0, 128:128 + N_HEADS].set(dt_bias[l, 1])
    dft_c, w1, w2 = _dft_constants()
    cw, cb = conv_w[l], conv_b[l]
    j1, j2 = D_INNER, D_INNER + BC_WIDTH
    w = dict(
        wuf=wi[:, :i1].astype(BF16), wz=wi[:, i1:i2].astype(BF16), wxs=wi[:, i2:i3].astype(BF16),
        wb=wi[:, i3:i4].astype(BF16), wc=wi[:, i4:i5].astype(BF16), wdt=wdt.astype(BF16),
        wg=wi[:, i6:].astype(BF16), dft=dft_c,
        cwx=cw[:, :j1], cbx=row(cb[:j1]), cwb=cw[:, j1:j2], cbb=row(cb[j1:j2]),
        cwc=cw[:, j2:], cbc=row(cb[j2:]), dtb=dtb, bg=row(b_gate[l]))
    gain_pre = row(norm_mix_pre[l])

    alog_pad = jnp.zeros((2, 1, 128), F32).at[:, 0, :N_HEADS].set(a_log[l])

    cxs, cbm, ccm, cdt = _in_projection(ctx, mod_c[:, 0], mod_c[:, 1], gain_pre, w, ctx_len, ctx_len, False)
    h_zero = jnp.zeros((bn, N_GROUPS, D_STATE, GROUP_COLS), F32)
    hc_f, hc_b = _ssd_scan(cxs, cbm, ccm, cdt, alog_pad, h_zero, h_zero, emit_y=False)

    zr, zi, z, xs, bm, cm, dt, gates = _in_projection(x, sh1, sc1, gain_pre, w, GRID_W, INPROJ_TILE, True)
    f = _sequence_dft_real(zr, zi, w1, w2)
    yf, _, yb, _ = _ssd_scan(xs, bm, cm, dt, alog_pad, hc_f, hc_b)

    consts = [row(jnp.repeat(d_skip[l], HEAD_DIM)), row(ssd_norm[l]), row(norm_mix_post[l]),
              row(norm_ffn_pre[l]), w_ssd_out[l].astype(BF16), w_fourier[l].astype(BF16),
              w_out[l].astype(BF16), jnp.pad(w_router[l], ((0, 0), (0, LANES - N_EXPERTS)))]
    hx, hm, aff_t = _mix(yf, yb, xs, z, f, gates, x, g1, sh2, sc2, consts, MIX_TILE)

    cap = EC_FACTOR * length // N_EXPERTS
    top_idx, top_aff = _route(aff_t, cap)
    idx = jnp.swapaxes(top_idx, 0, 1).reshape(N_EXPERTS * bn, 1, cap)
    vals = jnp.swapaxes(top_aff, 0, 1).reshape(N_EXPERTS * bn, 1, cap)
    ye = _experts(idx, hm, w_e_gate[l], w_e_up[l], w_e_down[l])
    return _combine_residual(idx, vals, ye, hx, g2, row(norm_ffn_post[l]))
```

```python
import functools

import numpy as np
import jax
import jax.numpy as jnp
from jax import lax
from jax.experimental import pallas as pl
from jax.experimental.pallas import tpu as pltpu

F32 = jnp.float32
BF16 = jnp.bfloat16
HIGHEST = lax.Precision.HIGHEST

D_MODEL = 1024
GRID_W = 64
F_GROUP_W = 128
F_WIDTH = 512
D_INNER = 1536
HEAD_DIM = 64
N_HEADS = 24
N_GROUPS = 4
HEADS_PER_GROUP = 6
D_STATE = 128
CHUNK = 128
BC_WIDTH = N_GROUPS * D_STATE
N_EXPERTS = 16
EC_FACTOR = 2
N_MOD = 6
RMS_EPS = 1e-6
GROUP_COLS = HEADS_PER_GROUP * HEAD_DIM

FFT_N1 = 128
FFT_N2 = 32
FFT_KB = 8

VMEM_LIMIT = 56 * 1024 * 1024
LANES = 128


def _cparams(n_axes, vmem=None):
    return pltpu.CompilerParams(dimension_semantics=("arbitrary",) * n_axes,
                                vmem_limit_bytes=vmem)


def _const_spec(shape):
    nd = len(shape)
    return pl.BlockSpec(shape, lambda *_: (0,) * nd, pipeline_mode=pl.Buffered(1))


def _rms(x, w):
    ms = jnp.mean(x * x, axis=-1, keepdims=True)
    return x * lax.rsqrt(ms + RMS_EPS) * w


def _sigmoid(x):
    return 1.0 / (1.0 + jnp.exp(-x))


def _silu(x):
    return x * _sigmoid(x)


ROW_TILE = (8, 128)
PAIR_TILE = (16, 128)


def _to_row_tiles(v):
    parts = [v[:, c * 128:(c + 1) * 128] for c in range(ROW_TILE[0])]
    return jnp.swapaxes(jnp.stack(parts, axis=0), 0, 1)


def _from_row_tiles(v3):
    t = jnp.swapaxes(v3, 0, 1)
    return jnp.concatenate([t[c] for c in range(ROW_TILE[0])], axis=1)


def _mod_kernel(c_ref, w_ref, b_ref, o_ref):
    c = c_ref[...]
    o_ref[...] = jnp.dot(_silu(c), w_ref[...], precision=HIGHEST,
                         preferred_element_type=F32) + b_ref[...]


def _modulation(cs, w_mod, b_mod):
    rows = cs.shape[0]
    n = w_mod.shape[1]
    blk = D_MODEL
    return pl.pallas_call(
        _mod_kernel,
        out_shape=jax.ShapeDtypeStruct((rows, n), F32),
        grid=(n // blk,),
        in_specs=[pl.BlockSpec((rows, D_MODEL), lambda j: (0, 0)),
                  pl.BlockSpec((D_MODEL, blk), lambda j: (0, j)),
                  pl.BlockSpec((1, blk), lambda j: (0, j))],
        out_specs=pl.BlockSpec((rows, blk), lambda j: (0, j)),
        compiler_params=_cparams(1),
        name="modulation",
    )(cs, w_mod, b_mod.reshape(1, n))


def _conv_silu(xbc, cw_ref, cb_ref, row_len):
    tm = xbc.shape[0]
    pos = lax.broadcasted_iota(jnp.int32, (tm, 1), 0) % row_len
    prev = jnp.where(pos == 0, 0.0, pltpu.roll(xbc, 1, axis=0))
    nxt = jnp.where(pos == row_len - 1, 0.0, pltpu.roll(xbc, tm - 1, axis=0))
    out = prev * cw_ref[0:1, :] + xbc * cw_ref[1:2, :] + nxt * cw_ref[2:3, :] + cb_ref[...]
    return _silu(out)


INPROJ_SUB = 128
INPROJ_TILE = 512


def _softplus(x):
    return jnp.maximum(x, 0.0) + jnp.log(1.0 + jnp.exp(-jnp.abs(x)))


def _inproj_kernel(x_ref, sh_ref, sc_ref, gain_ref, wuf_ref, wz_ref, wxs_ref, wb_ref, wc_ref,
                   wdt_ref, wg_ref, dft_ref, cwx_ref, cbx_ref, cwb_ref, cbb_ref, cwc_ref, cbc_ref,
                   dtb_ref, bg_ref,
                   zr_ref, zi_ref, z_ref, xs_ref, bm_ref, cm_ref, dt_ref, g_ref, *, row_len):
    dot = functools.partial(jnp.dot, preferred_element_type=F32)
    tm = x_ref.shape[1]
    sub = min(tm, max(INPROJ_SUB, row_len))
    assert sub % row_len == 0 and tm % sub == 0
    for r0 in range(0, tm, sub):
        rs = slice(r0, r0 + sub)
        h = _rms(x_ref[0, rs], gain_ref[...]) * (1.0 + sc_ref[0]) + sh_ref[0]
        hb = h.astype(BF16)
        xs_ref[0, rs] = _conv_silu(dot(hb, wxs_ref[...]), cwx_ref, cbx_ref, row_len).astype(BF16)
        bm_ref[0, rs] = _conv_silu(dot(hb, wb_ref[...]), cwb_ref, cbb_ref, row_len).astype(BF16)
        cm_ref[0, rs] = _conv_silu(dot(hb, wc_ref[...]), cwc_ref, cbc_ref, row_len).astype(BF16)
        dt_ref[0, rs] = _softplus(dot(hb, wdt_ref[...]) + dtb_ref[...])
        if zr_ref is None:
            continue
        uf = dot(hb, wuf_ref[...]).astype(BF16)
        for g in range(F_WIDTH // F_GROUP_W):
            sl = slice(g * F_GROUP_W, (g + 1) * F_GROUP_W)
            zz = dot(uf[:, sl], dft_ref[...].astype(BF16))
            zr_ref[0, rs, sl] = zz[:, :F_GROUP_W].astype(BF16)
            zi_ref[0, rs, sl] = zz[:, F_GROUP_W:].astype(BF16)
        z_ref[0, rs] = dot(hb, wz_ref[...]).astype(BF16)
        g_ref[0, rs] = _sigmoid(dot(hb, wg_ref[...]) + bg_ref[...]).astype(BF16)


def _inproj_ctx_kernel(x_ref, sh_ref, sc_ref, gain_ref, wxs_ref, wb_ref, wc_ref, wdt_ref,
                       cwx_ref, cbx_ref, cwb_ref, cbb_ref, cwc_ref, cbc_ref, dtb_ref,
                       xs_ref, bm_ref, cm_ref, dt_ref, *, row_len):
    _inproj_kernel(x_ref, sh_ref, sc_ref, gain_ref, None, None, wxs_ref, wb_ref, wc_ref,
                   wdt_ref, None, None, cwx_ref, cbx_ref, cwb_ref, cbb_ref, cwc_ref, cbc_ref,
                   dtb_ref, None, None, None, None, xs_ref, bm_ref, cm_ref, dt_ref, None,
                   row_len=row_len)


def _in_projection(x, shift, scale, gain, w, row_len, tm, full):
    bn, length, d = x.shape
    grid = (bn, length // tm)
    tok = lambda width: pl.BlockSpec((1, tm, width), lambda b, i: (b, i, 0))
    vec = pl.BlockSpec((1, 1, d), lambda b, i: (b, 0, 0))
    out = lambda width, dt: jax.ShapeDtypeStruct((bn, length, width), dt)
    conv_ops = [w["cwx"], w["cbx"], w["cwb"], w["cbb"], w["cwc"], w["cbc"]]
    ssd_shapes = [out(D_INNER, BF16), out(BC_WIDTH, BF16), out(BC_WIDTH, BF16), out(256, F32)]
    ssd_specs = [tok(D_INNER), tok(BC_WIDTH), tok(BC_WIDTH), tok(256)]
    if full:
        ops = [x, shift, scale, gain, w["wuf"], w["wz"], w["wxs"], w["wb"], w["wc"], w["wdt"], w["wg"],
               w["dft"]] + conv_ops + [w["dtb"], w["bg"]]
        kern = functools.partial(_inproj_kernel, row_len=row_len)
        out_shape = [out(F_WIDTH, BF16), out(F_WIDTH, BF16), out(D_INNER, BF16)] + ssd_shapes + \
                    [out(2 * D_MODEL, BF16)]
        out_specs = [tok(F_WIDTH), tok(F_WIDTH), tok(D_INNER)] + ssd_specs + [tok(2 * D_MODEL)]
        name = "in_projection"
    else:
        ops = [x, shift, scale, gain, w["wxs"], w["wb"], w["wc"], w["wdt"]] + conv_ops + [w["dtb"]]
        kern = functools.partial(_inproj_ctx_kernel, row_len=row_len)
        out_shape = ssd_shapes
        out_specs = ssd_specs
        name = "in_projection_ctx"
    in_specs = [tok(d), vec, vec] + [_const_spec(o.shape) for o in ops[3:]]
    return pl.pallas_call(kern, out_shape=out_shape, grid=grid, in_specs=in_specs,
                          out_specs=out_specs, compiler_params=_cparams(2, VMEM_LIMIT),
                          name=name)(*ops)


def _dft_constants():
    j = np.arange(F_GROUP_W)
    ang = 2.0 * np.pi * np.outer(j, j) / F_GROUP_W
    dft_c = np.concatenate([np.cos(ang), -np.sin(ang)], axis=1)
    k1 = np.arange(FFT_N1)
    n2 = np.arange(FFT_N2)
    phi = 2.0 * np.pi * k1[None, :, None] * (FFT_N2 * k1[None, None, :] + n2[:, None, None]) / (FFT_N1 * FFT_N2)
    c1, s1 = np.cos(phi), np.sin(phi)
    w1 = np.concatenate([np.concatenate([c1, s1], axis=2), np.concatenate([-s1, c1], axis=2)], axis=1)
    a2 = 2.0 * np.pi * np.outer(n2, n2) / FFT_N2
    w2 = np.zeros((FFT_N2, FFT_KB, 2, FFT_KB, FFT_N2))
    for jj in range(FFT_KB):
        w2[:, jj, 0, jj, :] = np.cos(a2)
        w2[:, jj, 1, jj, :] = np.sin(a2)
    w2 = w2.reshape(FFT_N2 * FFT_KB, 2 * FFT_KB * FFT_N2)
    return jnp.asarray(dft_c, F32), jnp.asarray(w1, F32), jnp.asarray(w2, F32)


FFT_QB = 16


def _seq_dft_kernel(zr_ref, zi_ref, w1_ref, w2_ref, o_ref, qr_s, qi_s, tr_s, ti_s, *, scale):
    for qb in range(FFT_N2 // FFT_QB):
        qs = slice(qb * FFT_QB, (qb + 1) * FFT_QB)
        zr_t = jnp.swapaxes(zr_ref[0, :, qs, :], 0, 1)
        zi_t = jnp.swapaxes(zi_ref[0, :, qs, :], 0, 1)
        for q in range(FFT_QB):
            n2 = qb * FFT_QB + q
            rhs = jnp.concatenate([zr_t[q], zi_t[q]], axis=0)
            t = jnp.dot(w1_ref[n2].astype(BF16), rhs, preferred_element_type=F32)
            qr_s[q] = t[:FFT_N1].astype(BF16)
            qi_s[q] = t[FFT_N1:].astype(BF16)
        tr_s[:, qs, :] = jnp.swapaxes(qr_s[...], 0, 1)
        ti_s[:, qs, :] = jnp.swapaxes(qi_s[...], 0, 1)
    w2 = w2_ref[...].astype(BF16)
    rows = FFT_KB * FFT_N2
    for kb in range(FFT_N1 // FFT_KB):
        ks = slice(kb * FFT_KB, (kb + 1) * FFT_KB)
        rhs = jnp.concatenate([tr_s[ks].reshape(rows, F_WIDTH), ti_s[ks].reshape(rows, F_WIDTH)], axis=0)
        y = jnp.dot(w2, rhs, preferred_element_type=F32) * scale
        o_ref[0, :, ks, :] = y.reshape(FFT_N2, FFT_KB, F_WIDTH)


def _sequence_dft_real(zr, zi, w1, w2):
    bn, length, width = zr.shape
    blk = pl.BlockSpec((1, FFT_N1, FFT_N2, width), lambda b: (b, 0, 0, 0))
    scale = 1.0 / float(np.sqrt(length * F_GROUP_W))
    f = pl.pallas_call(
        functools.partial(_seq_dft_kernel, scale=scale),
        out_shape=jax.ShapeDtypeStruct((bn, FFT_N2, FFT_N1, width), F32),
        grid=(bn,),
        in_specs=[blk, blk, _const_spec(w1.shape), _const_spec(w2.shape)],
        out_specs=pl.BlockSpec((1, FFT_N2, FFT_N1, width), lambda b: (b, 0, 0, 0)),
        scratch_shapes=[pltpu.VMEM((FFT_QB, FFT_N1, width), BF16)] * 2
                       + [pltpu.VMEM((FFT_N1, FFT_N2, width), BF16)] * 2,
        compiler_params=_cparams(1, VMEM_LIMIT),
        name="seq_dft",
    )(zr.reshape(bn, FFT_N1, FFT_N2, width), zi.reshape(bn, FFT_N1, FFT_N2, width), w1, w2)
    return f.reshape(bn, length, width)


SSD_SUB = 64
SSD_CHUNKS_PER_STEP = 4
LOG2E = 1.4426950408889634


class _Dir:
    pass


def _ssd_prepare(dt_ref, alog_ref, reverse, row0):
    q, sb = CHUNK, SSD_SUB
    d = _Dir()
    d.reverse = reverse
    r_i = lax.broadcasted_iota(jnp.int32, (q, q), 0)
    c_i = lax.broadcasted_iota(jnp.int32, (q, q), 1)
    tri = ((r_i <= c_i) if reverse else (r_i >= c_i)).astype(BF16)
    a = -jnp.exp(alog_ref[0])
    d.row0 = row0
    d.dt = dt_ref[0, row0:row0 + q]
    da = d.dt * (a * LOG2E)
    p0 = da.astype(BF16)
    r1 = da - p0.astype(F32)
    p1 = r1.astype(BF16)
    p2 = (r1 - p1.astype(F32)).astype(BF16)
    parts = jnp.dot(tri, jnp.concatenate([p0, p1, p2], axis=1), preferred_element_type=F32)
    d.acum = parts[:, :LANES] + (parts[:, LANES:2 * LANES] + parts[:, 2 * LANES:])
    lane = lax.broadcasted_iota(jnp.int32, (sb, 128), 1)
    d.lo = lane < HEAD_DIM
    lo_h = lax.broadcasted_iota(jnp.int32, (q, 128), 1) < HEAD_DIM
    row_l = lax.broadcasted_iota(jnp.int32, (sb, 128), 0)
    s_l = jnp.where(d.lo, lane, lane - HEAD_DIM)
    d.mask = (s_l >= row_l) if reverse else (s_l <= row_l)
    d.order = (1, 0) if reverse else (0, 1)

    def halves(m):
        m_t = m.T
        m_sw = pltpu.roll(m_t, sb, axis=1)
        return (jnp.where(lo_h, m_t, m_sw), jnp.where(lo_h, m_sw, m_t))

    d.rt = halves(d.acum)
    d.rt_dt = halves(d.dt)
    return d


def _ssd_block_rows(d, step):
    sb = SSD_SUB
    blk = d.order[step]
    end_r = blk * sb if d.reverse else blk * sb + sb - 1
    end_row = d.acum[end_r:end_r + 1, :]
    if step == 0:
        base_row = jnp.zeros((1, 128), F32)
    else:
        pr = d.order[0] * sb if d.reverse else d.order[0] * sb + sb - 1
        base_row = d.acum[pr:pr + 1, :]
    d.rows8 = jnp.concatenate([base_row, jnp.exp2(end_row - base_row), jnp.zeros((6, 128), F32)], axis=0)
    local = slice(blk * sb, (blk + 1) * sb)
    d.rs = slice(d.row0 + blk * sb, d.row0 + (blk + 1) * sb)
    d.blk = blk
    d.acum_b = d.acum[local]
    d.dw_b = d.dt[local] * jnp.exp2(end_row - d.acum_b)


def _ssd_group_begin(d, refs, g):
    xs_ref, bm_ref, cm_ref, y_ref, st_ref = refs
    gs = slice(g * D_STATE, (g + 1) * D_STATE)
    bg = bm_ref[0, d.rs, gs]
    cg = cm_ref[0, d.rs, gs]
    bg_t = bg.astype(F32).T.astype(BF16)
    h_t = st_ref[g]
    d.grp = dict(bg_t=bg_t, h_t=h_t, xw=[], cd=[])
    if y_ref is not None:
        d.grp["cb2"] = jnp.dot(cg, jnp.concatenate([bg_t, bg_t], axis=1), preferred_element_type=F32)
        d.grp["yoff"] = jnp.dot(cg, h_t.astype(BF16), preferred_element_type=F32)


def _ssd_pair(d, refs, g, pr_i):
    xs_ref, bm_ref, cm_ref, y_ref, st_ref = refs
    lo = d.lo
    lo_row = lo[0:1]
    h0 = g * HEADS_PER_GROUP + 2 * pr_i
    h1 = h0 + 1
    ps = slice((g * 3 + pr_i) * 128, (g * 3 + pr_i + 1) * 128)
    pat = jnp.where(lo, h0, h1)
    dwp = jnp.take_along_axis(d.dw_b, pat, axis=1)
    r8 = jnp.take_along_axis(d.rows8, pat[0:8], axis=1)
    base_p, cd_p = r8[0:1], r8[1:2]
    x = xs_ref[0, d.rs, ps]
    gd = d.grp
    gd["xw"].append((x.astype(F32) * dwp).astype(BF16))
    gd["cd"].append(cd_p)
    if y_ref is None:
        return
    colp = jnp.take_along_axis(d.acum_b, pat, axis=1)
    rt, rt_dt = d.rt[d.blk], d.rt_dt[d.blk]
    rowp = jnp.where(lo_row, rt[h0:h0 + 1, :], rt[h1:h1 + 1, :])
    dt_row = jnp.where(lo_row, rt_dt[h0:h0 + 1, :], rt_dt[h1:h1 + 1, :])
    m = (jnp.exp2(jnp.where(d.mask, colp - rowp, -1e30)) * (gd["cb2"] * dt_row)).astype(BF16)
    zero = jnp.zeros_like(x)
    rhs = jnp.concatenate([jnp.where(lo, x, zero), jnp.where(lo, zero, x)], axis=0)
    ydiag = jnp.dot(m, rhs, preferred_element_type=F32)
    y = ydiag + gd["yoff"][:, pr_i * 128:(pr_i + 1) * 128] * jnp.exp2(colp - base_p)
    y_ref[0, d.rs, ps] = y.astype(BF16)


def _ssd_group_end(d, refs, g):
    st_ref = refs[4]
    gd = d.grp
    xw = jnp.concatenate(gd["xw"], axis=1)
    cd = jnp.concatenate(gd["cd"], axis=1)
    st_ref[g] = gd["h_t"] * cd + jnp.dot(gd["bg_t"], xw, preferred_element_type=F32)


def _ssd_kernel(xf, bf, cf, df, af, hf, xb, bb, cb_, db, ab, hb, *out_and_scratch, nchunks, emit_y):
    if emit_y:
        yf, hff, yb, hfb, stf, stb = out_and_scratch
    else:
        (hff, hfb, stf, stb), yf, yb = out_and_scratch, None, None
    c = pl.program_id(1)

    @pl.when(c == 0)
    def _():
        stf[...] = hf[0]
        stb[...] = hb[0]

    refs = ((xf, bf, cf, yf, stf), (xb, bb, cb_, yb, stb))
    per_step = xf.shape[1] // CHUNK
    for ci in range(per_step):
        dirs = (_ssd_prepare(df, af, False, ci * CHUNK), _ssd_prepare(db, ab, True, (per_step - 1 - ci) * CHUNK))
        for step in range(CHUNK // SSD_SUB):
            for d in dirs:
                _ssd_block_rows(d, step)
            for g in range(N_GROUPS):
                for d, r in zip(dirs, refs):
                    _ssd_group_begin(d, r, g)
                for pr_i in range(HEADS_PER_GROUP // 2):
                    for d, r in zip(dirs, refs):
                        _ssd_pair(d, r, g, pr_i)
                for d, r in zip(dirs, refs):
                    _ssd_group_end(d, r, g)

    @pl.when(c == nchunks - 1)
    def _():
        hff[0] = stf[...]
        hfb[0] = stb[...]


def _ssd_scan(xs, bm, cm, dt, alog_pad, h0f, h0b, emit_y=True):
    bn, length, _ = xs.shape
    rows = CHUNK * min(SSD_CHUNKS_PER_STEP, length // CHUNK)
    nchunks = length // rows
    st_shape = (N_GROUPS, D_STATE, GROUP_COLS)
    st_spec = pl.BlockSpec((1,) + st_shape, lambda b, c: (b, 0, 0, 0))

    def specs(reverse):
        cc = (lambda c: nchunks - 1 - c) if reverse else (lambda c: c)
        di = 1 if reverse else 0
        tok = lambda width: pl.BlockSpec((1, rows, width), lambda b, c: (b, cc(c), 0))
        ins = [tok(D_INNER), tok(BC_WIDTH), tok(BC_WIDTH),
               pl.BlockSpec((1, rows, 128), lambda b, c: (b, cc(c), di)),
               pl.BlockSpec((1, 1, 128), lambda b, c: (di, 0, 0)), st_spec]
        return ins, ([tok(D_INNER)] if emit_y else []) + [st_spec]

    in_f, out_f = specs(False)
    in_b, out_b = specs(True)
    y_shape = [jax.ShapeDtypeStruct((bn, length, D_INNER), BF16)] if emit_y else []
    h_shape = [jax.ShapeDtypeStruct((bn,) + st_shape, F32)]
    return pl.pallas_call(
        functools.partial(_ssd_kernel, nchunks=nchunks, emit_y=emit_y),
        out_shape=y_shape + h_shape + y_shape + h_shape,
        grid=(bn, nchunks),
        in_specs=in_f + in_b,
        out_specs=out_f + out_b,
        scratch_shapes=[pltpu.VMEM(st_shape, F32), pltpu.VMEM(st_shape, F32)],
        compiler_params=_cparams(2),
        name="ssd_scan",
    )(xs, bm, cm, dt, alog_pad, h0f, xs, bm, cm, dt, alog_pad, h0b)


MIX_SUB = 128
MIX_TILE = 512


def _split_bf16(v):
    hi = v.astype(BF16)
    return hi, (v - hi.astype(F32)).astype(BF16)


def _mix_kernel(yf_ref, yb_ref, xs_ref, z_ref, f_ref, g_ref, x_ref, g1_ref, sh2_ref, sc2_ref,
                dsk_ref, nssd_ref, npost_ref, npre_ref, wso_ref, wf_ref, wo_ref, wr_ref,
                hx_ref, hm_ref, aff_ref):
    dot = functools.partial(jnp.dot, preferred_element_type=F32)
    wr_hi, wr_lo = _split_bf16(wr_ref[...])
    subs = [slice(r0, r0 + MIX_SUB) for r0 in range(0, x_ref.shape[1], MIX_SUB)]
    ys = []
    for rs in subs:
        y = (yf_ref[0, rs].astype(F32) + yb_ref[0, rs].astype(F32)
             + dsk_ref[...] * xs_ref[0, rs].astype(F32))
        ys.append(_rms(y * _silu(z_ref[0, rs].astype(F32)), nssd_ref[...]).astype(BF16))
    s_branches = [dot(y, wso_ref[...]) for y in ys]
    f_branches = [dot(f_ref[0, rs].astype(BF16), wf_ref[...]) for rs in subs]
    merged = []
    for rs, s_branch, f_branch in zip(subs, s_branches, f_branches):
        gates = g_ref[0, rs]
        merged.append(gates[:, :D_MODEL] * f_branch.astype(BF16) + gates[:, D_MODEL:] * s_branch.astype(BF16))
    mixes = [dot(m, wo_ref[...]) for m in merged]
    hms = []
    for rs, mix in zip(subs, mixes):
        hx = x_ref[0, rs] + g1_ref[0] * _rms(mix, npost_ref[...])
        hx_ref[0, rs] = hx
        hm = _rms(hx, npre_ref[...]) * (1.0 + sc2_ref[0]) + sh2_ref[0]
        hm_ref[0, rs] = _to_row_tiles(hm)
        hms.append(hm)
    for rs, hm in zip(subs, hms):
        hm_hi, hm_lo = _split_bf16(hm)
        logits = dot(hm_hi, wr_hi) + (dot(hm_hi, wr_lo) + dot(hm_lo, wr_hi))
        real = lax.broadcasted_iota(jnp.int32, logits.shape, 1) < N_EXPERTS
        logits = jnp.where(real, logits, -1e30)
        e = jnp.exp(logits - jnp.max(logits, axis=-1, keepdims=True))
        aff = e / jnp.sum(e, axis=-1, keepdims=True)
        aff_ref[0, :, rs] = aff.T[:N_EXPERTS]


def _mix(yf, yb, xs, z, f, gates, x, g1, sh2, sc2, consts, tm):
    bn, length, d = x.shape
    tok = lambda width: pl.BlockSpec((1, tm, width), lambda b, i: (b, i, 0))
    vec = pl.BlockSpec((1, 1, d), lambda b, i: (b, 0, 0))
    return pl.pallas_call(
        _mix_kernel,
        out_shape=[jax.ShapeDtypeStruct((bn, length, d), F32),
                   jax.ShapeDtypeStruct((bn, length) + ROW_TILE, F32),
                   jax.ShapeDtypeStruct((bn, N_EXPERTS, length), F32)],
        grid=(bn, length // tm),
        in_specs=[tok(D_INNER), tok(D_INNER), tok(D_INNER), tok(D_INNER), tok(F_WIDTH), tok(2 * D_MODEL), tok(d),
                  vec, vec, vec] + [_const_spec(c.shape) for c in consts],
        out_specs=[tok(d), pl.BlockSpec((1, tm) + ROW_TILE, lambda b, i: (b, i, 0, 0)),
                   pl.BlockSpec((1, N_EXPERTS, tm), lambda b, i: (b, 0, i))],
        compiler_params=_cparams(2, VMEM_LIMIT),
        name="merge_out_router",
    )(yf, yb, xs, z, f, gates, x, g1, sh2, sc2, *consts)


def _to_tile_rows(v):
    return jnp.concatenate([v[:, k * LANES:(k + 1) * LANES] for k in range(v.shape[1] // LANES)], axis=0)


def _route_kernel(aff_ref, idx_ref, val_ref, *, cap):
    I32 = jnp.int32
    a = aff_ref[0]
    ne, length = a.shape
    nt = length // LANES
    rows = nt * ne

    def search(i, t):
        cand = t | lax.shift_left(jnp.int32(1), 30 - i)
        cnt = jnp.sum((a >= pltpu.bitcast(cand, F32)).astype(F32), axis=1, keepdims=True)
        return jnp.where(cnt >= cap, cand, t)

    thr = pltpu.bitcast(lax.fori_loop(0, 31, search, jnp.zeros((ne, 1), I32)), F32)
    n_gt = jnp.sum((a > thr).astype(F32), axis=1, keepdims=True)
    need = cap - n_gt

    val_r = _to_tile_rows(a)
    tile_rows = lambda v: jnp.concatenate([v] * nt, axis=0)
    thr_r = tile_rows(thr)
    need_r = tile_rows(need)
    li = lax.broadcasted_iota(I32, (LANES, LANES), 0)
    lj = lax.broadcasted_iota(I32, (LANES, LANES), 1)
    upper = (li < lj).astype(BF16)
    ri = lax.broadcasted_iota(I32, (rows, rows), 0)
    rj = lax.broadcasted_iota(I32, (rows, rows), 1)
    earlier = (((ri & (ne - 1)) == (rj & (ne - 1))) & (rj < ri)).astype(BF16)

    def prefix(mask):
        local = jnp.dot(mask.astype(BF16), upper, preferred_element_type=F32)
        total = jnp.sum(mask.astype(F32), axis=1, keepdims=True)
        tot_b = jnp.broadcast_to(total, (rows, LANES)).astype(BF16)
        offs = jnp.dot(earlier, tot_b, preferred_element_type=F32)[:, 0:1]
        return local, offs

    eq = val_r == thr_r
    eq_local, eq_off = prefix(eq)
    sel = (val_r > thr_r) | (eq & (eq_local + eq_off < need_r))
    sel_local, sel_off = prefix(sel)

    lane = lax.broadcasted_iota(I32, (rows, LANES), 1)
    k_row = lax.shift_right_logical(lax.broadcasted_iota(I32, (rows, LANES), 0), int(np.log2(ne)))
    tok = jnp.where(sel, k_row * LANES + lane, -1)
    val = val_r
    dist = jnp.where(sel, lane - sel_local.astype(I32), 0)
    for s in range(int(np.log2(LANES))):
        sh = 1 << s
        mv = (tok >= 0) & ((lax.shift_right_logical(dist, s) & 1) == 1)
        in_tok = pltpu.roll(jnp.where(mv, tok, -1), LANES - sh, axis=1)
        in_val = pltpu.roll(val, LANES - sh, axis=1)
        in_dist = pltpu.roll(dist, LANES - sh, axis=1)
        arrive = in_tok >= 0
        stay = (tok >= 0) & jnp.logical_not(mv)
        tok = jnp.where(arrive, in_tok, jnp.where(stay, tok, -1))
        val = jnp.where(arrive, in_val, val)
        dist = jnp.where(arrive, in_dist, dist)

    off = sel_off.astype(I32)
    lane_e = lax.broadcasted_iota(I32, (ne, LANES), 1)
    ncol = cap // LANES
    out_tok = [jnp.zeros((ne, LANES), I32) for _ in range(ncol)]
    out_val = [jnp.zeros((ne, LANES), F32) for _ in range(ncol)]
    for k in range(nt):
        rs = slice(k * ne, (k + 1) * ne)
        o = off[rs]
        sh, col = o & (LANES - 1), lax.shift_right_logical(o, int(np.log2(LANES)))
        pat = (lane_e - sh) & (LANES - 1)
        r_tok = jnp.take_along_axis(tok[rs], pat, axis=1)
        r_val = jnp.take_along_axis(val[rs], pat, axis=1)
        ok = r_tok >= 0
        for c in range(ncol):
            here = ok & (((col == c) & (lane_e >= sh)) | ((col + 1 == c) & (lane_e < sh)))
            out_tok[c] = jnp.where(here, r_tok, out_tok[c])
            out_val[c] = jnp.where(here, r_val, out_val[c])
    idx_ref[0] = jnp.concatenate(out_tok, axis=1)
    val_ref[0] = jnp.concatenate(out_val, axis=1)


def _route(aff_t, cap):
    bn, ne, length = aff_t.shape
    assert ne & (ne - 1) == 0 and cap % LANES == 0 and length % LANES == 0
    return pl.pallas_call(
        functools.partial(_route_kernel, cap=cap),
        out_shape=[jax.ShapeDtypeStruct((bn, ne, cap), jnp.int32), jax.ShapeDtypeStruct((bn, ne, cap), F32)],
        grid=(bn,),
        in_specs=[pl.BlockSpec((1, ne, length), lambda b: (b, 0, 0))],
        out_specs=[pl.BlockSpec((1, ne, cap), lambda b: (b, 0, 0))] * 2,
        compiler_params=_cparams(1),
        name="expert_choice_route",
    )(aff_t)


def _expert_kernel(idx_ref, idxn_ref, hm_ref, wg_ref, wu_ref, wd_ref, ye_ref, buf0, buf1, wgb, wub, wdb, sem,
                   *, cap, nb, nsteps):
    s = pl.program_id(0)
    bufs = (buf0, buf1)
    dot = functools.partial(jnp.dot, preferred_element_type=F32)

    def row_copy(k, j, b, r):
        return pltpu.make_async_copy(hm_ref.at[b, pl.ds(r, 1)], bufs[k].at[pl.ds(j, 1)], sem.at[k])

    def wait_rows(k):
        pltpu.make_async_copy(hm_ref.at[0, pl.ds(0, cap)], bufs[k], sem.at[k]).wait()

    b0 = (2 * s) % nb

    @pl.when(b0 == 0)
    def _():
        wgb[...] = wg_ref[0].astype(BF16)
        wub[...] = wu_ref[0].astype(BF16)
        wdb[...] = wd_ref[0].astype(BF16)

    @pl.when(s == 0)
    def _():
        def issue(j, carry):
            row_copy(0, j, b0, idx_ref[0, 0, j]).start()
            return carry
        lax.fori_loop(0, cap, issue, 0)

    for k in range(2):
        wait_rows(k)
        x = _from_row_tiles(bufs[k][...]).astype(BF16)
        if k == 0:
            for j in range(cap):
                row_copy(1, j, b0 + 1, idx_ref[1, 0, j]).start(priority=j % 2)
            for j in range(cap):
                row_copy(0, j, (b0 + 2) % nb, idxn_ref[0, 0, j]).start(priority=j % 2)
        hid = (_silu(dot(x, wgb[...])) * dot(x, wub[...])).astype(BF16)
        ye_ref[k] = _to_row_tiles(dot(hid, wdb[...])).astype(BF16).reshape((cap // 2,) + PAIR_TILE)

    @pl.when(s == nsteps - 1)
    def _():
        wait_rows(0)


def _experts(idx, hm, wg, wu, wd):
    bn = hm.shape[0]
    ne, d, ff = wg.shape
    cap = idx.shape[-1]
    per = bn // 2
    nsteps = ne * per
    cur = lambda s: (s, 0, 0)
    nxt = lambda s: ((s + 1) % nsteps, 0, 0)
    wspec = pl.BlockSpec((1, d, ff), lambda s: (s // per, 0, 0))
    return pl.pallas_call(
        functools.partial(_expert_kernel, cap=cap, nb=bn, nsteps=nsteps),
        out_shape=jax.ShapeDtypeStruct((ne * bn, cap // 2) + PAIR_TILE, BF16),
        grid=(nsteps,),
        in_specs=[pl.BlockSpec((2, 1, cap), cur, memory_space=pltpu.SMEM),
                  pl.BlockSpec((2, 1, cap), nxt, memory_space=pltpu.SMEM),
                  pl.BlockSpec(memory_space=pl.ANY),
                  wspec, wspec, pl.BlockSpec((1, ff, d), lambda s: (s // per, 0, 0))],
        out_specs=pl.BlockSpec((2, cap // 2) + PAIR_TILE, lambda s: (s, 0, 0, 0)),
        scratch_shapes=[pltpu.VMEM((cap,) + ROW_TILE, F32), pltpu.VMEM((cap,) + ROW_TILE, F32),
                        pltpu.VMEM((d, ff), BF16), pltpu.VMEM((d, ff), BF16), pltpu.VMEM((ff, d), BF16),
                        pltpu.SemaphoreType.DMA((2,))],
        compiler_params=_cparams(1, VMEM_LIMIT),
        name="expert_ffn",
    )(idx, idx, hm, wg, wu, wd)


COMBINE_ROWS = 16


COMBINE_EXPERTS = 2


def _combine_kernel(*refs, cap, nb, rows_out):
    n = COMBINE_EXPERTS
    idx_refs, val_refs, ye_refs = refs[:n], refs[n:2 * n], refs[2 * n:3 * n]
    hx_ref, g2_ref, w_ref, o_ref, acc = refs[3 * n:]
    b = pl.program_id(0)
    e2 = pl.program_id(1)
    slot = b % 2
    acc_cur = acc.at[slot]

    @pl.when(jnp.logical_and(e2 == 0, b < nb))
    def _():
        acc_cur[...] = jnp.zeros(acc.shape[1:], F32)

    @pl.when(b < nb)
    def _():
        for idx_ref, val_ref, ye_ref in zip(idx_refs, val_refs, ye_refs):
            def body(i, carry, idx_ref=idx_ref, val_ref=val_ref, ye_ref=ye_ref):
                j0 = i * COMBINE_ROWS
                rows = [idx_ref[0, 0, j0 + u] for u in range(COMBINE_ROWS)]
                pairs = [ye_ref[0, i * (COMBINE_ROWS // 2) + p].astype(F32) for p in range(COMBINE_ROWS // 2)]
                ye = [pairs[u // 2][(u % 2) * ROW_TILE[0]:(u % 2 + 1) * ROW_TILE[0]] for u in range(COMBINE_ROWS)]
                new = [acc_cur[rows[u]] + ye[u] * val_ref[0, 0, j0 + u] for u in range(COMBINE_ROWS)]
                for u in range(COMBINE_ROWS):
                    acc_cur[rows[u]] = new[u]
                return carry

            lax.fori_loop(0, cap // COMBINE_ROWS, body, 0)

    @pl.when(b > 0)
    def _():
        f3 = acc[1 - slot, pl.ds(pl.multiple_of(e2 * rows_out, rows_out), rows_out)]
        ms = jnp.mean(f3 * f3, axis=(1, 2), keepdims=True)
        y = _from_row_tiles(f3 * lax.rsqrt(ms + RMS_EPS) * w_ref[...])
        o_ref[0] = hx_ref[0] + g2_ref[0] * y


def _combine_residual(idx, vals, ye, hx, g2, w):
    bn, length, d = hx.shape
    n = COMBINE_EXPERTS
    steps = N_EXPERTS // n
    cap = idx.shape[-1]
    rows_out = length // steps
    w3 = w.reshape((1,) + ROW_TILE)
    cur = lambda b: jnp.minimum(b, bn - 1)
    prev = lambda b: jnp.maximum(b - 1, 0)
    item = lambda k: (lambda b, e2: ((e2 * n + k) * bn + cur(b), 0, 0))
    item4 = lambda k: (lambda b, e2: ((e2 * n + k) * bn + cur(b), 0, 0, 0))
    smem = [pl.BlockSpec((1, 1, cap), item(k), memory_space=pltpu.SMEM) for k in range(n)]
    tok_prev = pl.BlockSpec((1, rows_out, d), lambda b, e2: (prev(b), jnp.where(b > 0, e2, 0), 0))
    return pl.pallas_call(
        functools.partial(_combine_kernel, cap=cap, nb=bn, rows_out=rows_out),
        out_shape=jax.ShapeDtypeStruct((bn, length, d), F32),
        grid=(bn + 1, steps),
        in_specs=smem + smem + [pl.BlockSpec((1, cap // 2) + PAIR_TILE, item4(k)) for k in range(n)]
                 + [tok_prev, pl.BlockSpec((1, 1, d), lambda b, e2: (prev(b), 0, 0)), _const_spec(w3.shape)],
        out_specs=tok_prev,
        scratch_shapes=[pltpu.VMEM((2, length) + ROW_TILE, F32)],
        compiler_params=_cparams(2, VMEM_LIMIT),
        name="expert_combine_residual",
    )(*([idx] * n + [vals] * n + [ye] * n), hx, g2, w3)


def kernel(x, c, ctx, c_ctx, w_mod, b_mod, norm_mix_pre, norm_mix_post, norm_ffn_pre, norm_ffn_post,
           w_in, conv_w, conv_b, dt_bias, a_log, d_skip, ssd_norm, w_fourier, w_ssd_out, b_gate, w_out,
           w_router, w_e_gate, w_e_up, w_e_down):
    bn, length, d = x.shape
    ctx_len = ctx.shape[1]
    l = 0
    row = lambda v: v.reshape(1, -1)

    cs = jnp.concatenate([c, c_ctx[None], jnp.zeros((7, d), F32)], axis=0)
    mods = _modulation(cs, w_mod[l], b_mod[l])
    mod_x = mods[:bn].reshape(bn, N_MOD, 1, d)
    mod_c = jnp.broadcast_to(mods[bn].reshape(1, N_MOD, 1, d), (bn, N_MOD, 1, d))
    sh1, sc1, g1, sh2, sc2, g2 = [mod_x[:, i] for i in range(N_MOD)]

    i1 = F_WIDTH
    i2 = i1 + D_INNER
    i3 = i2 + D_INNER
    i4 = i3 + BC_WIDTH
    i5 = i4 + BC_WIDTH
    i6 = i5 + 2 * N_HEADS
    wi = w_in[l]
    wdt = jnp.zeros((d, 256), F32).at[:, :N_HEADS].set(wi[:, i5:i5 + N_HEADS])
    wdt = wdt.at[:, 128:128 + N_HEADS].set(wi[:, i5 + N_HEADS:i6])
    dtb = jnp.zeros((1, 256), F32).at[0, :N_HEADS].set(dt_bias[l, 0]).at[0, 128:128 + N_HEADS].set(dt_bias[l, 1])
    dft_c, w1, w2 = _dft_constants()
    cw, cb = conv_w[l], conv_b[l]
    j1, j2 = D_INNER, D_INNER + BC_WIDTH
    w = dict(
        wuf=wi[:, :i1].astype(BF16), wz=wi[:, i1:i2].astype(BF16), wxs=wi[:, i2:i3].astype(BF16),
        wb=wi[:, i3:i4].astype(BF16), wc=wi[:, i4:i5].astype(BF16), wdt=wdt.astype(BF16),
        wg=wi[:, i6:].astype(BF16), dft=dft_c,
        cwx=cw[:, :j1], cbx=row(cb[:j1]), cwb=cw[:, j1:j2], cbb=row(cb[j1:j2]),
        cwc=cw[:, j2:], cbc=row(cb[j2:]), dtb=dtb, bg=row(b_gate[l]))
    gain_pre = row(norm_mix_pre[l])

    alog_pad = jnp.zeros((2, 1, 128), F32).at[:, 0, :N_HEADS].set(a_log[l])

    cxs, cbm, ccm, cdt = _in_projection(ctx, mod_c[:, 0], mod_c[:, 1], gain_pre, w, ctx_len, ctx_len, False)
    h_zero = jnp.zeros((bn, N_GROUPS, D_STATE, GROUP_COLS), F32)
    hc_f, hc_b = _ssd_scan(cxs, cbm, ccm, cdt, alog_pad, h_zero, h_zero, emit_y=False)

    zr, zi, z, xs, bm, cm, dt, gates = _in_projection(x, sh1, sc1, gain_pre, w, GRID_W, INPROJ_TILE, True)
    f = _sequence_dft_real(zr, zi, w1, w2)
    yf, _, yb, _ = _ssd_scan(xs, bm, cm, dt, alog_pad, hc_f, hc_b)

    consts = [row(jnp.repeat(d_skip[l], HEAD_DIM)), row(ssd_norm[l]), row(norm_mix_post[l]),
              row(norm_ffn_pre[l]), w_ssd_out[l].astype(BF16), w_fourier[l].astype(BF16),
              w_out[l].astype(BF16), jnp.pad(w_router[l], ((0, 0), (0, LANES - N_EXPERTS)))]
    hx, hm, aff_t = _mix(yf, yb, xs, z, f, gates, x, g1, sh2, sc2, consts, MIX_TILE)

    cap = EC_FACTOR * length // N_EXPERTS
    top_idx, top_aff = _route(aff_t, cap)
    idx = jnp.swapaxes(top_idx, 0, 1).reshape(N_EXPERTS * bn, 1, cap)
    vals = jnp.swapaxes(top_aff, 0, 1).reshape(N_EXPERTS * bn, 1, cap)
    ye = _experts(idx, hm, w_e_gate[l], w_e_up[l], w_e_down[l])
    return _combine_residual(idx, vals, ye, hx, g2, row(norm_ffn_post[l]))
```

```python
import functools

import numpy as np
import jax
import jax.numpy as jnp
from jax import lax
from jax.experimental import pallas as pl
from jax.experimental.pallas import tpu as pltpu

F32 = jnp.float32
BF16 = jnp.bfloat16
HIGHEST = lax.Precision.HIGHEST

D_MODEL = 1024
GRID_W = 64
F_GROUP_W = 128
F_WIDTH = 512
D_INNER = 1536
HEAD_DIM = 64
N_HEADS = 24
N_GROUPS = 4
HEADS_PER_GROUP = 6
D_STATE = 128
CHUNK = 128
BC_WIDTH = N_GROUPS * D_STATE
N_EXPERTS = 16
EC_FACTOR = 2
N_MOD = 6
RMS_EPS = 1e-6
GROUP_COLS = HEADS_PER_GROUP * HEAD_DIM

FFT_N1 = 128
FFT_N2 = 32
FFT_KB = 8

VMEM_LIMIT = 56 * 1024 * 1024
LANES = 128


def _cparams(n_axes, vmem=None):
    return pltpu.CompilerParams(dimension_semantics=("arbitrary",) * n_axes,
                                vmem_limit_bytes=vmem)


def _const_spec(shape):
    nd = len(shape)
    return pl.BlockSpec(shape, lambda *_: (0,) * nd, pipeline_mode=pl.Buffered(1))


def _rms(x, w):
    ms = jnp.mean(x * x, axis=-1, keepdims=True)
    return x * lax.rsqrt(ms + RMS_EPS) * w


def _sigmoid(x):
    return 1.0 / (1.0 + jnp.exp(-x))


def _silu(x):
    return x * _sigmoid(x)


ROW_TILE = (8, 128)
PAIR_TILE = (16, 128)


def _to_row_tiles(v):
    parts = [v[:, c * 128:(c + 1) * 128] for c in range(ROW_TILE[0])]
    return jnp.swapaxes(jnp.stack(parts, axis=0), 0, 1)


def _from_row_tiles(v3):
    t = jnp.swapaxes(v3, 0, 1)
    return jnp.concatenate([t[c] for c in range(ROW_TILE[0])], axis=1)


def _mod_kernel(c_ref, w_ref, b_ref, o_ref):
    c = c_ref[...]
    o_ref[...] = jnp.dot(_silu(c), w_ref[...], precision=HIGHEST,
                         preferred_element_type=F32) + b_ref[...]


def _modulation(cs, w_mod, b_mod):
    rows = cs.shape[0]
    n = w_mod.shape[1]
    blk = D_MODEL
    return pl.pallas_call(
        _mod_kernel,
        out_shape=jax.ShapeDtypeStruct((rows, n), F32),
        grid=(n // blk,),
        in_specs=[pl.BlockSpec((rows, D_MODEL), lambda j: (0, 0)),
                  pl.BlockSpec((D_MODEL, blk), lambda j: (0, j)),
                  pl.BlockSpec((1, blk), lambda j: (0, j))],
        out_specs=pl.BlockSpec((rows, blk), lambda j: (0, j)),
        compiler_params=_cparams(1),
        name="modulation",
    )(cs, w_mod, b_mod.reshape(1, n))


def _conv_silu(xbc, cw_ref, cb_ref, row_len):
    tm = xbc.shape[0]
    pos = lax.broadcasted_iota(jnp.int32, (tm, 1), 0) % row_len
    prev = jnp.where(pos == 0, 0.0, pltpu.roll(xbc, 1, axis=0))
    nxt = jnp.where(pos == row_len - 1, 0.0, pltpu.roll(xbc, tm - 1, axis=0))
    out = prev * cw_ref[0:1, :] + xbc * cw_ref[1:2, :] + nxt * cw_ref[2:3, :] + cb_ref[...]
    return _silu(out)


INPROJ_SUB = 128
INPROJ_TILE = 512


def _softplus(x):
    return jnp.maximum(x, 0.0) + jnp.log(1.0 + jnp.exp(-jnp.abs(x)))


def _inproj_kernel(x_ref, sh_ref, sc_ref, gain_ref, wuf_ref, wz_ref, wxs_ref, wb_ref, wc_ref,
                   wdt_ref, wg_ref, dft_ref, cwx_ref, cbx_ref, cwb_ref, cbb_ref, cwc_ref, cbc_ref,
                   dtb_ref, bg_ref,
                   zr_ref, zi_ref, z_ref, xs_ref, bm_ref, cm_ref, dt_ref, g_ref, *, row_len):
    dot = functools.partial(jnp.dot, preferred_element_type=F32)
    tm = x_ref.shape[1]
    sub = min(tm, max(INPROJ_SUB, row_len))
    assert sub % row_len == 0 and tm % sub == 0
    for r0 in range(0, tm, sub):
        rs = slice(r0, r0 + sub)
        h = _rms(x_ref[0, rs], gain_ref[...]) * (1.0 + sc_ref[0]) + sh_ref[0]
        hb = h.astype(BF16)
        xs_ref[0, rs] = _conv_silu(dot(hb, wxs_ref[...]), cwx_ref, cbx_ref, row_len).astype(BF16)
        bm_ref[0, rs] = _conv_silu(dot(hb, wb_ref[...]), cwb_ref, cbb_ref, row_len).astype(BF16)
        cm_ref[0, rs] = _conv_silu(dot(hb, wc_ref[...]), cwc_ref, cbc_ref, row_len).astype(BF16)
        dt_ref[0, rs] = _softplus(dot(hb, wdt_ref[...]) + dtb_ref[...])
        if zr_ref is None:
            continue
        uf = dot(hb, wuf_ref[...]).astype(BF16)
        for g in range(F_WIDTH // F_GROUP_W):
            sl = slice(g * F_GROUP_W, (g + 1) * F_GROUP_W)
            zz = dot(uf[:, sl], dft_ref[...].astype(BF16))
            zr_ref[0, rs, sl] = zz[:, :F_GROUP_W].astype(BF16)
            zi_ref[0, rs, sl] = zz[:, F_GROUP_W:].astype(BF16)
        z_ref[0, rs] = _silu(dot(hb, wz_ref[...])).astype(BF16)
        g_ref[0, rs] = _sigmoid(dot(hb, wg_ref[...]) + bg_ref[...]).astype(BF16)


def _inproj_ctx_kernel(x_ref, sh_ref, sc_ref, gain_ref, wxs_ref, wb_ref, wc_ref, wdt_ref,
                       cwx_ref, cbx_ref, cwb_ref, cbb_ref, cwc_ref, cbc_ref, dtb_ref,
                       xs_ref, bm_ref, cm_ref, dt_ref, *, row_len):
    _inproj_kernel(x_ref, sh_ref, sc_ref, gain_ref, None, None, wxs_ref, wb_ref, wc_ref,
                   wdt_ref, None, None, cwx_ref, cbx_ref, cwb_ref, cbb_ref, cwc_ref, cbc_ref,
                   dtb_ref, None, None, None, None, xs_ref, bm_ref, cm_ref, dt_ref, None,
                   row_len=row_len)


def _in_projection(x, shift, scale, gain, w, row_len, tm, full):
    bn, length, d = x.shape
    grid = (bn, length // tm)
    tok = lambda width: pl.BlockSpec((1, tm, width), lambda b, i: (b, i, 0))
    vec = pl.BlockSpec((1, 1, d), lambda b, i: (b, 0, 0))
    out = lambda width, dt: jax.ShapeDtypeStruct((bn, length, width), dt)
    conv_ops = [w["cwx"], w["cbx"], w["cwb"], w["cbb"], w["cwc"], w["cbc"]]
    ssd_shapes = [out(D_INNER, BF16), out(BC_WIDTH, BF16), out(BC_WIDTH, BF16), out(256, F32)]
    ssd_specs = [tok(D_INNER), tok(BC_WIDTH), tok(BC_WIDTH), tok(256)]
    if full:
        ops = [x, shift, scale, gain, w["wuf"], w["wz"], w["wxs"], w["wb"], w["wc"], w["wdt"], w["wg"],
               w["dft"]] + conv_ops + [w["dtb"], w["bg"]]
        kern = functools.partial(_inproj_kernel, row_len=row_len)
        out_shape = [out(F_WIDTH, BF16), out(F_WIDTH, BF16), out(D_INNER, BF16)] + ssd_shapes + \
                    [out(2 * D_MODEL, BF16)]
        out_specs = [tok(F_WIDTH), tok(F_WIDTH), tok(D_INNER)] + ssd_specs + [tok(2 * D_MODEL)]
        name = "in_projection"
    else:
        ops = [x, shift, scale, gain, w["wxs"], w["wb"], w["wc"], w["wdt"]] + conv_ops + [w["dtb"]]
        kern = functools.partial(_inproj_ctx_kernel, row_len=row_len)
        out_shape = ssd_shapes
        out_specs = ssd_specs
        name = "in_projection_ctx"
    in_specs = [tok(d), vec, vec] + [_const_spec(o.shape) for o in ops[3:]]
    return pl.pallas_call(kern, out_shape=out_shape, grid=grid, in_specs=in_specs,
                          out_specs=out_specs, compiler_params=_cparams(2, VMEM_LIMIT),
                          name=name)(*ops)


def _dft_constants():
    j = np.arange(F_GROUP_W)
    ang = 2.0 * np.pi * np.outer(j, j) / F_GROUP_W
    dft_c = np.concatenate([np.cos(ang), -np.sin(ang)], axis=1)
    k1 = np.arange(FFT_N1)
    n2 = np.arange(FFT_N2)
    phi = 2.0 * np.pi * k1[None, :, None] * (FFT_N2 * k1[None, None, :] + n2[:, None, None]) / (FFT_N1 * FFT_N2)
    c1, s1 = np.cos(phi), np.sin(phi)
    w1 = np.concatenate([np.concatenate([c1, s1], axis=2), np.concatenate([-s1, c1], axis=2)], axis=1)
    a2 = 2.0 * np.pi * np.outer(n2, n2) / FFT_N2
    w2 = np.zeros((FFT_N2, FFT_KB, 2, FFT_KB, FFT_N2))
    for jj in range(FFT_KB):
        w2[:, jj, 0, jj, :] = np.cos(a2)
        w2[:, jj, 1, jj, :] = np.sin(a2)
    w2 = w2.reshape(FFT_N2 * FFT_KB, 2 * FFT_KB * FFT_N2)
    return jnp.asarray(dft_c, F32), jnp.asarray(w1, F32), jnp.asarray(w2, F32)


FFT_QB = 16


def _seq_dft_kernel(zr_ref, zi_ref, w1_ref, w2_ref, o_ref, qr_s, qi_s, tr_s, ti_s, *, scale):
    for qb in range(FFT_N2 // FFT_QB):
        qs = slice(qb * FFT_QB, (qb + 1) * FFT_QB)
        zr_t = jnp.swapaxes(zr_ref[0, :, qs, :], 0, 1)
        zi_t = jnp.swapaxes(zi_ref[0, :, qs, :], 0, 1)
        for q in range(FFT_QB):
            n2 = qb * FFT_QB + q
            rhs = jnp.concatenate([zr_t[q], zi_t[q]], axis=0)
            t = jnp.dot(w1_ref[n2].astype(BF16), rhs, preferred_element_type=F32)
            qr_s[q] = t[:FFT_N1].astype(BF16)
            qi_s[q] = t[FFT_N1:].astype(BF16)
        tr_s[:, qs, :] = jnp.swapaxes(qr_s[...], 0, 1)
        ti_s[:, qs, :] = jnp.swapaxes(qi_s[...], 0, 1)
    w2 = w2_ref[...].astype(BF16)
    rows = FFT_KB * FFT_N2
    for kb in range(FFT_N1 // FFT_KB):
        ks = slice(kb * FFT_KB, (kb + 1) * FFT_KB)
        rhs = jnp.concatenate([tr_s[ks].reshape(rows, F_WIDTH), ti_s[ks].reshape(rows, F_WIDTH)], axis=0)
        y = jnp.dot(w2, rhs, preferred_element_type=F32) * scale
        o_ref[0, :, ks, :] = y.reshape(FFT_N2, FFT_KB, F_WIDTH)


def _sequence_dft_real(zr, zi, w1, w2):
    bn, length, width = zr.shape
    blk = pl.BlockSpec((1, FFT_N1, FFT_N2, width), lambda b: (b, 0, 0, 0))
    scale = 1.0 / float(np.sqrt(length * F_GROUP_W))
    f = pl.pallas_call(
        functools.partial(_seq_dft_kernel, scale=scale),
        out_shape=jax.ShapeDtypeStruct((bn, FFT_N2, FFT_N1, width), F32),
        grid=(bn,),
        in_specs=[blk, blk, _const_spec(w1.shape), _const_spec(w2.shape)],
        out_specs=pl.BlockSpec((1, FFT_N2, FFT_N1, width), lambda b: (b, 0, 0, 0)),
        scratch_shapes=[pltpu.VMEM((FFT_QB, FFT_N1, width), BF16)] * 2
                       + [pltpu.VMEM((FFT_N1, FFT_N2, width), BF16)] * 2,
        compiler_params=_cparams(1, VMEM_LIMIT),
        name="seq_dft",
    )(zr.reshape(bn, FFT_N1, FFT_N2, width), zi.reshape(bn, FFT_N1, FFT_N2, width), w1, w2)
    return f.reshape(bn, length, width)


SSD_SUB = 64
SSD_CHUNKS_PER_STEP = 4
LOG2E = 1.4426950408889634


class _Dir:
    pass


def _ssd_prepare(dt_ref, alog_ref, reverse, row0):
    q, sb = CHUNK, SSD_SUB
    d = _Dir()
    d.reverse = reverse
    r_i = lax.broadcasted_iota(jnp.int32, (q, q), 0)
    c_i = lax.broadcasted_iota(jnp.int32, (q, q), 1)
    tri = ((r_i <= c_i) if reverse else (r_i >= c_i)).astype(BF16)
    a = -jnp.exp(alog_ref[0])
    d.row0 = row0
    d.dt = dt_ref[0, row0:row0 + q]
    da = d.dt * (a * LOG2E)
    p0 = da.astype(BF16)
    r1 = da - p0.astype(F32)
    p1 = r1.astype(BF16)
    p2 = (r1 - p1.astype(F32)).astype(BF16)
    parts = jnp.dot(tri, jnp.concatenate([p0, p1, p2], axis=1), preferred_element_type=F32)
    d.acum = parts[:, :LANES] + (parts[:, LANES:2 * LANES] + parts[:, 2 * LANES:])
    lane = lax.broadcasted_iota(jnp.int32, (sb, 128), 1)
    d.lo = lane < HEAD_DIM
    lo_h = lax.broadcasted_iota(jnp.int32, (q, 128), 1) < HEAD_DIM
    row_l = lax.broadcasted_iota(jnp.int32, (sb, 128), 0)
    s_l = jnp.where(d.lo, lane, lane - HEAD_DIM)
    d.mask = (s_l >= row_l) if reverse else (s_l <= row_l)
    d.order = (1, 0) if reverse else (0, 1)

    def halves(m):
        m_t = m.T
        m_sw = pltpu.roll(m_t, sb, axis=1)
        return (jnp.where(lo_h, m_t, m_sw), jnp.where(lo_h, m_sw, m_t))

    d.rt = halves(d.acum)
    d.rt_dt = halves(d.dt)
    return d


def _ssd_block_rows(d, step):
    sb = SSD_SUB
    blk = d.order[step]
    end_r = blk * sb if d.reverse else blk * sb + sb - 1
    end_row = d.acum[end_r:end_r + 1, :]
    if step == 0:
        base_row = jnp.zeros((1, 128), F32)
    else:
        pr = d.order[0] * sb if d.reverse else d.order[0] * sb + sb - 1
        base_row = d.acum[pr:pr + 1, :]
    d.rows8 = jnp.concatenate([base_row, jnp.exp2(end_row - base_row), jnp.zeros((6, 128), F32)], axis=0)
    local = slice(blk * sb, (blk + 1) * sb)
    d.rs = slice(d.row0 + blk * sb, d.row0 + (blk + 1) * sb)
    d.blk = blk
    d.acum_b = d.acum[local]
    d.dw_b = d.dt[local] * jnp.exp2(end_row - d.acum_b)


def _ssd_group_begin(d, refs, g):
    xs_ref, bm_ref, cm_ref, y_ref, st_ref = refs
    gs = slice(g * D_STATE, (g + 1) * D_STATE)
    bg = bm_ref[0, d.rs, gs]
    cg = cm_ref[0, d.rs, gs]
    bg_t = bg.astype(F32).T.astype(BF16)
    h_t = st_ref[g]
    d.grp = dict(bg_t=bg_t, h_t=h_t, xw=[], cd=[])
    if y_ref is not None:
        d.grp["cb2"] = jnp.dot(cg, jnp.concatenate([bg_t, bg_t], axis=1), preferred_element_type=F32)
        d.grp["yoff"] = jnp.dot(cg, h_t.astype(BF16), preferred_element_type=F32)


def _ssd_pair(d, refs, g, pr_i):
    xs_ref, bm_ref, cm_ref, y_ref, st_ref = refs
    lo = d.lo
    lo_row = lo[0:1]
    h0 = g * HEADS_PER_GROUP + 2 * pr_i
    h1 = h0 + 1
    ps = slice((g * 3 + pr_i) * 128, (g * 3 + pr_i + 1) * 128)
    pat = jnp.where(lo, h0, h1)
    dwp = jnp.take_along_axis(d.dw_b, pat, axis=1)
    r8 = jnp.take_along_axis(d.rows8, pat[0:8], axis=1)
    base_p, cd_p = r8[0:1], r8[1:2]
    x = xs_ref[0, d.rs, ps]
    gd = d.grp
    gd["xw"].append((x.astype(F32) * dwp).astype(BF16))
    gd["cd"].append(cd_p)
    if y_ref is None:
        return
    colp = jnp.take_along_axis(d.acum_b, pat, axis=1)
    rt, rt_dt = d.rt[d.blk], d.rt_dt[d.blk]
    rowp = jnp.where(lo_row, rt[h0:h0 + 1, :], rt[h1:h1 + 1, :])
    dt_row = jnp.where(lo_row, rt_dt[h0:h0 + 1, :], rt_dt[h1:h1 + 1, :])
    m = (jnp.exp2(jnp.where(d.mask, colp - rowp, -1e30)) * (gd["cb2"] * dt_row)).astype(BF16)
    zero = jnp.zeros_like(x)
    rhs = jnp.concatenate([jnp.where(lo, x, zero), jnp.where(lo, zero, x)], axis=0)
    ydiag = jnp.dot(m, rhs, preferred_element_type=F32)
    y = ydiag + gd["yoff"][:, pr_i * 128:(pr_i + 1) * 128] * jnp.exp2(colp - base_p)
    y_ref[0, d.rs, ps] = y.astype(BF16)


def _ssd_group_end(d, refs, g):
    st_ref = refs[4]
    gd = d.grp
    xw = jnp.concatenate(gd["xw"], axis=1)
    cd = jnp.concatenate(gd["cd"], axis=1)
    st_ref[g] = gd["h_t"] * cd + jnp.dot(gd["bg_t"], xw, preferred_element_type=F32)


def _ssd_kernel(xf, bf, cf, df, af, hf, xb, bb, cb_, db, ab, hb, *out_and_scratch, nchunks, emit_y):
    if emit_y:
        yf, hff, yb, hfb, stf, stb = out_and_scratch
    else:
        (hff, hfb, stf, stb), yf, yb = out_and_scratch, None, None
    c = pl.program_id(1)

    @pl.when(c == 0)
    def _():
        stf[...] = hf[0]
        stb[...] = hb[0]

    refs = ((xf, bf, cf, yf, stf), (xb, bb, cb_, yb, stb))
    per_step = xf.shape[1] // CHUNK
    for ci in range(per_step):
        dirs = (_ssd_prepare(df, af, False, ci * CHUNK), _ssd_prepare(db, ab, True, (per_step - 1 - ci) * CHUNK))
        for step in range(CHUNK // SSD_SUB):
            for d in dirs:
                _ssd_block_rows(d, step)
            for g in range(N_GROUPS):
                for d, r in zip(dirs, refs):
                    _ssd_group_begin(d, r, g)
                for pr_i in range(HEADS_PER_GROUP // 2):
                    for d, r in zip(dirs, refs):
                        _ssd_pair(d, r, g, pr_i)
                for d, r in zip(dirs, refs):
                    _ssd_group_end(d, r, g)

    @pl.when(c == nchunks - 1)
    def _():
        hff[0] = stf[...]
        hfb[0] = stb[...]


def _ssd_scan(xs, bm, cm, dt, alog_pad, h0f, h0b, emit_y=True):
    bn, length, _ = xs.shape
    rows = CHUNK * min(SSD_CHUNKS_PER_STEP, length // CHUNK)
    nchunks = length // rows
    st_shape = (N_GROUPS, D_STATE, GROUP_COLS)
    st_spec = pl.BlockSpec((1,) + st_shape, lambda b, c: (b, 0, 0, 0))

    def specs(reverse):
        cc = (lambda c: nchunks - 1 - c) if reverse else (lambda c: c)
        di = 1 if reverse else 0
        tok = lambda width: pl.BlockSpec((1, rows, width), lambda b, c: (b, cc(c), 0))
        ins = [tok(D_INNER), tok(BC_WIDTH), tok(BC_WIDTH),
               pl.BlockSpec((1, rows, 128), lambda b, c: (b, cc(c), di)),
               pl.BlockSpec((1, 1, 128), lambda b, c: (di, 0, 0)), st_spec]
        return ins, ([tok(D_INNER)] if emit_y else []) + [st_spec]

    in_f, out_f = specs(False)
    in_b, out_b = specs(True)
    y_shape = [jax.ShapeDtypeStruct((bn, length, D_INNER), BF16)] if emit_y else []
    h_shape = [jax.ShapeDtypeStruct((bn,) + st_shape, F32)]
    return pl.pallas_call(
        functools.partial(_ssd_kernel, nchunks=nchunks, emit_y=emit_y),
        out_shape=y_shape + h_shape + y_shape + h_shape,
        grid=(bn, nchunks),
        in_specs=in_f + in_b,
        out_specs=out_f + out_b,
        scratch_shapes=[pltpu.VMEM(st_shape, F32), pltpu.VMEM(st_shape, F32)],
        compiler_params=_cparams(2),
        name="ssd_scan",
    )(xs, bm, cm, dt, alog_pad, h0f, xs, bm, cm, dt, alog_pad, h0b)


MIX_SUB = 128
MIX_TILE = 512


def _split_bf16(v):
    hi = v.astype(BF16)
    return hi, (v - hi.astype(F32)).astype(BF16)


def _mix_kernel(yf_ref, yb_ref, xs_ref, z_ref, f_ref, g_ref, x_ref, g1_ref, sh2_ref, sc2_ref,
                dsk_ref, nssd_ref, npost_ref, npre_ref, wso_ref, wf_ref, wo_ref, wr_ref,
                hx_ref, hm_ref, aff_ref):
    dot = functools.partial(jnp.dot, preferred_element_type=F32)
    wr_hi, wr_lo = _split_bf16(wr_ref[...])
    subs = [slice(r0, r0 + MIX_SUB) for r0 in range(0, x_ref.shape[1], MIX_SUB)]
    ys = []
    for rs in subs:
        y = (yf_ref[0, rs].astype(F32) + yb_ref[0, rs].astype(F32)
             + dsk_ref[...] * xs_ref[0, rs].astype(F32))
        ys.append(_rms(y * z_ref[0, rs].astype(F32), nssd_ref[...]).astype(BF16))
    s_branches = [dot(y, wso_ref[...]) for y in ys]
    f_branches = [dot(f_ref[0, rs].astype(BF16), wf_ref[...]) for rs in subs]
    merged = []
    for rs, s_branch, f_branch in zip(subs, s_branches, f_branches):
        gates = g_ref[0, rs].astype(F32)
        merged.append((gates[:, :D_MODEL] * f_branch + gates[:, D_MODEL:] * s_branch).astype(BF16))
    mixes = [dot(m, wo_ref[...]) for m in merged]
    hms = []
    for rs, mix in zip(subs, mixes):
        hx = x_ref[0, rs] + g1_ref[0] * _rms(mix, npost_ref[...])
        hx_ref[0, rs] = hx
        hm = _rms(hx, npre_ref[...]) * (1.0 + sc2_ref[0]) + sh2_ref[0]
        hm_ref[0, rs] = _to_row_tiles(hm)
        hms.append(hm)
    for rs, hm in zip(subs, hms):
        hm_hi, hm_lo = _split_bf16(hm)
        logits = dot(hm_hi, wr_hi) + (dot(hm_hi, wr_lo) + dot(hm_lo, wr_hi))
        real = lax.broadcasted_iota(jnp.int32, logits.shape, 1) < N_EXPERTS
        logits = jnp.where(real, logits, -1e30)
        e = jnp.exp(logits - jnp.max(logits, axis=-1, keepdims=True))
        aff = e / jnp.sum(e, axis=-1, keepdims=True)
        aff_ref[0, :, rs] = aff.T[:N_EXPERTS]


def _mix(yf, yb, xs, z, f, gates, x, g1, sh2, sc2, consts, tm):
    bn, length, d = x.shape
    tok = lambda width: pl.BlockSpec((1, tm, width), lambda b, i: (b, i, 0))
    vec = pl.BlockSpec((1, 1, d), lambda b, i: (b, 0, 0))
    return pl.pallas_call(
        _mix_kernel,
        out_shape=[jax.ShapeDtypeStruct((bn, length, d), F32),
                   jax.ShapeDtypeStruct((bn, length) + ROW_TILE, F32),
                   jax.ShapeDtypeStruct((bn, N_EXPERTS, length), F32)],
        grid=(bn, length // tm),
        in_specs=[tok(D_INNER), tok(D_INNER), tok(D_INNER), tok(D_INNER), tok(F_WIDTH), tok(2 * D_MODEL), tok(d),
                  vec, vec, vec] + [_const_spec(c.shape) for c in consts],
        out_specs=[tok(d), pl.BlockSpec((1, tm) + ROW_TILE, lambda b, i: (b, i, 0, 0)),
                   pl.BlockSpec((1, N_EXPERTS, tm), lambda b, i: (b, 0, i))],
        compiler_params=_cparams(2, VMEM_LIMIT),
        name="merge_out_router",
    )(yf, yb, xs, z, f, gates, x, g1, sh2, sc2, *consts)


def _to_tile_rows(v):
    return jnp.concatenate([v[:, k * LANES:(k + 1) * LANES] for k in range(v.shape[1] // LANES)], axis=0)


def _route_kernel(aff_ref, idx_ref, val_ref, *, cap):
    I32 = jnp.int32
    a = aff_ref[0]
    ne, length = a.shape
    nt = length // LANES
    rows = nt * ne

    def search(i, t):
        cand = t | lax.shift_left(jnp.int32(1), 30 - i)
        cnt = jnp.sum((a >= pltpu.bitcast(cand, F32)).astype(F32), axis=1, keepdims=True)
        return jnp.where(cnt >= cap, cand, t)

    thr = pltpu.bitcast(lax.fori_loop(0, 31, search, jnp.zeros((ne, 1), I32)), F32)
    n_gt = jnp.sum((a > thr).astype(F32), axis=1, keepdims=True)
    need = cap - n_gt

    val_r = _to_tile_rows(a)
    tile_rows = lambda v: jnp.concatenate([v] * nt, axis=0)
    thr_r = tile_rows(thr)
    need_r = tile_rows(need)
    li = lax.broadcasted_iota(I32, (LANES, LANES), 0)
    lj = lax.broadcasted_iota(I32, (LANES, LANES), 1)
    upper = (li < lj).astype(BF16)
    ri = lax.broadcasted_iota(I32, (rows, rows), 0)
    rj = lax.broadcasted_iota(I32, (rows, rows), 1)
    earlier = (((ri & (ne - 1)) == (rj & (ne - 1))) & (rj < ri)).astype(BF16)

    def prefix(mask):
        local = jnp.dot(mask.astype(BF16), upper, preferred_element_type=F32)
        total = jnp.sum(mask.astype(F32), axis=1, keepdims=True)
        tot_b = jnp.broadcast_to(total, (rows, LANES)).astype(BF16)
        offs = jnp.dot(earlier, tot_b, preferred_element_type=F32)[:, 0:1]
        return local, offs

    eq = val_r == thr_r
    eq_local, eq_off = prefix(eq)
    sel = (val_r > thr_r) | (eq & (eq_local + eq_off < need_r))
    sel_local, sel_off = prefix(sel)

    lane = lax.broadcasted_iota(I32, (rows, LANES), 1)
    k_row = lax.shift_right_logical(lax.broadcasted_iota(I32, (rows, LANES), 0), int(np.log2(ne)))
    tok = jnp.where(sel, k_row * LANES + lane, -1)
    val = val_r
    dist = jnp.where(sel, lane - sel_local.astype(I32), 0)
    for s in range(int(np.log2(LANES))):
        sh = 1 << s
        mv = (tok >= 0) & ((lax.shift_right_logical(dist, s) & 1) == 1)
        in_tok = pltpu.roll(jnp.where(mv, tok, -1), LANES - sh, axis=1)
        in_val = pltpu.roll(val, LANES - sh, axis=1)
        in_dist = pltpu.roll(dist, LANES - sh, axis=1)
        arrive = in_tok >= 0
        stay = (tok >= 0) & jnp.logical_not(mv)
        tok = jnp.where(arrive, in_tok, jnp.where(stay, tok, -1))
        val = jnp.where(arrive, in_val, val)
        dist = jnp.where(arrive, in_dist, dist)

    off = sel_off.astype(I32)
    lane_e = lax.broadcasted_iota(I32, (ne, LANES), 1)
    ncol = cap // LANES
    out_tok = [jnp.zeros((ne, LANES), I32) for _ in range(ncol)]
    out_val = [jnp.zeros((ne, LANES), F32) for _ in range(ncol)]
    for k in range(nt):
        rs = slice(k * ne, (k + 1) * ne)
        o = off[rs]
        sh, col = o & (LANES - 1), lax.shift_right_logical(o, int(np.log2(LANES)))
        pat = (lane_e - sh) & (LANES - 1)
        r_tok = jnp.take_along_axis(tok[rs], pat, axis=1)
        r_val = jnp.take_along_axis(val[rs], pat, axis=1)
        ok = r_tok >= 0
        for c in range(ncol):
            here = ok & (((col == c) & (lane_e >= sh)) | ((col + 1 == c) & (lane_e < sh)))
            out_tok[c] = jnp.where(here, r_tok, out_tok[c])
            out_val[c] = jnp.where(here, r_val, out_val[c])
    idx_ref[0] = jnp.concatenate(out_tok, axis=1)
    val_ref[0] = jnp.concatenate(out_val, axis=1)


def _route(aff_t, cap):
    bn, ne, length = aff_t.shape
    assert ne & (ne - 1) == 0 and cap % LANES == 0 and length % LANES == 0
    return pl.pallas_call(
        functools.partial(_route_kernel, cap=cap),
        out_shape=[jax.ShapeDtypeStruct((bn, ne, cap), jnp.int32), jax.ShapeDtypeStruct((bn, ne, cap), F32)],
        grid=(bn,),
        in_specs=[pl.BlockSpec((1, ne, length), lambda b: (b, 0, 0))],
        out_specs=[pl.BlockSpec((1, ne, cap), lambda b: (b, 0, 0))] * 2,
        compiler_params=_cparams(1),
        name="expert_choice_route",
    )(aff_t)


def _expert_kernel(idx_ref, idxn_ref, hm_ref, wg_ref, wu_ref, wd_ref, ye_ref, buf0, buf1, wgb, wub, wdb, sem,
                   *, cap, nb, nsteps):
    s = pl.program_id(0)
    bufs = (buf0, buf1)
    dot = functools.partial(jnp.dot, preferred_element_type=F32)

    def row_copy(k, j, b, r):
        return pltpu.make_async_copy(hm_ref.at[b, pl.ds(r, 1)], bufs[k].at[pl.ds(j, 1)], sem.at[k])

    def wait_rows(k):
        pltpu.make_async_copy(hm_ref.at[0, pl.ds(0, cap)], bufs[k], sem.at[k]).wait()

    b0 = (2 * s) % nb

    @pl.when(b0 == 0)
    def _():
        wgb[...] = wg_ref[0].astype(BF16)
        wub[...] = wu_ref[0].astype(BF16)
        wdb[...] = wd_ref[0].astype(BF16)

    @pl.when(s == 0)
    def _():
        def issue(j, carry):
            row_copy(0, j, b0, idx_ref[0, 0, j]).start()
            return carry
        lax.fori_loop(0, cap, issue, 0)

    for k in range(2):
        wait_rows(k)
        x = _from_row_tiles(bufs[k][...]).astype(BF16)
        if k == 0:
            for j in range(cap):
                row_copy(1, j, b0 + 1, idx_ref[1, 0, j]).start(priority=j % 2)
            for j in range(cap):
                row_copy(0, j, (b0 + 2) % nb, idxn_ref[0, 0, j]).start(priority=j % 2)
        hid = (_silu(dot(x, wgb[...])) * dot(x, wub[...])).astype(BF16)
        ye_ref[k] = _to_row_tiles(dot(hid, wdb[...])).astype(BF16).reshape((cap // 2,) + PAIR_TILE)

    @pl.when(s == nsteps - 1)
    def _():
        wait_rows(0)


def _experts(idx, hm, wg, wu, wd):
    bn = hm.shape[0]
    ne, d, ff = wg.shape
    cap = idx.shape[-1]
    per = bn // 2
    nsteps = ne * per
    cur = lambda s: (s, 0, 0)
    nxt = lambda s: ((s + 1) % nsteps, 0, 0)
    wspec = pl.BlockSpec((1, d, ff), lambda s: (s // per, 0, 0))
    return pl.pallas_call(
        functools.partial(_expert_kernel, cap=cap, nb=bn, nsteps=nsteps),
        out_shape=jax.ShapeDtypeStruct((ne * bn, cap // 2) + PAIR_TILE, BF16),
        grid=(nsteps,),
        in_specs=[pl.BlockSpec((2, 1, cap), cur, memory_space=pltpu.SMEM),
                  pl.BlockSpec((2, 1, cap), nxt, memory_space=pltpu.SMEM),
                  pl.BlockSpec(memory_space=pl.ANY),
                  wspec, wspec, pl.BlockSpec((1, ff, d), lambda s: (s // per, 0, 0))],
        out_specs=pl.BlockSpec((2, cap // 2) + PAIR_TILE, lambda s: (s, 0, 0, 0)),
        scratch_shapes=[pltpu.VMEM((cap,) + ROW_TILE, F32), pltpu.VMEM((cap,) + ROW_TILE, F32),
                        pltpu.VMEM((d, ff), BF16), pltpu.VMEM((d, ff), BF16), pltpu.VMEM((ff, d), BF16),
                        pltpu.SemaphoreType.DMA((2,))],
        compiler_params=_cparams(1, VMEM_LIMIT),
        name="expert_ffn",
    )(idx, idx, hm, wg, wu, wd)


COMBINE_ROWS = 16


COMBINE_EXPERTS = 2


def _combine_kernel(*refs, cap, nb, rows_out):
    n = COMBINE_EXPERTS
    idx_refs, val_refs, ye_refs = refs[:n], refs[n:2 * n], refs[2 * n:3 * n]
    hx_ref, g2_ref, w_ref, o_ref, acc = refs[3 * n:]
    b = pl.program_id(0)
    e2 = pl.program_id(1)
    slot = b % 2
    acc_cur = acc.at[slot]

    @pl.when(jnp.logical_and(e2 == 0, b < nb))
    def _():
        acc_cur[...] = jnp.zeros(acc.shape[1:], F32)

    @pl.when(b < nb)
    def _():
        for idx_ref, val_ref, ye_ref in zip(idx_refs, val_refs, ye_refs):
            def body(i, carry, idx_ref=idx_ref, val_ref=val_ref, ye_ref=ye_ref):
                j0 = i * COMBINE_ROWS
                rows = [idx_ref[0, 0, j0 + u] for u in range(COMBINE_ROWS)]
                pairs = [ye_ref[0, i * (COMBINE_ROWS // 2) + p].astype(F32) for p in range(COMBINE_ROWS // 2)]
                ye = [pairs[u // 2][(u % 2) * ROW_TILE[0]:(u % 2 + 1) * ROW_TILE[0]] for u in range(COMBINE_ROWS)]
                new = [acc_cur[rows[u]] + ye[u] * val_ref[0, 0, j0 + u] for u in range(COMBINE_ROWS)]
                for u in range(COMBINE_ROWS):
                    acc_cur[rows[u]] = new[u]
                return carry

            lax.fori_loop(0, cap // COMBINE_ROWS, body, 0)

    @pl.when(b > 0)
    def _():
        f3 = acc[1 - slot, pl.ds(pl.multiple_of(e2 * rows_out, rows_out), rows_out)]
        ms = jnp.mean(f3 * f3, axis=(1, 2), keepdims=True)
        y = _from_row_tiles(f3 * lax.rsqrt(ms + RMS_EPS) * w_ref[...])
        o_ref[0] = hx_ref[0] + g2_ref[0] * y


def _combine_residual(idx, vals, ye, hx, g2, w):
    bn, length, d = hx.shape
    n = COMBINE_EXPERTS
    steps = N_EXPERTS // n
    cap = idx.shape[-1]
    rows_out = length // steps
    w3 = w.reshape((1,) + ROW_TILE)
    cur = lambda b: jnp.minimum(b, bn - 1)
    prev = lambda b: jnp.maximum(b - 1, 0)
    item = lambda k: (lambda b, e2: ((e2 * n + k) * bn + cur(b), 0, 0))
    item4 = lambda k: (lambda b, e2: ((e2 * n + k) * bn + cur(b), 0, 0, 0))
    smem = [pl.BlockSpec((1, 1, cap), item(k), memory_space=pltpu.SMEM) for k in range(n)]
    tok_prev = pl.BlockSpec((1, rows_out, d), lambda b, e2: (prev(b), jnp.where(b > 0, e2, 0), 0))
    return pl.pallas_call(
        functools.partial(_combine_kernel, cap=cap, nb=bn, rows_out=rows_out),
        out_shape=jax.ShapeDtypeStruct((bn, length, d), F32),
        grid=(bn + 1, steps),
        in_specs=smem + smem + [pl.BlockSpec((1, cap // 2) + PAIR_TILE, item4(k)) for k in range(n)]
                 + [tok_prev, pl.BlockSpec((1, 1, d), lambda b, e2: (prev(b), 0, 0)), _const_spec(w3.shape)],
        out_specs=tok_prev,
        scratch_shapes=[pltpu.VMEM((2, length) + ROW_TILE, F32)],
        compiler_params=_cparams(2, VMEM_LIMIT),
        name="expert_combine_residual",
    )(*([idx] * n + [vals] * n + [ye] * n), hx, g2, w3)


def kernel(x, c, ctx, c_ctx, w_mod, b_mod, norm_mix_pre, norm_mix_post, norm_ffn_pre, norm_ffn_post,
           w_in, conv_w, conv_b, dt_bias, a_log, d_skip, ssd_norm, w_fourier, w_ssd_out, b_gate, w_out,
           w_router, w_e_gate, w_e_up, w_e_down):
    bn, length, d = x.shape
    ctx_len = ctx.shape[1]
    l = 0
    row = lambda v: v.reshape(1, -1)

    cs = jnp.concatenate([c, c_ctx[None], jnp.zeros((7, d), F32)], axis=0)
    mods = _modulation(cs, w_mod[l], b_mod[l])
    mod_x = mods[:bn].reshape(bn, N_MOD, 1, d)
    mod_c = jnp.broadcast_to(mods[bn].reshape(1, N_MOD, 1, d), (bn, N_MOD, 1, d))
    sh1, sc1, g1, sh2, sc2, g2 = [mod_x[:, i] for i in range(N_MOD)]

    i1 = F_WIDTH
    i2 = i1 + D_INNER
    i3 = i2 + D_INNER
    i4 = i3 + BC_WIDTH
    i5 = i4 + BC_WIDTH
    i6 = i5 + 2 * N_HEADS
    wi = w_in[l]
    wdt = jnp.zeros((d, 256), F32).at[:, :N_HEADS].set(wi[:, i5:i5 + N_HEADS])
    wdt = wdt.at[:, 128:128 + N_HEADS].set(wi[:, i5 + N_HEADS:i6])
    dtb = jnp.zeros((1, 256), F32).at[0, :N_HEADS].set(dt_bias[l, 0]).at[0, 128:128 + N_HEADS].set(dt_bias[l, 1])
    dft_c, w1, w2 = _dft_constants()
    cw, cb = conv_w[l], conv_b[l]
    j1, j2 = D_INNER, D_INNER + BC_WIDTH
    w = dict(
        wuf=wi[:, :i1].astype(BF16), wz=wi[:, i1:i2].astype(BF16), wxs=wi[:, i2:i3].astype(BF16),
        wb=wi[:, i3:i4].astype(BF16), wc=wi[:, i4:i5].astype(BF16), wdt=wdt.astype(BF16),
        wg=wi[:, i6:].astype(BF16), dft=dft_c,
        cwx=cw[:, :j1], cbx=row(cb[:j1]), cwb=cw[:, j1:j2], cbb=row(cb[j1:j2]),
        cwc=cw[:, j2:], cbc=row(cb[j2:]), dtb=dtb, bg=row(b_gate[l]))
    gain_pre = row(norm_mix_pre[l])

    alog_pad = jnp.zeros((2, 1, 128), F32).at[:, 0, :N_HEADS].set(a_log[l])

    cxs, cbm, ccm, cdt = _in_projection(ctx, mod_c[:, 0], mod_c[:, 1], gain_pre, w, ctx_len, ctx_len, False)
    h_zero = jnp.zeros((bn, N_GROUPS, D_STATE, GROUP_COLS), F32)
    hc_f, hc_b = _ssd_scan(cxs, cbm, ccm, cdt, alog_pad, h_zero, h_zero, emit_y=False)

    zr, zi, z, xs, bm, cm, dt, gates = _in_projection(x, sh1, sc1, gain_pre, w, GRID_W, INPROJ_TILE, True)
    f = _sequence_dft_real(zr, zi, w1, w2)
    yf, _, yb, _ = _ssd_scan(xs, bm, cm, dt, alog_pad, hc_f, hc_b)

    consts = [row(jnp.repeat(d_skip[l], HEAD_DIM)), row(ssd_norm[l]), row(norm_mix_post[l]),
              row(norm_ffn_pre[l]), w_ssd_out[l].astype(BF16), w_fourier[l].astype(BF16),
              w_out[l].astype(BF16), jnp.pad(w_router[l], ((0, 0), (0, LANES - N_EXPERTS)))]
    hx, hm, aff_t = _mix(yf, yb, xs, z, f, gates, x, g1, sh2, sc2, consts, MIX_TILE)

    cap = EC_FACTOR * length // N_EXPERTS
    top_idx, top_aff = _route(aff_t, cap)
    idx = jnp.swapaxes(top_idx, 0, 1).reshape(N_EXPERTS * bn, 1, cap)
    vals = jnp.swapaxes(top_aff, 0, 1).reshape(N_EXPERTS * bn, 1, cap)
    ye = _experts(idx, hm, w_e_gate[l], w_e_up[l], w_e_down[l])
    return _combine_residual(idx, vals, ye, hx, g2, row(norm_ffn_post[l]))
```
